```python
import math
import jax
import jax.numpy as jnp
from jax import lax
import numpy as np

D_MODEL = 2048
BATCH = 2
SEQ = 4096
DEPTH = 4
DEC_BATCH = 16
DEC_SEQ = 64
PAST_LEN = 1024

CHUNK = 64
N_MIXERS = 2
N_LRU = (DEPTH + 1) // 2
N_ATTN = DEPTH // 2
LRU_W = D_MODEL
LRU_HEADS = 8
LRU_BLK = LRU_W // LRU_HEADS
LRU_CONV = 4
LRU_C = 8.0
DIFF_HEADS = 8
DK = D_MODEL // 16
DV = 2 * DK
ATT_QK_W = DIFF_HEADS * 2 * DK
ATT_V_W = DIFF_HEADS * DV
Q_BLOCK = 128
SWEEP_MIN_KEYS = 2048
MEM_TOKENS = 256
MEM_HEADS = 4
MEM_HD = D_MODEL // 8
MEM_W = MEM_HEADS * MEM_HD
D_FF = 5632
FFN_CONV = 3
RMS_EPS = 1e-6

kernel_name = 'hybrid_rglru_diffattn_stream_step'


def rms_norm(x, g, eps=RMS_EPS):
    xf = x.astype(jnp.float32)
    xf = xf * lax.rsqrt(jnp.mean(xf * xf, axis=-1, keepdims=True) + eps)
    return (xf * g.astype(jnp.float32)).astype(x.dtype)


def causal_dwconv(x, buf, w, b):
    width = w.shape[0]
    T = x.shape[1]
    xp = jnp.concatenate([buf.astype(x.dtype), x], axis=1)
    y = b + xp[:, 0:T] * w[0]
    for j in range(1, width):
        y = y + xp[:, j:j + T] * w[j]
    return y, xp[:, T:]


def linear_scan(a, b, h0):
    b = b.at[:, 0].add(a[:, 0] * h0)
    def combine(left, right):
        a_l, b_l = left
        a_r, b_r = right
        return a_l * a_r, a_r * b_l + b_r
    _, hs = lax.associative_scan(combine, (a, b), axis=1)
    return hs


def rg_lru(x, h0, w_a, b_a, w_x, b_x, lam):
    B, T, C = x.shape
    xf = x.astype(jnp.float32)
    xb = xf.reshape(B, T, LRU_HEADS, LRU_BLK)
    r = jax.nn.sigmoid(jnp.einsum('bthi,hij->bthj', xb, w_a.astype(jnp.float32)).reshape(B, T, C) + b_a.astype(jnp.float32))
    ig = jax.nn.sigmoid(jnp.einsum('bthi,hij->bthj', xb, w_x.astype(jnp.float32)).reshape(B, T, C) + b_x.astype(jnp.float32))
    log_a = LRU_C * r * jax.nn.log_sigmoid(lam.astype(jnp.float32))
    a = jnp.exp(log_a)
    gated = jnp.sqrt(-jnp.expm1(2.0 * log_a)) * (ig * xf)
    hs = linear_scan(a, gated, h0.astype(jnp.float32))
    return hs.astype(x.dtype), hs[:, -1].astype(x.dtype)


def diff_attention(q, k, v, q_pos, k_pos, lam):
    B, Tq = q.shape[0], q.shape[1]
    kf = k.astype(jnp.float32)
    vf = v.astype(jnp.float32)
    k_chunk = k_pos // CHUNK
    def attend(qb, qp):
        s = jnp.einsum('bqhcd,bkhcd->bhcqk', qb.astype(jnp.float32), kf) * (DK ** -0.5)
        visible = k_chunk[None, :] <= (qp // CHUNK)[:, None]
        s = jnp.where(visible, s, -jnp.inf)
        pr = jax.nn.softmax(s, axis=-1)
        wgt = pr[:, :, 0] - lam * pr[:, :, 1]
        return jnp.einsum('bhqk,bkhd->bqhd', wgt, vf)
    if k.shape[1] >= SWEEP_MIN_KEYS and Tq > Q_BLOCK and Tq % Q_BLOCK == 0:
        nb = Tq // Q_BLOCK
        qb = q.reshape(B, nb, Q_BLOCK, DIFF_HEADS, 2, DK).swapaxes(0, 1)
        pb = q_pos.reshape(nb, Q_BLOCK)
        o = lax.map(lambda args: attend(args[0], args[1]), (qb, pb))
        return o.swapaxes(0, 1).reshape(B, Tq, DIFF_HEADS, DV)
    return attend(q, q_pos)


def mem_attention(q, mk, mv):
    s = jnp.einsum('bthd,bmhd->bhtm', q.astype(jnp.float32), mk.astype(jnp.float32)) * (MEM_HD ** -0.5)
    pr = jax.nn.softmax(s, axis=-1)
    return jnp.einsum('bhtm,bmhd->bthd', pr, mv.astype(jnp.float32))


def mem_kv(mem, g_norm, w_kv, g_k):
    B, M, _ = mem.shape
    kv = rms_norm(mem, g_norm) @ w_kv
    k, v = jnp.split(kv, 2, axis=-1)
    k = rms_norm(k.reshape(B, M, MEM_HEADS, MEM_HD), g_k)
    return k, v.reshape(B, M, MEM_HEADS, MEM_HD)


def run_trunk(x, past_len, mem_k, mem_v, lru_h, lru_conv, attn_k_past, attn_v_past, ffn_conv, p):
    B, T, _ = x.shape
    q_pos = past_len + jnp.arange(T, dtype=jnp.int32)
    k_pos = jnp.arange(past_len + T, dtype=jnp.int32)
    new_h, new_lconv, new_k, new_v, new_fconv = [], [], [], [], []
    for i in range(DEPTH):
        j = i // N_MIXERS
        h = rms_norm(x, p['norm_mix'][i])
        if i % N_MIXERS == 0:
            z = h @ p['lru_w_in'][j]
            gate, xr, mq = jnp.split(z, [LRU_W, 2 * LRU_W], axis=-1)
            xc, buf = causal_dwconv(xr, lru_conv[j], p['lru_conv_w'][j], p['lru_conv_b'][j])
            hs, h_last = rg_lru(xc, lru_h[j], p['lru_gate_a_w'][j], p['lru_gate_a_b'][j],
                                p['lru_gate_x_w'][j], p['lru_gate_x_b'][j], p['lru_lambda'][j])
            mix = jax.nn.gelu(gate) * hs
            w_out = p['lru_w_out'][j]
            new_h.append(h_last)
            new_lconv.append(buf)
        else:
            z = h @ p['attn_w_in'][j]
            q, k, v, mq = jnp.split(z, [ATT_QK_W, 2 * ATT_QK_W, 2 * ATT_QK_W + ATT_V_W], axis=-1)
            q = rms_norm(q.reshape(B, T, DIFF_HEADS, 2, DK), p['attn_q_norm'][j])
            k = rms_norm(k.reshape(B, T, DIFF_HEADS, 2, DK), p['attn_k_norm'][j])
            v = v.reshape(B, T, DIFF_HEADS, DV)
            new_k.append(k)
            new_v.append(v)
            if attn_k_past is not None:
                k = jnp.concatenate([attn_k_past[j].astype(k.dtype), k], axis=1)
                v = jnp.concatenate([attn_v_past[j].astype(v.dtype), v], axis=1)
            lam_init = 0.8 - 0.6 * math.exp(-0.3 * i)
            lp = p['attn_lambda'][j].astype(jnp.float32)
            lam = jnp.exp(jnp.sum(lp[0] * lp[1])) - jnp.exp(jnp.sum(lp[2] * lp[3])) + lam_init
            o = diff_attention(q, k, v, q_pos, k_pos, lam)
            o = rms_norm(o, p['attn_subln'][j]) * (1.0 - lam_init)
            mix = o.reshape(B, T, ATT_V_W).astype(x.dtype)
            w_out = p['attn_w_out'][j]
        mq = rms_norm(mq.reshape(B, T, MEM_HEADS, MEM_HD), p['mem_q_norm'][i])
        mo = mem_attention(mq, mem_k[i], mem_v[i]).reshape(B, T, MEM_W).astype(x.dtype)
        x = x + jnp.concatenate([mix.astype(x.dtype), mo], axis=-1) @ w_out
        h = rms_norm(x, p['norm_ffn'][i])
        up, fbuf = causal_dwconv(h @ p['ffn_w_up'][i], ffn_conv[i], p['ffn_conv_w'][i], p['ffn_conv_b'][i])
        g, u = jnp.split(up, 2, axis=-1)
        x = x + (jax.nn.gelu(g) * u) @ p['ffn_w_down'][i]
        new_fconv.append(fbuf)
    return (x, jnp.stack(new_h), jnp.stack(new_lconv), jnp.stack(new_k), jnp.stack(new_v), jnp.stack(new_fconv))


def setup_inputs(seed: int = 0) -> dict:
    key = jax.random.key(seed)
    keys = iter(jax.random.split(key, 64))
    def normal(shape, scale=1.0):
        return scale * jax.random.normal(next(keys), shape, jnp.float32)
    def gain(shape):
        return 1.0 + 0.05 * jax.random.normal(next(keys), shape, jnp.float32)
    s_lam = jax.random.uniform(next(keys), (N_LRU, LRU_W), jnp.float32, 0.9, 0.999) ** (1.0 / LRU_C)
    lru_lambda = jnp.log(s_lam) - jnp.log1p(-s_lam)
    return {
        'x_prompt': normal((BATCH, SEQ, D_MODEL)),
        'x_sample': normal((DEC_BATCH, DEC_SEQ, D_MODEL)),
        'mem_prompt': normal((BATCH, MEM_TOKENS, D_MODEL)),
        'cache_attn_k': normal((N_ATTN, DEC_BATCH, PAST_LEN, DIFF_HEADS, 2, DK)),
        'cache_attn_v': normal((N_ATTN, DEC_BATCH, PAST_LEN, DIFF_HEADS, DV)),
        'cache_mem_k': normal((DEPTH, DEC_BATCH, MEM_TOKENS, MEM_HEADS, MEM_HD)),
        'cache_mem_v': normal((DEPTH, DEC_BATCH, MEM_TOKENS, MEM_HEADS, MEM_HD)),
        'state_lru_h': normal((N_LRU, DEC_BATCH, LRU_W), 0.5),
        'state_lru_conv': normal((N_LRU, DEC_BATCH, LRU_CONV - 1, LRU_W)),
        'state_ffn_conv': normal((DEPTH, DEC_BATCH, FFN_CONV - 1, 2 * D_FF)),
        'norm_mix': gain((DEPTH, D_MODEL)),
        'norm_ffn': gain((DEPTH, D_MODEL)),
        'lru_w_in': normal((N_LRU, D_MODEL, 2 * LRU_W + MEM_W), D_MODEL ** -0.5),
        'lru_conv_w': normal((N_LRU, LRU_CONV, LRU_W), LRU_CONV ** -0.5),
        'lru_conv_b': normal((N_LRU, LRU_W), 0.02),
        'lru_gate_a_w': normal((N_LRU, LRU_HEADS, LRU_BLK, LRU_BLK), LRU_BLK ** -0.5),
        'lru_gate_a_b': normal((N_LRU, LRU_W), 0.02),
        'lru_gate_x_w': normal((N_LRU, LRU_HEADS, LRU_BLK, LRU_BLK), LRU_BLK ** -0.5),
        'lru_gate_x_b': normal((N_LRU, LRU_W), 0.02),
        'lru_lambda': lru_lambda,
        'lru_w_out': normal((N_LRU, LRU_W + MEM_W, D_MODEL), (LRU_W + MEM_W) ** -0.5),
        'attn_w_in': normal((N_ATTN, D_MODEL, 2 * ATT_QK_W + ATT_V_W + MEM_W), D_MODEL ** -0.5),
        'attn_q_norm': gain((N_ATTN, DK)),
        'attn_k_norm': gain((N_ATTN, DK)),
        'attn_lambda': normal((N_ATTN, 4, DK), 0.1),
        'attn_subln': gain((N_ATTN, DV)),
        'attn_w_out': normal((N_ATTN, ATT_V_W + MEM_W, D_MODEL), (ATT_V_W + MEM_W) ** -0.5),
        'mem_norm': gain((DEPTH, D_MODEL)),
        'mem_w_kv': normal((DEPTH, D_MODEL, 2 * MEM_W), D_MODEL ** -0.5),
        'mem_q_norm': gain((DEPTH, MEM_HD)),
        'mem_k_norm': gain((DEPTH, MEM_HD)),
        'ffn_w_up': normal((DEPTH, D_MODEL, 2 * D_FF), D_MODEL ** -0.5),
        'ffn_conv_w': normal((DEPTH, FFN_CONV, 2 * D_FF), FFN_CONV ** -0.5),
        'ffn_conv_b': normal((DEPTH, 2 * D_FF), 0.02),
        'ffn_w_down': normal((DEPTH, D_FF, D_MODEL), D_FF ** -0.5),
    }


def reference(x_prompt, x_sample, mem_prompt, cache_attn_k, cache_attn_v, cache_mem_k, cache_mem_v,
              state_lru_h, state_lru_conv, state_ffn_conv,
              norm_mix, norm_ffn, lru_w_in, lru_conv_w, lru_conv_b, lru_gate_a_w, lru_gate_a_b,
              lru_gate_x_w, lru_gate_x_b, lru_lambda, lru_w_out, attn_w_in, attn_q_norm, attn_k_norm,
              attn_lambda, attn_subln, attn_w_out, mem_norm, mem_w_kv, mem_q_norm, mem_k_norm,
              ffn_w_up, ffn_conv_w, ffn_conv_b, ffn_w_down):
    p = {
        'norm_mix': norm_mix, 'norm_ffn': norm_ffn,
        'lru_w_in': lru_w_in, 'lru_conv_w': lru_conv_w, 'lru_conv_b': lru_conv_b,
        'lru_gate_a_w': lru_gate_a_w, 'lru_gate_a_b': lru_gate_a_b,
        'lru_gate_x_w': lru_gate_x_w, 'lru_gate_x_b': lru_gate_x_b,
        'lru_lambda': lru_lambda, 'lru_w_out': lru_w_out,
        'attn_w_in': attn_w_in, 'attn_q_norm': attn_q_norm, 'attn_k_norm': attn_k_norm,
        'attn_lambda': attn_lambda, 'attn_subln': attn_subln, 'attn_w_out': attn_w_out,
        'mem_q_norm': mem_q_norm,
        'ffn_w_up': ffn_w_up, 'ffn_conv_w': ffn_conv_w, 'ffn_conv_b': ffn_conv_b, 'ffn_w_down': ffn_w_down,
    }
    B = x_prompt.shape[0]
    dt = x_prompt.dtype
    mks, mvs = [], []
    for i in range(DEPTH):
        mk, mv = mem_kv(mem_prompt, mem_norm[i], mem_w_kv[i], mem_k_norm[i])
        mks.append(mk)
        mvs.append(mv)
    p_mem_k = jnp.stack(mks)
    p_mem_v = jnp.stack(mvs)
    zeros_h = jnp.zeros((N_LRU, B, LRU_W), dt)
    zeros_lconv = jnp.zeros((N_LRU, B, LRU_CONV - 1, LRU_W), dt)
    zeros_fconv = jnp.zeros((DEPTH, B, FFN_CONV - 1, 2 * D_FF), dt)
    y_prompt, p_lru_h, p_lru_conv, p_attn_k, p_attn_v, p_ffn_conv = run_trunk(
        x_prompt, 0, p_mem_k, p_mem_v, zeros_h, zeros_lconv, None, None, zeros_fconv, p)
    past_len = cache_attn_k.shape[2]
    y_sample, s_lru_h, s_lru_conv, s_attn_k, s_attn_v, s_ffn_conv = run_trunk(
        x_sample, past_len, cache_mem_k, cache_mem_v, state_lru_h, state_lru_conv,
        cache_attn_k, cache_attn_v, state_ffn_conv, p)
    return (y_prompt, y_sample, p_lru_h, p_lru_conv, p_attn_k, p_attn_v, p_mem_k, p_mem_v, p_ffn_conv,
            s_lru_h, s_lru_conv, s_attn_k, s_attn_v, s_ffn_conv)
```

```python
import functools
import math

import jax
import jax.numpy as jnp
from jax import lax
from jax.experimental import pallas as pl
from jax.experimental.pallas import tpu as pltpu

F32 = jnp.float32
BF16 = jnp.bfloat16

CHUNK = 64
LRU_HEADS = 8
LRU_C = 8.0
DIFF_HEADS = 8
MEM_HEADS = 4
RMS_EPS = 1e-6
NEG_BIG = -1e30

V7X_VMEM_LIMIT_BYTES = 56 * 1024 * 1024


def _cparams(n_grid_axes):
    return pltpu.CompilerParams(
        dimension_semantics=("arbitrary",) * n_grid_axes,
        vmem_limit_bytes=V7X_VMEM_LIMIT_BYTES,
    )


def _dot(a, b):
    return jnp.dot(a, b, preferred_element_type=F32)


def _dot_nt(a, b):
    return lax.dot_general(a, b, (((1,), (1,)), ((), ())), preferred_element_type=F32)


def _rms(x, g):
    return x * lax.rsqrt(jnp.mean(x * x, axis=-1, keepdims=True) + RMS_EPS) * g


def _gelu(x):
    return x * (0.5 * (1.0 + jnp.tanh(0.7978845608028654 * (x + 0.044715 * (x * x * x)))))


def _shifted(seg, prev, n_prev, row):
    out = []
    for d in range(1, n_prev + 1):
        v = pltpu.roll(seg, d, 0)
        for r in range(d - 1, -1, -1):
            v = jnp.where(row == r, prev[n_prev - d + r:n_prev - d + r + 1], v)
        out.append(v)
    return out


def _segment_scan(a, b, seg_len):
    n = a.shape[0]
    row = lax.broadcasted_iota(jnp.int32, (n, 1), 0) & (seg_len - 1)
    sh = 1
    while sh < seg_len:
        valid = row >= sh
        a_sh = pltpu.roll(a, sh, 0)
        b_sh = pltpu.roll(b, sh, 0)
        b = jnp.where(valid, b + a * b_sh, b)
        a = jnp.where(valid, a * a_sh, a)
        sh *= 2
    return a, b


def _mem_attn(q, gq, mk_ref, mv_ref, n_seg, seg_len):
    hd = q.shape[-1]
    qn = _rms(q, gq).astype(BF16)
    outs = []
    for s in range(n_seg):
        k = mk_ref[s].astype(BF16)
        v = mv_ref[s].astype(BF16)
        sc = _dot_nt(qn[s * seg_len:(s + 1) * seg_len], k) * (hd ** -0.5)
        p = jnp.exp(sc - jnp.max(sc, axis=-1, keepdims=True))
        pr = p / jnp.sum(p, axis=-1, keepdims=True)
        outs.append(_dot(pr.astype(BF16), v))
    return outs[0] if n_seg == 1 else jnp.concatenate(outs, axis=0)


class _Tiling:
    def __init__(self, n_seq, t, bm):
        self.n_seq, self.t = n_seq, t
        if t >= bm:
            assert t % bm == 0
            self.n_seg, self.seg_len, self.tiles_per_seq = 1, bm, t // bm
        else:
            assert bm % t == 0 and n_seq % (bm // t) == 0
            self.n_seg, self.seg_len, self.tiles_per_seq = bm // t, t, 1
        self.bm = self.n_seg * self.seg_len
        self.n_tiles = n_seq * t // self.bm
        assert self.seg_len & (self.seg_len - 1) == 0 and self.seg_len >= 8


def _mem_kv_kernel(mem_ref, gn_ref, wk_ref, wv_ref, gk_ref, k_ref, v_ref, hn_ref):
    @pl.when(pl.program_id(1) == 0)
    def _():
        hn_ref[...] = _rms(mem_ref[...], gn_ref[...]).astype(BF16)

    hn = hn_ref[...]
    k_ref[...] = _rms(_dot(hn, wk_ref[...]), gk_ref[...])
    v_ref[...] = _dot(hn, wv_ref[...])


def _mem_kv(mem2d, mem_norm, w_kv, mem_k_norm):
    depth, d, two_w = w_kv.shape
    mem_w = two_w // 2
    hd = mem_w // MEM_HEADS
    m = mem2d.shape[0]
    return pl.pallas_call(
        _mem_kv_kernel,
        grid=(depth, MEM_HEADS),
        in_specs=[
            pl.BlockSpec((m, d), lambda l, j: (0, 0)),
            pl.BlockSpec((None, 1, d), lambda l, j: (l, 0, 0)),
            pl.BlockSpec((None, d, hd), lambda l, j: (l, 0, j)),
            pl.BlockSpec((None, d, hd), lambda l, j: (l, 0, MEM_HEADS + j)),
            pl.BlockSpec((None, 1, hd), lambda l, j: (l, 0, 0)),
        ],
        out_specs=[
            pl.BlockSpec((None, m, hd), lambda l, j: (l, 0, j)),
            pl.BlockSpec((None, m, hd), lambda l, j: (l, 0, j)),
        ],
        out_shape=[jax.ShapeDtypeStruct((depth, m, mem_w), F32)] * 2,
        scratch_shapes=[pltpu.VMEM((m, d), BF16)],
        compiler_params=_cparams(2),
        name="mem_kv",
    )(mem2d, mem_norm.reshape(depth, 1, d), w_kv, w_kv, mem_k_norm.reshape(depth, 1, hd))


def _lru_in_kernel(x_ref, gn_ref, wg_ref, wx_ref, cw_ref, cb_ref, wa_ref, wi_ref, ba_ref, bi_ref,
                   lam_ref, h0_ref, c0_ref, gq_ref, mk_ref, mv_ref,
                   mix_ref, mo_ref, hn_out_ref, cn_out_ref,
                   hn_ref, hc_ref, cc_ref, *, n_seg, seg_len, tiles_per_seq):
    i = pl.program_id(0)
    j = pl.program_id(1)
    carried = tiles_per_seq > 1

    @pl.when(j == 0)
    def _():
        hn_ref[...] = _rms(x_ref[...], gn_ref[...]).astype(BF16)

    @pl.when(j < LRU_HEADS)
    def _():
        hn = hn_ref[...]
        gate = _dot(hn, wg_ref[...])
        xr = _dot(hn, wx_ref[...])
        if carried:
            @pl.when(i % tiles_per_seq == 0)
            def _():
                cc_ref[j] = c0_ref[0]
                hc_ref[j] = h0_ref[0]

        cw = cw_ref[...]
        cb = cb_ref[...]
        n_prev = cw.shape[0] - 1
        row = lax.broadcasted_iota(jnp.int32, (seg_len, 1), 0)
        xcs = []
        for s in range(n_seg):
            seg = xr[s * seg_len:(s + 1) * seg_len]
            prev = cc_ref[j] if carried else c0_ref[s]
            delayed = _shifted(seg, prev, n_prev, row)
            xc = cb + delayed[n_prev - 1] * cw[0:1]
            for t in range(1, n_prev):
                xc = xc + delayed[n_prev - 1 - t] * cw[t:t + 1]
            xc = xc + seg * cw[n_prev:n_prev + 1]
            tail = seg[seg_len - n_prev:seg_len]
            cn_out_ref[s] = tail
            if carried:
                cc_ref[j] = tail
            xcs.append(xc)
        xc = xcs[0] if n_seg == 1 else jnp.concatenate(xcs, axis=0)

        xcb = xc.astype(BF16)
        r = jax.nn.sigmoid(_dot(xcb, wa_ref[...]) + ba_ref[...])
        ig = jax.nn.sigmoid(_dot(xcb, wi_ref[...]) + bi_ref[...])
        log_a = (LRU_C * r) * jax.nn.log_sigmoid(lam_ref[...])
        a = jnp.exp(log_a)
        one_minus_a2 = -jnp.tanh(log_a) * (a * a + 1.0)
        gated = jnp.sqrt(one_minus_a2) * (ig * xc)
        cum_a, cum_b = _segment_scan(a, gated, seg_len)
        hss = []
        for s in range(n_seg):
            h0 = hc_ref[j] if carried else h0_ref[s]
            sl = slice(s * seg_len, (s + 1) * seg_len)
            hs = cum_b[sl] + cum_a[sl] * h0
            last = hs[seg_len - 1:seg_len]
            hn_out_ref[s] = last
            if carried:
                hc_ref[j] = last
            hss.append(hs)
        hs = hss[0] if n_seg == 1 else jnp.concatenate(hss, axis=0)
        mix_ref[...] = (_gelu(gate) * hs).astype(BF16)

    @pl.when(j >= LRU_HEADS)
    def _():
        q = _dot(hn_ref[...], wx_ref[...])
        mo_ref[...] = _mem_attn(q, gq_ref[...], mk_ref, mv_ref, n_seg, seg_len).astype(BF16)


def _lru_in(x2d, tl, gn, w_in, conv_w, conv_b, wa, wi, ba, bi, lam, h0, c0, gq, mk, mv):
    m, d = x2d.shape
    hw = d // LRU_HEADS
    n_steps = LRU_HEADS + MEM_HEADS
    ns, tps = tl.n_seg, tl.tiles_per_seq
    width = conv_w.shape[0]
    last = LRU_HEADS - 1

    def hcol(j):
        return jnp.minimum(j, last)

    def mcol(j):
        return jnp.maximum(j - LRU_HEADS, 0)

    kern = functools.partial(_lru_in_kernel, n_seg=ns, seg_len=tl.seg_len, tiles_per_seq=tps)
    return pl.pallas_call(
        kern,
        grid=(tl.n_tiles, n_steps),
        in_specs=[
            pl.BlockSpec((tl.bm, d), lambda i, j: (i, 0)),
            pl.BlockSpec((1, d), lambda i, j: (0, 0)),
            pl.BlockSpec((d, hw), lambda i, j: (0, hcol(j))),
            pl.BlockSpec((d, hw), lambda i, j: (0, LRU_HEADS + j)),
            pl.BlockSpec((width, hw), lambda i, j: (0, hcol(j))),
            pl.BlockSpec((1, hw), lambda i, j: (0, hcol(j))),
            pl.BlockSpec((None, hw, hw), lambda i, j: (hcol(j), 0, 0)),
            pl.BlockSpec((None, hw, hw), lambda i, j: (hcol(j), 0, 0)),
            pl.BlockSpec((1, hw), lambda i, j: (0, hcol(j))),
            pl.BlockSpec((1, hw), lambda i, j: (0, hcol(j))),
            pl.BlockSpec((1, hw), lambda i, j: (0, hcol(j))),
            pl.BlockSpec((ns, 1, hw), lambda i, j: (i // tps, 0, hcol(j))),
            pl.BlockSpec((ns, width - 1, hw), lambda i, j: (i // tps, 0, hcol(j))),
            pl.BlockSpec((1, hw), lambda i, j: (0, 0)),
            pl.BlockSpec((ns, mk.shape[1], hw), lambda i, j: (i // tps, 0, mcol(j))),
            pl.BlockSpec((ns, mv.shape[1], hw), lambda i, j: (i // tps, 0, mcol(j))),
        ],
        out_specs=[
            pl.BlockSpec((tl.bm, hw), lambda i, j: (i, hcol(j))),
            pl.BlockSpec((tl.bm, hw), lambda i, j: (i, mcol(j))),
            pl.BlockSpec((None, ns, 1, hw), lambda i, j: (i, 0, 0, hcol(j))),
            pl.BlockSpec((None, ns, width - 1, hw), lambda i, j: (i, 0, 0, hcol(j))),
        ],
        out_shape=[
            jax.ShapeDtypeStruct((m, d), BF16),
            jax.ShapeDtypeStruct((m, MEM_HEADS * hw), BF16),
            jax.ShapeDtypeStruct((tl.n_tiles, ns, 1, d), F32),
            jax.ShapeDtypeStruct((tl.n_tiles, ns, width - 1, d), F32),
        ],
        scratch_shapes=[
            pltpu.VMEM((tl.bm, d), BF16),
            pltpu.VMEM((LRU_HEADS, 1, hw), F32),
            pltpu.VMEM((LRU_HEADS, width - 1, hw), F32),
        ],
        compiler_params=_cparams(2),
        name="lru_in",
    )(x2d, gn, w_in, w_in, conv_w, conv_b, wa, wi, ba, bi, lam, h0, c0, gq, mk, mv)


def _att_in_kernel(x_ref, gn_ref, w_ref, qg_ref, kg_ref, gq_ref, mk_ref, mv_ref,
                   qb_ref, kf_ref, kb_ref, vf_ref, vb_ref, mo_ref, hn_ref, *, n_seg, seg_len):
    j = pl.program_id(1)
    h = DIFF_HEADS

    @pl.when(j == 0)
    def _():
        hn_ref[...] = _rms(x_ref[...], gn_ref[...]).astype(BF16)

    z = _dot(hn_ref[...], w_ref[...])
    dk = qg_ref.shape[-1]

    @pl.when(j < h)
    def _():
        for c in range(z.shape[-1] // dk):
            sl = slice(c * dk, (c + 1) * dk)
            qb_ref[:, sl] = _rms(z[:, sl], qg_ref[...]).astype(BF16)

    @pl.when((j >= h) & (j < 2 * h))
    def _():
        for c in range(z.shape[-1] // dk):
            sl = slice(c * dk, (c + 1) * dk)
            kn = _rms(z[:, sl], kg_ref[...])
            kf_ref[:, sl] = kn
            kb_ref[:, sl] = kn.astype(BF16)

    @pl.when((j >= 2 * h) & (j < 3 * h))
    def _():
        vf_ref[...] = z
        vb_ref[...] = z.astype(BF16)

    @pl.when(j >= 3 * h)
    def _():
        mo_ref[...] = _mem_attn(z, gq_ref[...], mk_ref, mv_ref, n_seg, seg_len).astype(BF16)


def _att_in(x2d, tl, gn, w_in, qg, kg, gq, mk, mv):
    m, d = x2d.shape
    h = DIFF_HEADS
    hw = d // h
    n_steps = 3 * h + MEM_HEADS
    ns, tps = tl.n_seg, tl.tiles_per_seq
    dk = qg.shape[-1]

    def clampcol(j, lo, n):
        return jnp.clip(j - lo, 0, n - 1)

    kern = functools.partial(_att_in_kernel, n_seg=ns, seg_len=tl.seg_len)
    col = lambda lo, n: (lambda i, j: (i, clampcol(j, lo, n)))
    return pl.pallas_call(
        kern,
        grid=(tl.n_tiles, n_steps),
        in_specs=[
            pl.BlockSpec((tl.bm, d), lambda i, j: (i, 0)),
            pl.BlockSpec((1, d), lambda i, j: (0, 0)),
            pl.BlockSpec((d, hw), lambda i, j: (0, j)),
            pl.BlockSpec((1, dk), lambda i, j: (0, 0)),
            pl.BlockSpec((1, dk), lambda i, j: (0, 0)),
            pl.BlockSpec((1, hw), lambda i, j: (0, 0)),
            pl.BlockSpec((ns, mk.shape[1], hw), lambda i, j: (i // tps, 0, clampcol(j, 3 * h, MEM_HEADS))),
            pl.BlockSpec((ns, mv.shape[1], hw), lambda i, j: (i // tps, 0, clampcol(j, 3 * h, MEM_HEADS))),
        ],
        out_specs=[
            pl.BlockSpec((tl.bm, hw), col(0, h)),
            pl.BlockSpec((tl.bm, hw), col(h, h)),
            pl.BlockSpec((tl.bm, hw), col(h, h)),
            pl.BlockSpec((tl.bm, hw), col(2 * h, h)),
            pl.BlockSpec((tl.bm, hw), col(2 * h, h)),
            pl.BlockSpec((tl.bm, hw), col(3 * h, MEM_HEADS)),
        ],
        out_shape=[
            jax.ShapeDtypeStruct((m, d), BF16),
            jax.ShapeDtypeStruct((m, d), F32),
            jax.ShapeDtypeStruct((m, d), BF16),
            jax.ShapeDtypeStruct((m, d), F32),
            jax.ShapeDtypeStruct((m, d), BF16),
            jax.ShapeDtypeStruct((m, MEM_HEADS * hw), BF16),
        ],
        scratch_shapes=[pltpu.VMEM((tl.bm, d), BF16)],
        compiler_params=_cparams(2),
        name="att_in",
    )(x2d, gn, w_in, qg, kg, gq, mk, mv)


def _attn_block(q, k, v, m_ref, l_ref, acc_ref, mask):
    dk = q.shape[-1] // 2
    for c in range(2):
        sl = slice(c * dk, (c + 1) * dk)
        s = _dot_nt(q[:, sl], k[:, sl]) * (dk ** -0.5)
        if mask is not None:
            s = jnp.where(mask, s, NEG_BIG)
        m_old = m_ref[c]
        m_new = jnp.maximum(m_old, jnp.max(s, axis=-1, keepdims=True))
        alpha = jnp.exp(m_old - m_new)
        p = jnp.exp(s - m_new)
        l_ref[c] = alpha * l_ref[c] + jnp.sum(p, axis=-1, keepdims=True)
        acc_ref[c] = alpha * acc_ref[c] + _dot(p.astype(BF16), v)
        m_ref[c] = m_new


def _attn_init(m_ref, l_ref, acc_ref):
    m_ref[...] = jnp.full(m_ref.shape, NEG_BIG, F32)
    l_ref[...] = jnp.zeros(l_ref.shape, F32)
    acc_ref[...] = jnp.zeros(acc_ref.shape, F32)


def _attn_finish(lp_ref, sg_ref, l_ref, acc_ref, lam_init):
    lp = lp_ref[...]
    lam = (jnp.exp(jnp.sum(lp[0:1] * lp[1:2], axis=-1, keepdims=True))
           - jnp.exp(jnp.sum(lp[2:3] * lp[3:4], axis=-1, keepdims=True)) + lam_init)
    o = acc_ref[0] / l_ref[0] - lam * (acc_ref[1] / l_ref[1])
    return (_rms(o, sg_ref[...]) * (1.0 - lam_init)).astype(BF16)


def _attn_prompt_kernel(lp_ref, sg_ref, q_ref, k_ref, v_ref, o_ref, m_ref, l_ref, acc_ref, *, tq, lam_init):
    qi = pl.program_id(2)
    _attn_init(m_ref, l_ref, acc_ref)
    q = q_ref[...]

    def body(kv, carry):
        start = pl.multiple_of(kv * tq, tq)
        _attn_block(q, k_ref[pl.ds(start, tq), :], v_ref[pl.ds(start, tq), :], m_ref, l_ref, acc_ref, None)
        return carry

    lax.fori_loop(0, qi, body, 0)
    start = pl.multiple_of(qi * tq, tq)
    shift = CHUNK.bit_length() - 1
    rowc = lax.shift_right_logical(lax.broadcasted_iota(jnp.int32, (tq, tq), 0), shift)
    colc = lax.shift_right_logical(lax.broadcasted_iota(jnp.int32, (tq, tq), 1), shift)
    _attn_block(q, k_ref[pl.ds(start, tq), :], v_ref[pl.ds(start, tq), :], m_ref, l_ref, acc_ref, colc <= rowc)
    o_ref[...] = _attn_finish(lp_ref, sg_ref, l_ref, acc_ref, lam_init)


def _attn_prompt(qb, kb, vb, lp, sg, n_seq, t, tq, lam_init):
    d = qb.shape[-1]
    hw = d // DIFF_HEADS
    assert t % tq == 0 and tq % CHUNK == 0
    q3, k3, v3 = (a.reshape(n_seq, t, d) for a in (qb, kb, vb))
    kern = functools.partial(_attn_prompt_kernel, tq=tq, lam_init=lam_init)
    out = pl.pallas_call(
        kern,
        grid=(n_seq, DIFF_HEADS, t // tq),
        in_specs=[
            pl.BlockSpec(lp.shape, lambda b, h, qi: (0, 0)),
            pl.BlockSpec((1, hw), lambda b, h, qi: (0, 0)),
            pl.BlockSpec((None, tq, hw), lambda b, h, qi: (b, qi, h)),
            pl.BlockSpec((None, t, hw), lambda b, h, qi: (b, 0, h)),
            pl.BlockSpec((None, t, hw), lambda b, h, qi: (b, 0, h)),
        ],
        out_specs=pl.BlockSpec((None, tq, hw), lambda b, h, qi: (b, qi, h)),
        out_shape=jax.ShapeDtypeStruct((n_seq, t, d), BF16),
        scratch_shapes=[
            pltpu.VMEM((2, tq, 1), F32),
            pltpu.VMEM((2, tq, 1), F32),
            pltpu.VMEM((2, tq, hw), F32),
        ],
        compiler_params=_cparams(3),
        name="attn_prompt",
    )(lp, sg, q3, k3, v3)
    return out.reshape(n_seq * t, d)


def _attn_cached_kernel(lp_ref, sg_ref, q_ref, kp_ref, vp_ref, kn_ref, vn_ref, o_ref,
                        m_ref, l_ref, acc_ref, *, lam_init):
    _attn_init(m_ref, l_ref, acc_ref)
    q = q_ref[...]
    _attn_block(q, kp_ref[...].astype(BF16), vp_ref[...].astype(BF16), m_ref, l_ref, acc_ref, None)
    _attn_block(q, kn_ref[...], vn_ref[...], m_ref, l_ref, acc_ref, None)
    o_ref[...] = _attn_finish(lp_ref, sg_ref, l_ref, acc_ref, lam_init)


def _attn_cached(qb, kb, vb, k_past, v_past, lp, sg, n_seq, t, lam_init):
    d = qb.shape[-1]
    hw = d // DIFF_HEADS
    past = k_past.shape[1]
    assert past % CHUNK == 0 and t <= CHUNK
    q3, k3, v3 = (a.reshape(n_seq, t, d) for a in (qb, kb, vb))
    kern = functools.partial(_attn_cached_kernel, lam_init=lam_init)
    new_spec = pl.BlockSpec((None, t, hw), lambda b, h: (b, 0, h))
    past_spec = pl.BlockSpec((None, past, hw), lambda b, h: (b, 0, h))
    out = pl.pallas_call(
        kern,
        grid=(n_seq, DIFF_HEADS),
        in_specs=[
            pl.BlockSpec(lp.shape, lambda b, h: (0, 0)),
            pl.BlockSpec((1, hw), lambda b, h: (0, 0)),
            new_spec, past_spec, past_spec, new_spec, new_spec,
        ],
        out_specs=new_spec,
        out_shape=jax.ShapeDtypeStruct((n_seq, t, d), BF16),
        scratch_shapes=[
            pltpu.VMEM((2, t, 1), F32),
            pltpu.VMEM((2, t, 1), F32),
            pltpu.VMEM((2, t, hw), F32),
        ],
        compiler_params=_cparams(2),
        name="attn_cached",
    )(lp, sg, q3, k_past.reshape(n_seq, past, d), v_past.reshape(n_seq, past, d), k3, v3)
    return out.reshape(n_seq * t, d)


def _out_kernel(x_ref, mix_ref, mo_ref, w1_ref, w2_ref, o_ref):
    o_ref[...] = x_ref[...] + (_dot(mix_ref[...], w1_ref[...]) + _dot(mo_ref[...], w2_ref[...]))


def _out_proj(x2d, mix, mo, w_out, bm, bn):
    m, d = x2d.shape
    k1, k2 = mix.shape[1], mo.shape[1]
    assert k1 % k2 == 0
    return pl.pallas_call(
        _out_kernel,
        grid=(m // bm, d // bn),
        in_specs=[
            pl.BlockSpec((bm, bn), lambda i, j: (i, j)),
            pl.BlockSpec((bm, k1), lambda i, j: (i, 0)),
            pl.BlockSpec((bm, k2), lambda i, j: (i, 0)),
            pl.BlockSpec((k1, bn), lambda i, j: (0, j)),
            pl.BlockSpec((k2, bn), lambda i, j: (k1 // k2, j)),
        ],
        out_specs=pl.BlockSpec((bm, bn), lambda i, j: (i, j)),
        out_shape=jax.ShapeDtypeStruct((m, d), F32),
        compiler_params=_cparams(2),
        name="out_proj",
    )(x2d, mix, mo, w_out, w_out)


def _ffn_kernel(x_ref, gn_ref, wup_ref, cw_ref, cb_ref, st_ref, wdn_ref, o_ref, nst_ref,
                hn_ref, carry_ref, *, n_seg, seg_len, tiles_per_seq):
    i = pl.program_id(0)
    j = pl.program_id(1)
    carried = tiles_per_seq > 1

    @pl.when(j == 0)
    def _():
        x = x_ref[...]
        hn_ref[...] = _rms(x, gn_ref[...]).astype(BF16)
        o_ref[...] = x

    up = _dot(hn_ref[...], wup_ref[...])
    if carried:
        @pl.when(i % tiles_per_seq == 0)
        def _():
            carry_ref[j] = st_ref[0]

    cw = cw_ref[...]
    cb = cb_ref[...]
    n_prev = cw.shape[0] - 1
    row = lax.broadcasted_iota(jnp.int32, (seg_len, 1), 0)
    ys = []
    for s in range(n_seg):
        seg = up[s * seg_len:(s + 1) * seg_len]
        prev = carry_ref[j] if carried else st_ref[s]
        delayed = _shifted(seg, prev, n_prev, row)
        y = cb + delayed[n_prev - 1] * cw[0:1]
        for t in range(1, n_prev):
            y = y + delayed[n_prev - 1 - t] * cw[t:t + 1]
        y = y + seg * cw[n_prev:n_prev + 1]
        tail = seg[seg_len - n_prev:seg_len]
        nst_ref[s] = tail
        if carried:
            carry_ref[j] = tail
        ys.append(y)
    y = ys[0] if n_seg == 1 else jnp.concatenate(ys, axis=0)
    bf = y.shape[-1] // 2
    act = (_gelu(y[:, :bf]) * y[:, bf:]).astype(BF16)
    o_ref[...] += _dot(act, wdn_ref[...])


def _ffn(x2d, tl, gn, wup_r, cw_r, cb_r, st_r, wdn, bf):
    m, d = x2d.shape
    dff = wdn.shape[0]
    nj = dff // bf
    ns, tps = tl.n_seg, tl.tiles_per_seq
    width = cw_r.shape[0]
    kern = functools.partial(_ffn_kernel, n_seg=ns, seg_len=tl.seg_len, tiles_per_seq=tps)
    return pl.pallas_call(
        kern,
        grid=(tl.n_tiles, nj),
        in_specs=[
            pl.BlockSpec((tl.bm, d), lambda i, j: (i, 0)),
            pl.BlockSpec((1, d), lambda i, j: (0, 0)),
            pl.BlockSpec((d, 2 * bf), lambda i, j: (0, j)),
            pl.BlockSpec((width, 2 * bf), lambda i, j: (0, j)),
            pl.BlockSpec((1, 2 * bf), lambda i, j: (0, j)),
            pl.BlockSpec((ns, width - 1, 2 * bf), lambda i, j: (i // tps, 0, j)),
            pl.BlockSpec((bf, d), lambda i, j: (j, 0)),
        ],
        out_specs=[
            pl.BlockSpec((tl.bm, d), lambda i, j: (i, 0)),
            pl.BlockSpec((None, ns, width - 1, 2 * bf), lambda i, j: (i, 0, 0, j)),
        ],
        out_shape=[
            jax.ShapeDtypeStruct((m, d), F32),
            jax.ShapeDtypeStruct((tl.n_tiles, ns, width - 1, 2 * dff), F32),
        ],
        scratch_shapes=[
            pltpu.VMEM((tl.bm, d), BF16),
            pltpu.VMEM((nj, width - 1, 2 * bf), F32),
        ],
        compiler_params=_cparams(2),
        name="ffn",
    )(x2d, gn, wup_r, cw_r, cb_r, st_r, wdn)


def _interleave_cols(a, bf):
    dff = a.shape[-1] // 2
    lead = a.shape[:-1]
    a = a.reshape(lead + (2, dff // bf, bf))
    return jnp.swapaxes(a, -3, -2).reshape(lead + (2 * dff,))


def _deinterleave_cols(a, bf):
    dff = a.shape[-1] // 2
    lead = a.shape[:-1]
    a = a.reshape(lead + (dff // bf, 2, bf))
    return jnp.swapaxes(a, -3, -2).reshape(lead + (2 * dff,))


def _last_tile_state(per_tile, tl):
    rows, c = per_tile.shape[2:]
    return per_tile.reshape(tl.n_seq, tl.tiles_per_seq, rows, c)[:, -1]


def _run_trunk(x, mem_k, mem_v, lru_h, lru_conv, attn_past, ffn_conv, p, cfg):
    n_seq, t, d = x.shape
    depth = p["norm_mix"].shape[0]
    x2d = x.reshape(n_seq * t, d)
    tl = _Tiling(n_seq, t, cfg["bm"])
    tl_ffn = _Tiling(n_seq, t, cfg["bm_ffn"])
    bf = cfg["bf"]
    mem_tokens = mem_k.shape[2]
    new_h, new_lconv, new_k, new_v, new_fconv = [], [], [], [], []
    for i in range(depth):
        j = i // 2
        mk = mem_k[i].reshape(n_seq, mem_tokens, -1)
        mv = mem_v[i].reshape(n_seq, mem_tokens, -1)
        gq = p["mem_q_norm"][i][None]
        gn = p["norm_mix"][i][None]
        if i % 2 == 0:
            mix, mo, h_last, buf = _lru_in(
                x2d, tl, gn, p["lru_w_in"][j], p["lru_conv_w"][j], p["lru_conv_b"][j][None],
                p["lru_gate_a_w"][j], p["lru_gate_x_w"][j], p["lru_gate_a_b"][j][None],
                p["lru_gate_x_b"][j][None], p["lru_lambda"][j][None],
                lru_h[j][:, None, :], lru_conv[j], gq, mk, mv)
            new_h.append(_last_tile_state(h_last, tl)[:, 0, :])
            new_lconv.append(_last_tile_state(buf, tl))
            w_out = p["lru_w_out"][j]
        else:
            qb, kf, kb, vf, vb, mo = _att_in(
                x2d, tl, gn, p["attn_w_in"][j], p["attn_q_norm"][j][None], p["attn_k_norm"][j][None],
                gq, mk, mv)
            new_k.append(kf)
            new_v.append(vf)
            lam_init = 0.8 - 0.6 * math.exp(-0.3 * i)
            lp = p["attn_lambda"][j]
            sg = p["attn_subln"][j][None]
            if attn_past is None:
                mix = _attn_prompt(qb, kb, vb, lp, sg, n_seq, t, cfg["tq"], lam_init)
            else:
                mix = _attn_cached(qb, kb, vb, attn_past[0][j], attn_past[1][j], lp, sg, n_seq, t, lam_init)
            w_out = p["attn_w_out"][j]
        x2d = _out_proj(x2d, mix, mo, w_out, cfg["bm_out"], cfg["bn_out"])
        x2d, fbuf = _ffn(x2d, tl_ffn, p["norm_ffn"][i][None], p["ffn_w_up"][i], p["ffn_conv_w"][i],
                         p["ffn_conv_b"][i][None], _interleave_cols(ffn_conv[i], bf), p["ffn_w_down"][i], bf)
        new_fconv.append(_deinterleave_cols(_last_tile_state(fbuf, tl_ffn), bf))
    return x2d.reshape(n_seq, t, d), new_h, new_lconv, new_k, new_v, new_fconv


def _forward(x_prompt, x_sample, mem_prompt, cache_attn_k, cache_attn_v, cache_mem_k, cache_mem_v,
             state_lru_h, state_lru_conv, state_ffn_conv, p, mem_norm, mem_w_kv, mem_k_norm, cfg_p, cfg_s):
    b, t, d = x_prompt.shape
    depth = p["norm_mix"].shape[0]
    n_lru, n_attn = (depth + 1) // 2, depth // 2
    bf = cfg_p["bf"]
    assert cfg_s["bf"] == bf

    p = dict(p)
    for name in ("lru_w_in", "lru_gate_a_w", "lru_gate_x_w", "lru_w_out", "attn_w_in", "attn_w_out",
                 "ffn_w_down"):
        p[name] = p[name].astype(BF16)
    p["ffn_w_up"] = _interleave_cols(p["ffn_w_up"], bf).astype(BF16)
    p["ffn_conv_w"] = _interleave_cols(p["ffn_conv_w"], bf)
    p["ffn_conv_b"] = _interleave_cols(p["ffn_conv_b"], bf)

    mem_tokens = mem_prompt.shape[1]
    mk, mv = _mem_kv(mem_prompt.reshape(b * mem_tokens, d), mem_norm, mem_w_kv.astype(BF16), mem_k_norm)
    hd = mem_k_norm.shape[-1]
    p_mem_k = mk.reshape(depth, b, mem_tokens, MEM_HEADS, hd)
    p_mem_v = mv.reshape(depth, b, mem_tokens, MEM_HEADS, hd)

    dff2 = state_ffn_conv.shape[-1]
    zeros_h = jnp.zeros((n_lru, b, d), F32)
    zeros_lconv = jnp.zeros((n_lru, b) + state_lru_conv.shape[2:], F32)
    zeros_fconv = jnp.zeros((depth, b, state_ffn_conv.shape[2], dff2), F32)
    yp, ph, plc, pk, pv, pfc = _run_trunk(x_prompt, p_mem_k, p_mem_v, zeros_h, zeros_lconv, None,
                                          zeros_fconv, p, cfg_p)
    ys, sh, slc, sk, sv, sfc = _run_trunk(x_sample, cache_mem_k, cache_mem_v, state_lru_h, state_lru_conv,
                                          (cache_attn_k, cache_attn_v), state_ffn_conv, p, cfg_s)

    dk = p["attn_q_norm"].shape[-1]
    db, dt = x_sample.shape[0], x_sample.shape[1]

    def kshape(a, n, tt):
        return jnp.stack(a).reshape(n_attn, n, tt, DIFF_HEADS, 2, dk)

    def vshape(a, n, tt):
        return jnp.stack(a).reshape(n_attn, n, tt, DIFF_HEADS, 2 * dk)

    return (yp, ys, jnp.stack(ph), jnp.stack(plc), kshape(pk, b, t), vshape(pv, b, t), p_mem_k, p_mem_v,
            jnp.stack(pfc), jnp.stack(sh), jnp.stack(slc), kshape(sk, db, dt), vshape(sv, db, dt),
            jnp.stack(sfc))


CFG_PROMPT = dict(bm=512, bm_ffn=512, bf=512, bm_out=512, bn_out=1024, tq=512)
CFG_SAMPLE = dict(bm=512, bm_ffn=512, bf=512, bm_out=512, bn_out=1024, tq=512)


def kernel(x_prompt, x_sample, mem_prompt, cache_attn_k, cache_attn_v, cache_mem_k, cache_mem_v, state_lru_h, state_lru_conv, state_ffn_conv, norm_mix, norm_ffn, lru_w_in, lru_conv_w, lru_conv_b, lru_gate_a_w, lru_gate_a_b, lru_gate_x_w, lru_gate_x_b, lru_lambda, lru_w_out, attn_w_in, attn_q_norm, attn_k_norm, attn_lambda, attn_subln, attn_w_out, mem_norm, mem_w_kv, mem_q_norm, mem_k_norm, ffn_w_up, ffn_conv_w, ffn_conv_b, ffn_w_down):
    p = {
        "norm_mix": norm_mix, "norm_ffn": norm_ffn,
        "lru_w_in": lru_w_in, "lru_conv_w": lru_conv_w, "lru_conv_b": lru_conv_b,
        "lru_gate_a_w": lru_gate_a_w, "lru_gate_a_b": lru_gate_a_b,
        "lru_gate_x_w": lru_gate_x_w, "lru_gate_x_b": lru_gate_x_b,
        "lru_lambda": lru_lambda, "lru_w_out": lru_w_out,
        "attn_w_in": attn_w_in, "attn_q_norm": attn_q_norm, "attn_k_norm": attn_k_norm,
        "attn_lambda": attn_lambda, "attn_subln": attn_subln, "attn_w_out": attn_w_out,
        "mem_q_norm": mem_q_norm,
        "ffn_w_up": ffn_w_up, "ffn_conv_w": ffn_conv_w, "ffn_conv_b": ffn_conv_b, "ffn_w_down": ffn_w_down,
    }
    return _forward(x_prompt, x_sample, mem_prompt, cache_attn_k, cache_attn_v, cache_mem_k, cache_mem_v,
                    state_lru_h, state_lru_conv, state_ffn_conv, p, mem_norm, mem_w_kv, mem_k_norm,
                    CFG_PROMPT, CFG_SAMPLE)
```

```python
import functools
import math

import jax
import jax.numpy as jnp
from jax import lax
from jax.experimental import pallas as pl
from jax.experimental.pallas import tpu as pltpu

F32 = jnp.float32
BF16 = jnp.bfloat16

CHUNK = 64
LRU_HEADS = 8
LRU_C = 8.0
DIFF_HEADS = 8
MEM_HEADS = 4
RMS_EPS = 1e-6
NEG_BIG = -1e30
SUBLANES = 8

V7X_VMEM_LIMIT_BYTES = 56 * 1024 * 1024


def _cparams(n_grid_axes):
    return pltpu.CompilerParams(
        dimension_semantics=("arbitrary",) * n_grid_axes,
        vmem_limit_bytes=V7X_VMEM_LIMIT_BYTES,
    )


def _dot(a, b):
    return jnp.dot(a, b, preferred_element_type=F32)


def _dot_nt(a, b):
    return lax.dot_general(a, b, (((1,), (1,)), ((), ())), preferred_element_type=F32)


def _rms(x, g):
    return x * lax.rsqrt(jnp.mean(x * x, axis=-1, keepdims=True) + RMS_EPS) * g


def _gelu(x):
    return x * (0.5 * (1.0 + jnp.tanh(0.7978845608028654 * (x + 0.044715 * (x * x * x)))))


def _cat_rows(parts):
    return parts[0] if len(parts) == 1 else jnp.concatenate(parts, axis=0)


def _delayed(seg, prev, d):
    n_prev = prev.shape[0]
    v = pltpu.roll(seg, d, 0)
    row = lax.broadcasted_iota(jnp.int32, (SUBLANES, 1), 0)
    top = v[0:SUBLANES]
    for r in range(d):
        top = jnp.where(row == r, prev[n_prev - d + r:n_prev - d + r + 1], top)
    return jnp.concatenate([top, v[SUBLANES:]], axis=0)


def _causal_conv(seg, prev, cw, cb):
    n_prev = cw.shape[0] - 1
    y = cb + _delayed(seg, prev, n_prev) * cw[0:1]
    for t in range(1, n_prev):
        y = y + _delayed(seg, prev, n_prev - t) * cw[t:t + 1]
    return y + seg * cw[n_prev:n_prev + 1]


def _scan_rows(a, b):
    n = a.shape[0]
    row = lax.broadcasted_iota(jnp.int32, (n, 1), 0)
    sh = 1
    while sh < min(SUBLANES, n):
        valid = row >= sh
        a_sh = pltpu.roll(a, sh, 0)
        b_sh = pltpu.roll(b, sh, 0)
        b = jnp.where(valid, b + a * b_sh, b)
        a = jnp.where(valid, a * a_sh, a)
        sh *= 2
    while sh < n:
        b = jnp.concatenate([b[:sh], b[sh:] + a[sh:] * b[:n - sh]], axis=0)
        a = jnp.concatenate([a[:sh], a[sh:] * a[:n - sh]], axis=0)
        sh *= 2
    return a, b


def _mem_attn(q, gq, mk_ref, mv_ref, n_seg, seg_len):
    hd = gq.shape[-1]
    outs = []
    for h in range(q.shape[-1] // hd):
        cols = slice(h * hd, (h + 1) * hd)
        qn = _rms(q[:, cols], gq).astype(BF16)
        segs = []
        for s in range(n_seg):
            k = mk_ref[s, :, cols].astype(BF16)
            v = mv_ref[s, :, cols].astype(BF16)
            sc = _dot_nt(qn[s * seg_len:(s + 1) * seg_len], k) * (hd ** -0.5)
            p = jnp.exp(sc - jnp.max(sc, axis=-1, keepdims=True))
            pr = p / jnp.sum(p, axis=-1, keepdims=True)
            segs.append(_dot(pr.astype(BF16), v))
        outs.append(_cat_rows(segs))
    return outs[0] if len(outs) == 1 else jnp.concatenate(outs, axis=-1)


class _Tiling:
    def __init__(self, n_seq, t, bm):
        self.n_seq, self.t = n_seq, t
        if t >= bm:
            assert t % bm == 0
            self.n_seg, self.seg_len, self.tiles_per_seq = 1, bm, t // bm
        else:
            assert bm % t == 0 and n_seq % (bm // t) == 0
            self.n_seg, self.seg_len, self.tiles_per_seq = bm // t, t, 1
        self.bm = self.n_seg * self.seg_len
        self.n_tiles = n_seq * t // self.bm
        assert self.seg_len & (self.seg_len - 1) == 0 and self.seg_len >= SUBLANES


def _last_tile_state(per_tile, tl):
    rows, c = per_tile.shape[2:]
    return per_tile.reshape(tl.n_seq, tl.tiles_per_seq, rows, c)[:, -1]


def _resident(block_shape, index_map):
    return pl.BlockSpec(block_shape, index_map, pipeline_mode=pl.Buffered(1))


def _mem_kv_kernel(mem_ref, gn_ref, wk_ref, wv_ref, gk_ref, k_ref, v_ref, hn_ref):
    @pl.when(pl.program_id(1) == 0)
    def _():
        hn_ref[...] = _rms(mem_ref[...], gn_ref[...]).astype(BF16)

    hn = hn_ref[...]
    k_ref[...] = _rms(_dot(hn, wk_ref[...]), gk_ref[...])
    v_ref[...] = _dot(hn, wv_ref[...])


def _mem_kv(mem2d, mem_norm, w_kv, mem_k_norm):
    depth, d, two_w = w_kv.shape
    mem_w = two_w // 2
    hd = mem_w // MEM_HEADS
    m = mem2d.shape[0]
    return pl.pallas_call(
        _mem_kv_kernel,
        grid=(depth, MEM_HEADS),
        in_specs=[
            pl.BlockSpec((m, d), lambda l, j: (0, 0)),
            pl.BlockSpec((None, 1, d), lambda l, j: (l, 0, 0)),
            pl.BlockSpec((None, d, hd), lambda l, j: (l, 0, j)),
            pl.BlockSpec((None, d, hd), lambda l, j: (l, 0, MEM_HEADS + j)),
            pl.BlockSpec((None, 1, hd), lambda l, j: (l, 0, 0)),
        ],
        out_specs=[
            pl.BlockSpec((None, m, hd), lambda l, j: (l, 0, j)),
            pl.BlockSpec((None, m, hd), lambda l, j: (l, 0, j)),
        ],
        out_shape=[jax.ShapeDtypeStruct((depth, m, mem_w), F32)] * 2,
        scratch_shapes=[pltpu.VMEM((m, d), BF16)],
        compiler_params=_cparams(2),
        name="mem_kv",
    )(mem2d, mem_norm, w_kv, w_kv, mem_k_norm)


def _lru_in_kernel(x_ref, gn_ref, wg_ref, wx_ref, cw_ref, cb_ref, wa_ref, wi_ref, ba_ref, bi_ref,
                   lam_ref, h0_ref, c0_ref, gq_ref, mk_ref, mv_ref,
                   mix_ref, mo_ref, hn_out_ref, cn_out_ref,
                   hn_ref, hc_ref, cc_ref, *, n_seg, seg_len, tiles_per_seq):
    i = pl.program_id(0)
    j = pl.program_id(1)
    carried = tiles_per_seq > 1

    @pl.when(j == 0)
    def _():
        hn_ref[...] = _rms(x_ref[...], gn_ref[...]).astype(BF16)

    @pl.when(j < LRU_HEADS)
    def _():
        hn = hn_ref[...]
        gate = _dot(hn, wg_ref[...])
        xr = _dot(hn, wx_ref[...])
        if carried:
            @pl.when(i % tiles_per_seq == 0)
            def _():
                cc_ref[j] = c0_ref[0]
                hc_ref[j] = h0_ref[0]

        cw = cw_ref[...]
        cb = cb_ref[...]
        n_prev = cw.shape[0] - 1
        xcs = []
        for s in range(n_seg):
            seg = xr[s * seg_len:(s + 1) * seg_len]
            xcs.append(_causal_conv(seg, cc_ref[j] if carried else c0_ref[s], cw, cb))
            tail = seg[seg_len - n_prev:seg_len]
            cn_out_ref[s] = tail
            if carried:
                cc_ref[j] = tail
        xc = _cat_rows(xcs)

        xcb = xc.astype(BF16)
        r = jax.nn.sigmoid(_dot(xcb, wa_ref[...]) + ba_ref[...])
        ig = jax.nn.sigmoid(_dot(xcb, wi_ref[...]) + bi_ref[...])
        log_a = (LRU_C * r) * jax.nn.log_sigmoid(lam_ref[...])
        a = jnp.exp(log_a)
        one_minus_a2 = -jnp.tanh(log_a) * (a * a + 1.0)
        gated = jnp.sqrt(one_minus_a2) * (ig * xc)
        hss = []
        for s in range(n_seg):
            sl = slice(s * seg_len, (s + 1) * seg_len)
            cum_a, cum_b = _scan_rows(a[sl], gated[sl])
            hs = cum_b + cum_a * (hc_ref[j] if carried else h0_ref[s])
            last = hs[seg_len - 1:seg_len]
            hn_out_ref[s] = last
            if carried:
                hc_ref[j] = last
            hss.append(hs)
        mix_ref[...] = (_gelu(gate) * _cat_rows(hss)).astype(BF16)

    @pl.when(j >= LRU_HEADS)
    def _():
        q = _dot(hn_ref[...], wx_ref[...])
        mo_ref[...] = _mem_attn(q, gq_ref[...], mk_ref, mv_ref, n_seg, seg_len).astype(BF16)


def _lru_in(x2d, tl, li, mi, p, h0, c0, mk, mv):
    m, d = x2d.shape
    hw = d // LRU_HEADS
    n_steps = LRU_HEADS + MEM_HEADS
    ns, tps = tl.n_seg, tl.tiles_per_seq
    width = p["lru_conv_w"].shape[1]
    last = LRU_HEADS - 1

    def hcol(j):
        return jnp.minimum(j, last)

    def mcol(j):
        return jnp.maximum(j - LRU_HEADS, 0)

    vec = pl.BlockSpec((None, 1, hw), lambda i, j: (mi, 0, hcol(j)))
    gate_w = pl.BlockSpec((None, None, hw, hw), lambda i, j: (mi, hcol(j), 0, 0))
    mem = pl.BlockSpec((None, ns, mk.shape[2], hw), lambda i, j: (li, i // tps, 0, mcol(j)))
    kern = functools.partial(_lru_in_kernel, n_seg=ns, seg_len=tl.seg_len, tiles_per_seq=tps)
    return pl.pallas_call(
        kern,
        grid=(tl.n_tiles, n_steps),
        in_specs=[
            _resident((tl.bm, d), lambda i, j: (i, 0)),
            pl.BlockSpec((None, 1, d), lambda i, j: (li, 0, 0)),
            pl.BlockSpec((None, d, hw), lambda i, j: (mi, 0, hcol(j))),
            pl.BlockSpec((None, d, hw), lambda i, j: (mi, 0, LRU_HEADS + j)),
            pl.BlockSpec((None, width, hw), lambda i, j: (mi, 0, hcol(j))),
            vec, gate_w, gate_w, vec, vec, vec,
            pl.BlockSpec((None, ns, 1, hw), lambda i, j: (mi, i // tps, 0, hcol(j))),
            pl.BlockSpec((None, ns, width - 1, hw), lambda i, j: (mi, i // tps, 0, hcol(j))),
            pl.BlockSpec((None, 1, hw), lambda i, j: (li, 0, 0)),
            mem, mem,
        ],
        out_specs=[
            pl.BlockSpec((tl.bm, hw), lambda i, j: (i, hcol(j))),
            pl.BlockSpec((tl.bm, hw), lambda i, j: (i, mcol(j))),
            pl.BlockSpec((None, ns, 1, hw), lambda i, j: (i, 0, 0, hcol(j))),
            pl.BlockSpec((None, ns, width - 1, hw), lambda i, j: (i, 0, 0, hcol(j))),
        ],
        out_shape=[
            jax.ShapeDtypeStruct((m, d), BF16),
            jax.ShapeDtypeStruct((m, MEM_HEADS * hw), BF16),
            jax.ShapeDtypeStruct((tl.n_tiles, ns, 1, d), F32),
            jax.ShapeDtypeStruct((tl.n_tiles, ns, width - 1, d), F32),
        ],
        scratch_shapes=[
            pltpu.VMEM((tl.bm, d), BF16),
            pltpu.VMEM((LRU_HEADS, 1, hw), F32),
            pltpu.VMEM((LRU_HEADS, width - 1, hw), F32),
        ],
        compiler_params=_cparams(2),
        name="lru_in",
    )(x2d, p["norm_mix"], p["lru_w_in"], p["lru_w_in"], p["lru_conv_w"], p["lru_conv_b"],
      p["lru_gate_a_w"], p["lru_gate_x_w"], p["lru_gate_a_b"], p["lru_gate_x_b"], p["lru_lambda"],
      h0, c0, p["mem_q_norm"], mk, mv)


def _att_in_kernel(x_ref, gn_ref, w_ref, qg_ref, kg_ref, gq_ref, mk_ref, mv_ref,
                   qb_ref, kf_ref, kb_ref, vf_ref, vb_ref, mo_ref, hn_ref, *, n_seg, seg_len, nq):
    j = pl.program_id(1)

    @pl.when(j == 0)
    def _():
        hn_ref[...] = _rms(x_ref[...], gn_ref[...]).astype(BF16)

    z = _dot(hn_ref[...], w_ref[...])
    dk = qg_ref.shape[-1]

    @pl.when(j < nq)
    def _():
        for c in range(z.shape[-1] // dk):
            sl = slice(c * dk, (c + 1) * dk)
            qb_ref[:, sl] = _rms(z[:, sl], qg_ref[...]).astype(BF16)

    @pl.when((j >= nq) & (j < 2 * nq))
    def _():
        for c in range(z.shape[-1] // dk):
            sl = slice(c * dk, (c + 1) * dk)
            kn = _rms(z[:, sl], kg_ref[...])
            kf_ref[:, sl] = kn
            kb_ref[:, sl] = kn.astype(BF16)

    @pl.when((j >= 2 * nq) & (j < 3 * nq))
    def _():
        vf_ref[...] = z
        vb_ref[...] = z.astype(BF16)

    @pl.when(j >= 3 * nq)
    def _():
        mo_ref[...] = _mem_attn(z, gq_ref[...], mk_ref, mv_ref, n_seg, seg_len).astype(BF16)


def _att_in(x2d, tl, li, ai, p, mk, mv, bc):
    m, d = x2d.shape
    mem_w = mk.shape[-1]
    nq, nm = d // bc, mem_w // bc
    n_steps = 3 * nq + nm
    ns, tps = tl.n_seg, tl.tiles_per_seq
    dk = p["attn_q_norm"].shape[-1]
    hd = p["mem_q_norm"].shape[-1]

    def col(lo, n):
        return lambda i, j: (i, jnp.clip(j - lo, 0, n - 1))

    mem = pl.BlockSpec((None, ns, mk.shape[2], bc),
                       lambda i, j: (li, i // tps, 0, jnp.clip(j - 3 * nq, 0, nm - 1)))
    kern = functools.partial(_att_in_kernel, n_seg=ns, seg_len=tl.seg_len, nq=nq)
    return pl.pallas_call(
        kern,
        grid=(tl.n_tiles, n_steps),
        in_specs=[
            _resident((tl.bm, d), lambda i, j: (i, 0)),
            pl.BlockSpec((None, 1, d), lambda i, j: (li, 0, 0)),
            pl.BlockSpec((None, d, bc), lambda i, j: (ai, 0, j)),
            pl.BlockSpec((None, 1, dk), lambda i, j: (ai, 0, 0)),
            pl.BlockSpec((None, 1, dk), lambda i, j: (ai, 0, 0)),
            pl.BlockSpec((None, 1, hd), lambda i, j: (li, 0, 0)),
            mem, mem,
        ],
        out_specs=[
            pl.BlockSpec((tl.bm, bc), col(0, nq)),
            pl.BlockSpec((tl.bm, bc), col(nq, nq)),
            pl.BlockSpec((tl.bm, bc), col(nq, nq)),
            pl.BlockSpec((tl.bm, bc), col(2 * nq, nq)),
            pl.BlockSpec((tl.bm, bc), col(2 * nq, nq)),
            pl.BlockSpec((tl.bm, bc), col(3 * nq, nm)),
        ],
        out_shape=[
            jax.ShapeDtypeStruct((m, d), BF16),
            jax.ShapeDtypeStruct((m, d), F32),
            jax.ShapeDtypeStruct((m, d), BF16),
            jax.ShapeDtypeStruct((m, d), F32),
            jax.ShapeDtypeStruct((m, d), BF16),
            jax.ShapeDtypeStruct((m, mem_w), BF16),
        ],
        scratch_shapes=[pltpu.VMEM((tl.bm, d), BF16)],
        compiler_params=_cparams(2),
        name="att_in",
    )(x2d, p["norm_mix"], p["attn_w_in"], p["attn_q_norm"], p["attn_k_norm"], p["mem_q_norm"], mk, mv)


def _attn_block(q, k, v, m_ref, l_ref, acc_ref, mask):
    dk = q.shape[-1] // 2
    for c in range(2):
        sl = slice(c * dk, (c + 1) * dk)
        s = _dot_nt(q[:, sl], k[:, sl]) * (dk ** -0.5)
        if mask is not None:
            s = jnp.where(mask, s, NEG_BIG)
        m_old = m_ref[c]
        m_new = jnp.maximum(m_old, jnp.max(s, axis=-1, keepdims=True))
        alpha = jnp.exp(m_old - m_new)
        p = jnp.exp(s - m_new)
        l_ref[c] = alpha * l_ref[c] + jnp.sum(p, axis=-1, keepdims=True)
        acc_ref[c] = alpha * acc_ref[c] + _dot(p.astype(BF16), v)
        m_ref[c] = m_new


def _attn_init(m_ref, l_ref, acc_ref):
    m_ref[...] = jnp.full(m_ref.shape, NEG_BIG, F32)
    l_ref[...] = jnp.zeros(l_ref.shape, F32)
    acc_ref[...] = jnp.zeros(acc_ref.shape, F32)


def _attn_finish(lp_ref, sg_ref, l_ref, acc_ref, lam_init):
    lp = lp_ref[...]
    lam = (jnp.exp(jnp.sum(lp[0:1] * lp[1:2], axis=-1, keepdims=True))
           - jnp.exp(jnp.sum(lp[2:3] * lp[3:4], axis=-1, keepdims=True)) + lam_init)
    o = acc_ref[0] / l_ref[0] - lam * (acc_ref[1] / l_ref[1])
    return (_rms(o, sg_ref[...]) * (1.0 - lam_init)).astype(BF16)


def _attn_prompt_kernel(lp_ref, sg_ref, q_ref, k_ref, v_ref, o_ref, m_ref, l_ref, acc_ref, *, tq, lam_init):
    qi = pl.program_id(2)
    _attn_init(m_ref, l_ref, acc_ref)
    q = q_ref[...]

    def body(kv, carry):
        start = pl.multiple_of(kv * tq, tq)
        _attn_block(q, k_ref[pl.ds(start, tq), :], v_ref[pl.ds(start, tq), :], m_ref, l_ref, acc_ref, None)
        return carry

    lax.fori_loop(0, qi, body, 0)
    start = pl.multiple_of(qi * tq, tq)
    shift = CHUNK.bit_length() - 1
    rowc = lax.shift_right_logical(lax.broadcasted_iota(jnp.int32, (tq, tq), 0), shift)
    colc = lax.shift_right_logical(lax.broadcasted_iota(jnp.int32, (tq, tq), 1), shift)
    _attn_block(q, k_ref[pl.ds(start, tq), :], v_ref[pl.ds(start, tq), :], m_ref, l_ref, acc_ref, colc <= rowc)
    o_ref[...] = _attn_finish(lp_ref, sg_ref, l_ref, acc_ref, lam_init)


def _attn_prompt(qb, kb, vb, ai, p, n_seq, t, tq, lam_init):
    d = qb.shape[-1]
    hw = d // DIFF_HEADS
    assert t % tq == 0 and tq % CHUNK == 0
    q3, k3, v3 = (a.reshape(n_seq, t, d) for a in (qb, kb, vb))
    lp, sg = p["attn_lambda"], p["attn_subln"]
    kern = functools.partial(_attn_prompt_kernel, tq=tq, lam_init=lam_init)
    out = pl.pallas_call(
        kern,
        grid=(n_seq, DIFF_HEADS, t // tq),
        in_specs=[
            pl.BlockSpec((None,) + lp.shape[1:], lambda b, h, qi: (ai, 0, 0)),
            pl.BlockSpec((None, 1, hw), lambda b, h, qi: (ai, 0, 0)),
            pl.BlockSpec((None, tq, hw), lambda b, h, qi: (b, qi, h)),
            pl.BlockSpec((None, t, hw), lambda b, h, qi: (b, 0, h)),
            pl.BlockSpec((None, t, hw), lambda b, h, qi: (b, 0, h)),
        ],
        out_specs=pl.BlockSpec((None, tq, hw), lambda b, h, qi: (b, qi, h)),
        out_shape=jax.ShapeDtypeStruct((n_seq, t, d), BF16),
        scratch_shapes=[
            pltpu.VMEM((2, tq, 1), F32),
            pltpu.VMEM((2, tq, 1), F32),
            pltpu.VMEM((2, tq, hw), F32),
        ],
        compiler_params=_cparams(3),
        name="attn_prompt",
    )(lp, sg, q3, k3, v3)
    return out.reshape(n_seq * t, d)


def _attn_cached_kernel(lp_ref, sg_ref, q_ref, kp_ref, vp_ref, kn_ref, vn_ref, o_ref,
                        m_ref, l_ref, acc_ref, *, lam_init):
    _attn_init(m_ref, l_ref, acc_ref)
    q = q_ref[...]
    _attn_block(q, kp_ref[...].astype(BF16), vp_ref[...].astype(BF16), m_ref, l_ref, acc_ref, None)
    _attn_block(q, kn_ref[...], vn_ref[...], m_ref, l_ref, acc_ref, None)
    o_ref[...] = _attn_finish(lp_ref, sg_ref, l_ref, acc_ref, lam_init)


def _attn_cached(qb, kb, vb, k_past, v_past, ai, p, n_seq, t, lam_init):
    d = qb.shape[-1]
    hw = d // DIFF_HEADS
    past = k_past.shape[2]
    assert past % CHUNK == 0 and t <= CHUNK
    q3, k3, v3 = (a.reshape(n_seq, t, d) for a in (qb, kb, vb))
    lp, sg = p["attn_lambda"], p["attn_subln"]
    n_attn = k_past.shape[0]
    kern = functools.partial(_attn_cached_kernel, lam_init=lam_init)
    new_spec = pl.BlockSpec((None, t, hw), lambda b, h: (b, 0, h))
    past_spec = pl.BlockSpec((None, None, past, hw), lambda b, h: (ai, b, 0, h))
    out = pl.pallas_call(
        kern,
        grid=(n_seq, DIFF_HEADS),
        in_specs=[
            pl.BlockSpec((None,) + lp.shape[1:], lambda b, h: (ai, 0, 0)),
            pl.BlockSpec((None, 1, hw), lambda b, h: (ai, 0, 0)),
            new_spec, past_spec, past_spec, new_spec, new_spec,
        ],
        out_specs=new_spec,
        out_shape=jax.ShapeDtypeStruct((n_seq, t, d), BF16),
        scratch_shapes=[
            pltpu.VMEM((2, t, 1), F32),
            pltpu.VMEM((2, t, 1), F32),
            pltpu.VMEM((2, t, hw), F32),
        ],
        compiler_params=_cparams(2),
        name="attn_cached",
    )(lp, sg, q3, k_past.reshape(n_attn, n_seq, past, d), v_past.reshape(n_attn, n_seq, past, d), k3, v3)
    return out.reshape(n_seq * t, d)


def _out_kernel(x_ref, mix_ref, mo_ref, w1_ref, w2_ref, o_ref):
    o_ref[...] = x_ref[...] + (_dot(mix_ref[...], w1_ref[...]) + _dot(mo_ref[...], w2_ref[...]))


def _out_proj(x2d, mix, mo, w_out, wi, bm, bn):
    m, d = x2d.shape
    k1, k2 = mix.shape[1], mo.shape[1]
    assert k1 % k2 == 0
    return pl.pallas_call(
        _out_kernel,
        grid=(m // bm, d // bn),
        in_specs=[
            pl.BlockSpec((bm, bn), lambda i, j: (i, j)),
            pl.BlockSpec((bm, k1), lambda i, j: (i, 0)),
            pl.BlockSpec((bm, k2), lambda i, j: (i, 0)),
            pl.BlockSpec((None, k1, bn), lambda i, j: (wi, 0, j)),
            pl.BlockSpec((None, k2, bn), lambda i, j: (wi, k1 // k2, j)),
        ],
        out_specs=pl.BlockSpec((bm, bn), lambda i, j: (i, j)),
        out_shape=jax.ShapeDtypeStruct((m, d), F32),
        compiler_params=_cparams(2),
        name="out_proj",
    )(x2d, mix, mo, w_out, w_out)


def _ffn_kernel(x_ref, gn_ref, wg_ref, wv_ref, cwg_ref, cwv_ref, cbg_ref, cbv_ref, stg_ref, stv_ref, wdn_ref,
                o_ref, nstg_ref, nstv_ref, hn_ref, cg_ref, cv_ref, *, n_seg, seg_len, tiles_per_seq):
    i = pl.program_id(0)
    j = pl.program_id(1)
    carried = tiles_per_seq > 1

    @pl.when(j == 0)
    def _():
        x = x_ref[...]
        hn_ref[...] = _rms(x, gn_ref[...]).astype(BF16)
        o_ref[...] = x

    if carried:
        @pl.when(i % tiles_per_seq == 0)
        def _():
            cg_ref[j] = stg_ref[0]
            cv_ref[j] = stv_ref[0]

    hn = hn_ref[...]

    def half(w_ref, cw_ref, cb_ref, st_ref, carry_ref, nst_ref):
        up = _dot(hn, w_ref[...])
        cw = cw_ref[...]
        cb = cb_ref[...]
        n_prev = cw.shape[0] - 1
        ys = []
        for s in range(n_seg):
            seg = up[s * seg_len:(s + 1) * seg_len]
            ys.append(_causal_conv(seg, carry_ref[j] if carried else st_ref[s], cw, cb))
            tail = seg[seg_len - n_prev:seg_len]
            nst_ref[s] = tail
            if carried:
                carry_ref[j] = tail
        return _cat_rows(ys)

    g = half(wg_ref, cwg_ref, cbg_ref, stg_ref, cg_ref, nstg_ref)
    u = half(wv_ref, cwv_ref, cbv_ref, stv_ref, cv_ref, nstv_ref)
    act = (_gelu(g) * u).astype(BF16)
    o_ref[...] += _dot(act, wdn_ref[...])


def _ffn(x2d, tl, li, p, state, bf):
    m, d = x2d.shape
    dff = p["ffn_w_down"].shape[1]
    nj = dff // bf
    ns, tps = tl.n_seg, tl.tiles_per_seq
    width = p["ffn_conv_w"].shape[1]
    kern = functools.partial(_ffn_kernel, n_seg=ns, seg_len=tl.seg_len, tiles_per_seq=tps)

    def halves(block, index_map):
        gate = pl.BlockSpec(block, lambda i, j: index_map(i, j, j))
        value = pl.BlockSpec(block, lambda i, j: index_map(i, j, nj + j))
        return [gate, value]

    nst_spec = pl.BlockSpec((None, ns, width - 1, bf), lambda i, j: (i, 0, 0, j))
    nst_shape = jax.ShapeDtypeStruct((tl.n_tiles, ns, width - 1, dff), F32)
    return pl.pallas_call(
        kern,
        grid=(tl.n_tiles, nj),
        in_specs=[
            pl.BlockSpec((tl.bm, d), lambda i, j: (i, 0)),
            pl.BlockSpec((None, 1, d), lambda i, j: (li, 0, 0)),
            *halves((None, d, bf), lambda i, j, c: (li, 0, c)),
            *halves((None, width, bf), lambda i, j, c: (li, 0, c)),
            *halves((None, 1, bf), lambda i, j, c: (li, 0, c)),
            *halves((None, ns, width - 1, bf), lambda i, j, c: (li, i // tps, 0, c)),
            pl.BlockSpec((None, bf, d), lambda i, j: (li, j, 0)),
        ],
        out_specs=[pl.BlockSpec((tl.bm, d), lambda i, j: (i, 0)), nst_spec, nst_spec],
        out_shape=[jax.ShapeDtypeStruct((m, d), F32), nst_shape, nst_shape],
        scratch_shapes=[
            pltpu.VMEM((tl.bm, d), BF16),
            pltpu.VMEM((nj, width - 1, bf), F32),
            pltpu.VMEM((nj, width - 1, bf), F32),
        ],
        compiler_params=_cparams(2),
        name="ffn",
    )(x2d, p["norm_ffn"], p["ffn_w_up"], p["ffn_w_up"], p["ffn_conv_w"], p["ffn_conv_w"],
      p["ffn_conv_b"], p["ffn_conv_b"], state, state, p["ffn_w_down"])


def _run_trunk(x, mem_k, mem_v, lru_h, lru_conv, attn_past, ffn_conv, p, cfg):
    n_seq, t, d = x.shape
    depth = p["norm_mix"].shape[0]
    x2d = x.reshape(n_seq * t, d)
    tl = _Tiling(n_seq, t, cfg["bm"])
    tl_ffn = _Tiling(n_seq, t, cfg["bm_ffn"])
    mem_tokens = mem_k.shape[2]
    mk = mem_k.reshape(depth, n_seq, mem_tokens, -1)
    mv = mem_v.reshape(depth, n_seq, mem_tokens, -1)
    h0 = lru_h[:, :, None, :]
    new_h, new_lconv, new_k, new_v, new_fconv = [], [], [], [], []
    for i in range(depth):
        j = i // 2
        if i % 2 == 0:
            mix, mo, h_last, buf = _lru_in(x2d, tl, i, j, p, h0, lru_conv, mk, mv)
            new_h.append(_last_tile_state(h_last, tl)[:, 0, :])
            new_lconv.append(_last_tile_state(buf, tl))
            w_out = p["lru_w_out"]
        else:
            qb, kf, kb, vf, vb, mo = _att_in(x2d, tl, i, j, p, mk, mv, cfg["bc"])
            new_k.append(kf)
            new_v.append(vf)
            lam_init = 0.8 - 0.6 * math.exp(-0.3 * i)
            if attn_past is None:
                mix = _attn_prompt(qb, kb, vb, j, p, n_seq, t, cfg["tq"], lam_init)
            else:
                mix = _attn_cached(qb, kb, vb, attn_past[0], attn_past[1], j, p, n_seq, t, lam_init)
            w_out = p["attn_w_out"]
        x2d = _out_proj(x2d, mix, mo, w_out, j, cfg["bm_out"], cfg["bn_out"])
        x2d, fg, fv = _ffn(x2d, tl_ffn, i, p, ffn_conv, cfg["bf"])
        new_fconv.append(jnp.concatenate([_last_tile_state(fg, tl_ffn), _last_tile_state(fv, tl_ffn)], axis=-1))
    return x2d.reshape(n_seq, t, d), new_h, new_lconv, new_k, new_v, new_fconv


def _forward(x_prompt, x_sample, mem_prompt, cache_attn_k, cache_attn_v, cache_mem_k, cache_mem_v,
             state_lru_h, state_lru_conv, state_ffn_conv, p, mem_norm, mem_w_kv, mem_k_norm, cfg_p, cfg_s):
    b, t, d = x_prompt.shape
    depth = p["norm_mix"].shape[0]
    n_lru, n_attn = (depth + 1) // 2, depth // 2

    p = dict(p)
    for name in ("lru_w_in", "lru_gate_a_w", "lru_gate_x_w", "lru_w_out", "attn_w_in", "attn_w_out",
                 "ffn_w_up", "ffn_w_down"):
        p[name] = p[name].astype(BF16)
    for name in ("norm_mix", "norm_ffn", "lru_conv_b", "lru_gate_a_b", "lru_gate_x_b", "lru_lambda",
                 "attn_q_norm", "attn_k_norm", "attn_subln", "mem_q_norm", "ffn_conv_b"):
        p[name] = p[name][:, None, :]

    mem_tokens = mem_prompt.shape[1]
    mk, mv = _mem_kv(mem_prompt.reshape(b * mem_tokens, d), mem_norm[:, None, :], mem_w_kv.astype(BF16),
                     mem_k_norm[:, None, :])
    hd = mem_k_norm.shape[-1]
    p_mem_k = mk.reshape(depth, b, mem_tokens, MEM_HEADS, hd)
    p_mem_v = mv.reshape(depth, b, mem_tokens, MEM_HEADS, hd)

    zeros_h = jnp.zeros((n_lru, b, d), F32)
    zeros_lconv = jnp.zeros((n_lru, b) + state_lru_conv.shape[2:], F32)
    zeros_fconv = jnp.zeros((depth, b) + state_ffn_conv.shape[2:], F32)
    yp, ph, plc, pk, pv, pfc = _run_trunk(x_prompt, p_mem_k, p_mem_v, zeros_h, zeros_lconv, None,
                                          zeros_fconv, p, cfg_p)
    ys, sh, slc, sk, sv, sfc = _run_trunk(x_sample, cache_mem_k, cache_mem_v, state_lru_h, state_lru_conv,
                                          (cache_attn_k, cache_attn_v), state_ffn_conv, p, cfg_s)

    dk = p["attn_q_norm"].shape[-1]
    db, dt = x_sample.shape[0], x_sample.shape[1]

    def kshape(a, n, tt):
        return jnp.stack(a).reshape(n_attn, n, tt, DIFF_HEADS, 2, dk)

    def vshape(a, n, tt):
        return jnp.stack(a).reshape(n_attn, n, tt, DIFF_HEADS, 2 * dk)

    return (yp, ys, jnp.stack(ph), jnp.stack(plc), kshape(pk, b, t), vshape(pv, b, t), p_mem_k, p_mem_v,
            jnp.stack(pfc), jnp.stack(sh), jnp.stack(slc), kshape(sk, db, dt), vshape(sv, db, dt),
            jnp.stack(sfc))


CFG_PROMPT = dict(bm=1024, bc=512, bm_ffn=512, bf=512, bm_out=1024, bn_out=1024, tq=512)
CFG_SAMPLE = dict(bm=512, bc=256, bm_ffn=512, bf=512, bm_out=1024, bn_out=1024, tq=512)


def kernel(x_prompt, x_sample, mem_prompt, cache_attn_k, cache_attn_v, cache_mem_k, cache_mem_v, state_lru_h, state_lru_conv, state_ffn_conv, norm_mix, norm_ffn, lru_w_in, lru_conv_w, lru_conv_b, lru_gate_a_w, lru_gate_a_b, lru_gate_x_w, lru_gate_x_b, lru_lambda, lru_w_out, attn_w_in, attn_q_norm, attn_k_norm, attn_lambda, attn_subln, attn_w_out, mem_norm, mem_w_kv, mem_q_norm, mem_k_norm, ffn_w_up, ffn_conv_w, ffn_conv_b, ffn_w_down):
    p = {
        "norm_mix": norm_mix, "norm_ffn": norm_ffn,
        "lru_w_in": lru_w_in, "lru_conv_w": lru_conv_w, "lru_conv_b": lru_conv_b,
        "lru_gate_a_w": lru_gate_a_w, "lru_gate_a_b": lru_gate_a_b,
        "lru_gate_x_w": lru_gate_x_w, "lru_gate_x_b": lru_gate_x_b,
        "lru_lambda": lru_lambda, "lru_w_out": lru_w_out,
        "attn_w_in": attn_w_in, "attn_q_norm": attn_q_norm, "attn_k_norm": attn_k_norm,
        "attn_lambda": attn_lambda, "attn_subln": attn_subln, "attn_w_out": attn_w_out,
        "mem_q_norm": mem_q_norm,
        "ffn_w_up": ffn_w_up, "ffn_conv_w": ffn_conv_w, "ffn_conv_b": ffn_conv_b, "ffn_w_down": ffn_w_down,
    }
    return _forward(x_prompt, x_sample, mem_prompt, cache_attn_k, cache_attn_v, cache_mem_k, cache_mem_v,
                    state_lru_h, state_lru_conv, state_ffn_conv, p, mem_norm, mem_w_kv, mem_k_norm,
                    CFG_PROMPT, CFG_SAMPLE)
```

```python
import functools
import math

import jax
import jax.numpy as jnp
from jax import lax
from jax.experimental import pallas as pl
from jax.experimental.pallas import tpu as pltpu

F32 = jnp.float32
BF16 = jnp.bfloat16

CHUNK = 64
LRU_HEADS = 8
LRU_C = 8.0
DIFF_HEADS = 8
MEM_HEADS = 4
RMS_EPS = 1e-6
NEG_BIG = -1e30
LOG2E = 1.4426950408889634
SUBLANES = 8

V7X_VMEM_LIMIT_BYTES = 56 * 1024 * 1024


def _cparams(n_grid_axes):
    return pltpu.CompilerParams(
        dimension_semantics=("arbitrary",) * n_grid_axes,
        vmem_limit_bytes=V7X_VMEM_LIMIT_BYTES,
    )


def _dot(a, b):
    return jnp.dot(a, b, preferred_element_type=F32)


def _dot_nt(a, b):
    return lax.dot_general(a, b, (((1,), (1,)), ((), ())), preferred_element_type=F32)


def _rms(x, g):
    return x * lax.rsqrt(jnp.mean(x * x, axis=-1, keepdims=True) + RMS_EPS) * g


def _gelu(x):
    return x * (0.5 * (1.0 + jnp.tanh(0.7978845608028654 * (x + 0.044715 * (x * x * x)))))


def _cat_rows(parts):
    return parts[0] if len(parts) == 1 else jnp.concatenate(parts, axis=0)


def _delayed(seg, prev, d):
    n_prev = prev.shape[0]
    v = pltpu.roll(seg, d, 0)
    row = lax.broadcasted_iota(jnp.int32, (SUBLANES, 1), 0)
    top = v[0:SUBLANES]
    for r in range(d):
        top = jnp.where(row == r, prev[n_prev - d + r:n_prev - d + r + 1], top)
    return jnp.concatenate([top, v[SUBLANES:]], axis=0)


def _causal_conv(seg, prev, cw, cb):
    n_prev = cw.shape[0] - 1
    y = cb + _delayed(seg, prev, n_prev) * cw[0:1]
    for t in range(1, n_prev):
        y = y + _delayed(seg, prev, n_prev - t) * cw[t:t + 1]
    return y + seg * cw[n_prev:n_prev + 1]


def _scan_rows(a, b):
    n = a.shape[0]
    row = lax.broadcasted_iota(jnp.int32, (n, 1), 0)
    sh = 1
    while sh < min(SUBLANES, n):
        valid = row >= sh
        a_sh = pltpu.roll(a, sh, 0)
        b_sh = pltpu.roll(b, sh, 0)
        b = jnp.where(valid, b + a * b_sh, b)
        a = jnp.where(valid, a * a_sh, a)
        sh *= 2
    while sh < n:
        b = jnp.concatenate([b[:sh], b[sh:] + a[sh:] * b[:n - sh]], axis=0)
        a = jnp.concatenate([a[:sh], a[sh:] * a[:n - sh]], axis=0)
        sh *= 2
    return a, b


def _mem_attn(q, gq, mk_ref, mv_ref, n_seg, seg_len):
    hd = gq.shape[-1]
    outs = []
    for h in range(q.shape[-1] // hd):
        cols = slice(h * hd, (h + 1) * hd)
        qn = _rms(q[:, cols], gq).astype(BF16)
        segs = []
        for s in range(n_seg):
            k = mk_ref[s, :, cols].astype(BF16)
            v = mv_ref[s, :, cols].astype(BF16)
            sc = _dot_nt(qn[s * seg_len:(s + 1) * seg_len], k) * (hd ** -0.5)
            p = jnp.exp(sc - jnp.max(sc, axis=-1, keepdims=True))
            pr = p / jnp.sum(p, axis=-1, keepdims=True)
            segs.append(_dot(pr.astype(BF16), v))
        outs.append(_cat_rows(segs))
    return outs[0] if len(outs) == 1 else jnp.concatenate(outs, axis=-1)


class _Tiling:
    def __init__(self, n_seq, t, bm):
        self.n_seq, self.t = n_seq, t
        if t >= bm:
            assert t % bm == 0
            self.n_seg, self.seg_len, self.tiles_per_seq = 1, bm, t // bm
        else:
            assert bm % t == 0 and n_seq % (bm // t) == 0
            self.n_seg, self.seg_len, self.tiles_per_seq = bm // t, t, 1
        self.bm = self.n_seg * self.seg_len
        self.n_tiles = n_seq * t // self.bm
        assert self.seg_len & (self.seg_len - 1) == 0 and self.seg_len >= SUBLANES


def _last_tile_state(per_tile, tl):
    rows, c = per_tile.shape[2:]
    return per_tile.reshape(tl.n_seq, tl.tiles_per_seq, rows, c)[:, -1]


def _resident(block_shape, index_map):
    return pl.BlockSpec(block_shape, index_map, pipeline_mode=pl.Buffered(1))


def _mem_kv_kernel(mem_ref, gn_ref, wk_ref, wv_ref, gk_ref, k_ref, v_ref, hn_ref):
    @pl.when(pl.program_id(1) == 0)
    def _():
        hn_ref[...] = _rms(mem_ref[...], gn_ref[...]).astype(BF16)

    hn = hn_ref[...]
    k_ref[...] = _rms(_dot(hn, wk_ref[...]), gk_ref[...])
    v_ref[...] = _dot(hn, wv_ref[...])


def _mem_kv(mem2d, mem_norm, w_kv, mem_k_norm):
    depth, d, two_w = w_kv.shape
    mem_w = two_w // 2
    hd = mem_w // MEM_HEADS
    m = mem2d.shape[0]
    return pl.pallas_call(
        _mem_kv_kernel,
        grid=(depth, MEM_HEADS),
        in_specs=[
            pl.BlockSpec((m, d), lambda l, j: (0, 0)),
            pl.BlockSpec((None, 1, d), lambda l, j: (l, 0, 0)),
            pl.BlockSpec((None, d, hd), lambda l, j: (l, 0, j)),
            pl.BlockSpec((None, d, hd), lambda l, j: (l, 0, MEM_HEADS + j)),
            pl.BlockSpec((None, 1, hd), lambda l, j: (l, 0, 0)),
        ],
        out_specs=[
            pl.BlockSpec((None, m, hd), lambda l, j: (l, 0, j)),
            pl.BlockSpec((None, m, hd), lambda l, j: (l, 0, j)),
        ],
        out_shape=[jax.ShapeDtypeStruct((depth, m, mem_w), F32)] * 2,
        scratch_shapes=[pltpu.VMEM((m, d), BF16)],
        compiler_params=_cparams(2),
        name="mem_kv",
    )(mem2d, mem_norm, w_kv, w_kv, mem_k_norm)


def _lru_in_kernel(x_ref, gn_ref, wg_ref, wx_ref, cw_ref, cb_ref, wa_ref, wi_ref, ba_ref, bi_ref,
                   lam_ref, h0_ref, c0_ref, gq_ref, mk_ref, mv_ref,
                   mix_ref, mo_ref, hn_out_ref, cn_out_ref,
                   hn_ref, hc_ref, cc_ref, *, n_seg, seg_len, tiles_per_seq):
    i = pl.program_id(0)
    j = pl.program_id(1)
    carried = tiles_per_seq > 1

    @pl.when(j == 0)
    def _():
        hn_ref[...] = _rms(x_ref[...], gn_ref[...]).astype(BF16)

    @pl.when(j < LRU_HEADS)
    def _():
        hn = hn_ref[...]
        gate = _dot(hn, wg_ref[...])
        xr = _dot(hn, wx_ref[...])
        if carried:
            @pl.when(i % tiles_per_seq == 0)
            def _():
                cc_ref[j] = c0_ref[0]
                hc_ref[j] = h0_ref[0]

        cw = cw_ref[...]
        cb = cb_ref[...]
        n_prev = cw.shape[0] - 1
        xcs = []
        for s in range(n_seg):
            seg = xr[s * seg_len:(s + 1) * seg_len]
            xcs.append(_causal_conv(seg, cc_ref[j] if carried else c0_ref[s], cw, cb))
            tail = seg[seg_len - n_prev:seg_len]
            cn_out_ref[s] = tail
            if carried:
                cc_ref[j] = tail
        xc = _cat_rows(xcs)

        xcb = xc.astype(BF16)
        r = jax.nn.sigmoid(_dot(xcb, wa_ref[...]) + ba_ref[...])
        ig = jax.nn.sigmoid(_dot(xcb, wi_ref[...]) + bi_ref[...])
        log_a = (LRU_C * r) * jax.nn.log_sigmoid(lam_ref[...])
        a = jnp.exp(log_a)
        one_minus_a2 = -jnp.tanh(log_a) * (a * a + 1.0)
        gated = jnp.sqrt(one_minus_a2) * (ig * xc)
        hss = []
        for s in range(n_seg):
            sl = slice(s * seg_len, (s + 1) * seg_len)
            cum_a, cum_b = _scan_rows(a[sl], gated[sl])
            hs = cum_b + cum_a * (hc_ref[j] if carried else h0_ref[s])
            last = hs[seg_len - 1:seg_len]
            hn_out_ref[s] = last
            if carried:
                hc_ref[j] = last
            hss.append(hs)
        mix_ref[...] = (_gelu(gate) * _cat_rows(hss)).astype(BF16)

    @pl.when(j >= LRU_HEADS)
    def _():
        q = _dot(hn_ref[...], wx_ref[...])
        mo_ref[...] = _mem_attn(q, gq_ref[...], mk_ref, mv_ref, n_seg, seg_len).astype(BF16)


def _lru_in(x2d, tl, li, mi, p, h0, c0, mk, mv):
    m, d = x2d.shape
    hw = d // LRU_HEADS
    n_steps = LRU_HEADS + MEM_HEADS
    ns, tps = tl.n_seg, tl.tiles_per_seq
    width = p["lru_conv_w"].shape[1]
    last = LRU_HEADS - 1

    def hcol(j):
        return jnp.minimum(j, last)

    def mcol(j):
        return jnp.maximum(j - LRU_HEADS, 0)

    vec = pl.BlockSpec((None, 1, hw), lambda i, j: (mi, 0, hcol(j)))
    gate_w = pl.BlockSpec((None, None, hw, hw), lambda i, j: (mi, hcol(j), 0, 0))
    mem = pl.BlockSpec((None, ns, mk.shape[2], hw), lambda i, j: (li, i // tps, 0, mcol(j)))
    kern = functools.partial(_lru_in_kernel, n_seg=ns, seg_len=tl.seg_len, tiles_per_seq=tps)
    return pl.pallas_call(
        kern,
        grid=(tl.n_tiles, n_steps),
        in_specs=[
            _resident((tl.bm, d), lambda i, j: (i, 0)),
            pl.BlockSpec((None, 1, d), lambda i, j: (li, 0, 0)),
            pl.BlockSpec((None, d, hw), lambda i, j: (mi, 0, hcol(j))),
            pl.BlockSpec((None, d, hw), lambda i, j: (mi, 0, LRU_HEADS + j)),
            pl.BlockSpec((None, width, hw), lambda i, j: (mi, 0, hcol(j))),
            vec, gate_w, gate_w, vec, vec, vec,
            pl.BlockSpec((None, ns, 1, hw), lambda i, j: (mi, i // tps, 0, hcol(j))),
            pl.BlockSpec((None, ns, width - 1, hw), lambda i, j: (mi, i // tps, 0, hcol(j))),
            pl.BlockSpec((None, 1, hw), lambda i, j: (li, 0, 0)),
            mem, mem,
        ],
        out_specs=[
            pl.BlockSpec((tl.bm, hw), lambda i, j: (i, hcol(j))),
            pl.BlockSpec((tl.bm, hw), lambda i, j: (i, mcol(j))),
            pl.BlockSpec((None, ns, 1, hw), lambda i, j: (i, 0, 0, hcol(j))),
            pl.BlockSpec((None, ns, width - 1, hw), lambda i, j: (i, 0, 0, hcol(j))),
        ],
        out_shape=[
            jax.ShapeDtypeStruct((m, d), BF16),
            jax.ShapeDtypeStruct((m, MEM_HEADS * hw), BF16),
            jax.ShapeDtypeStruct((tl.n_tiles, ns, 1, d), F32),
            jax.ShapeDtypeStruct((tl.n_tiles, ns, width - 1, d), F32),
        ],
        scratch_shapes=[
            pltpu.VMEM((tl.bm, d), BF16),
            pltpu.VMEM((LRU_HEADS, 1, hw), F32),
            pltpu.VMEM((LRU_HEADS, width - 1, hw), F32),
        ],
        compiler_params=_cparams(2),
        name="lru_in",
    )(x2d, p["norm_mix"], p["lru_w_in"], p["lru_w_in"], p["lru_conv_w"], p["lru_conv_b"],
      p["lru_gate_a_w"], p["lru_gate_x_w"], p["lru_gate_a_b"], p["lru_gate_x_b"], p["lru_lambda"],
      h0, c0, p["mem_q_norm"], mk, mv)


def _att_in_kernel(x_ref, gn_ref, w_ref, qg_ref, kg_ref, gq_ref, mk_ref, mv_ref,
                   qb_ref, kf_ref, kb_ref, vf_ref, vb_ref, mo_ref, hn_ref, *, n_seg, seg_len, nq):
    j = pl.program_id(1)

    @pl.when(j == 0)
    def _():
        hn_ref[...] = _rms(x_ref[...], gn_ref[...]).astype(BF16)

    z = _dot(hn_ref[...], w_ref[...])
    dk = qg_ref.shape[-1]

    @pl.when(j < nq)
    def _():
        qscale = dk ** -0.5 * LOG2E
        for c in range(z.shape[-1] // dk):
            sl = slice(c * dk, (c + 1) * dk)
            qb_ref[:, sl] = (_rms(z[:, sl], qg_ref[...]) * qscale).astype(BF16)

    @pl.when((j >= nq) & (j < 2 * nq))
    def _():
        for c in range(z.shape[-1] // dk):
            sl = slice(c * dk, (c + 1) * dk)
            kn = _rms(z[:, sl], kg_ref[...])
            kf_ref[:, sl] = kn
            kb_ref[:, sl] = kn.astype(BF16)

    @pl.when((j >= 2 * nq) & (j < 3 * nq))
    def _():
        vf_ref[...] = z
        vb_ref[...] = z.astype(BF16)

    @pl.when(j >= 3 * nq)
    def _():
        mo_ref[...] = _mem_attn(z, gq_ref[...], mk_ref, mv_ref, n_seg, seg_len).astype(BF16)


def _att_in(x2d, tl, li, ai, p, mk, mv, bc):
    m, d = x2d.shape
    mem_w = mk.shape[-1]
    nq, nm = d // bc, mem_w // bc
    n_steps = 3 * nq + nm
    ns, tps = tl.n_seg, tl.tiles_per_seq
    dk = p["attn_q_norm"].shape[-1]
    hd = p["mem_q_norm"].shape[-1]

    def col(lo, n):
        return lambda i, j: (i, jnp.clip(j - lo, 0, n - 1))

    mem = pl.BlockSpec((None, ns, mk.shape[2], bc),
                       lambda i, j: (li, i // tps, 0, jnp.clip(j - 3 * nq, 0, nm - 1)))
    kern = functools.partial(_att_in_kernel, n_seg=ns, seg_len=tl.seg_len, nq=nq)
    return pl.pallas_call(
        kern,
        grid=(tl.n_tiles, n_steps),
        in_specs=[
            _resident((tl.bm, d), lambda i, j: (i, 0)),
            pl.BlockSpec((None, 1, d), lambda i, j: (li, 0, 0)),
            pl.BlockSpec((None, d, bc), lambda i, j: (ai, 0, j)),
            pl.BlockSpec((None, 1, dk), lambda i, j: (ai, 0, 0)),
            pl.BlockSpec((None, 1, dk), lambda i, j: (ai, 0, 0)),
            pl.BlockSpec((None, 1, hd), lambda i, j: (li, 0, 0)),
            mem, mem,
        ],
        out_specs=[
            pl.BlockSpec((tl.bm, bc), col(0, nq)),
            pl.BlockSpec((tl.bm, bc), col(nq, nq)),
            pl.BlockSpec((tl.bm, bc), col(nq, nq)),
            pl.BlockSpec((tl.bm, bc), col(2 * nq, nq)),
            pl.BlockSpec((tl.bm, bc), col(2 * nq, nq)),
            pl.BlockSpec((tl.bm, bc), col(3 * nq, nm)),
        ],
        out_shape=[
            jax.ShapeDtypeStruct((m, d), BF16),
            jax.ShapeDtypeStruct((m, d), F32),
            jax.ShapeDtypeStruct((m, d), BF16),
            jax.ShapeDtypeStruct((m, d), F32),
            jax.ShapeDtypeStruct((m, d), BF16),
            jax.ShapeDtypeStruct((m, mem_w), BF16),
        ],
        scratch_shapes=[pltpu.VMEM((tl.bm, d), BF16)],
        compiler_params=_cparams(2),
        name="att_in",
    )(x2d, p["norm_mix"], p["attn_w_in"], p["attn_q_norm"], p["attn_k_norm"], p["mem_q_norm"], mk, mv)


def _attn_block(q, k, v, m_ref, l_ref, acc_ref, mask):
    dk = q.shape[-1] // 2
    tk = k.shape[0]
    for c in range(2):
        sl = slice(c * dk, (c + 1) * dk)
        s = _dot_nt(k[:, sl], q[:, sl])
        if mask is not None:
            s = jnp.where(mask, s, NEG_BIG)
        m_old = m_ref[c]
        m_new = jnp.maximum(m_old, jnp.max(s, axis=0, keepdims=True))
        alpha = jnp.exp2(m_old - m_new)
        p = jnp.exp2(s - m_new)
        l_ref[c] = alpha * l_ref[c] + jnp.sum(p.reshape(tk // SUBLANES, SUBLANES, p.shape[-1]), axis=0)
        pv = lax.dot_general(v, p.astype(BF16), (((0,), (0,)), ((), ())), preferred_element_type=F32)
        acc_ref[c] = alpha * acc_ref[c] + pv
        m_ref[c] = m_new


def _attn_init(m_ref, l_ref, acc_ref):
    m_ref[...] = jnp.full(m_ref.shape, NEG_BIG, F32)
    l_ref[...] = jnp.zeros(l_ref.shape, F32)
    acc_ref[...] = jnp.zeros(acc_ref.shape, F32)


def _attn_finish(lp_ref, sg_ref, l_ref, acc_ref, lam_init):
    lp = lp_ref[...]
    lam = (jnp.exp(jnp.sum(lp[0:1] * lp[1:2], axis=-1, keepdims=True))
           - jnp.exp(jnp.sum(lp[2:3] * lp[3:4], axis=-1, keepdims=True)) + lam_init)
    l0 = jnp.sum(l_ref[0], axis=0, keepdims=True)
    l1 = jnp.sum(l_ref[1], axis=0, keepdims=True)
    o = (acc_ref[0] / l0 - lam * (acc_ref[1] / l1)).T
    return (_rms(o, sg_ref[...]) * (1.0 - lam_init)).astype(BF16)


def _attn_prompt_kernel(lp_ref, sg_ref, q_ref, k_ref, v_ref, o_ref, m_ref, l_ref, acc_ref, *, tq, tk, lam_init):
    qi = pl.program_id(2)
    _attn_init(m_ref, l_ref, acc_ref)
    q = q_ref[...]
    per = tk // tq

    def visible(start, size):
        start = pl.multiple_of(start, size)
        _attn_block(q, k_ref[pl.ds(start, size), :], v_ref[pl.ds(start, size), :], m_ref, l_ref, acc_ref, None)

    def body(kv, carry):
        visible(kv * tk, tk)
        return carry

    n_big = qi // per
    lax.fori_loop(0, n_big, body, 0)
    for r in range(1, per):
        @pl.when(qi % per >= r)
        def _():
            visible(n_big * tk + (r - 1) * tq, tq)

    start = pl.multiple_of(qi * tq, tq)
    shift = CHUNK.bit_length() - 1
    keyc = lax.shift_right_logical(lax.broadcasted_iota(jnp.int32, (tq, tq), 0), shift)
    qryc = lax.shift_right_logical(lax.broadcasted_iota(jnp.int32, (tq, tq), 1), shift)
    _attn_block(q, k_ref[pl.ds(start, tq), :], v_ref[pl.ds(start, tq), :], m_ref, l_ref, acc_ref, keyc <= qryc)
    o_ref[...] = _attn_finish(lp_ref, sg_ref, l_ref, acc_ref, lam_init)


def _attn_prompt(qb, kb, vb, ai, p, n_seq, t, tq, tk, lam_init):
    d = qb.shape[-1]
    hw = d // DIFF_HEADS
    assert t % tq == 0 and tq % CHUNK == 0 and tk % tq == 0
    q3, k3, v3 = (a.reshape(n_seq, t, d) for a in (qb, kb, vb))
    lp, sg = p["attn_lambda"], p["attn_subln"]
    kern = functools.partial(_attn_prompt_kernel, tq=tq, tk=tk, lam_init=lam_init)
    out = pl.pallas_call(
        kern,
        grid=(n_seq, DIFF_HEADS, t // tq),
        in_specs=[
            pl.BlockSpec((None,) + lp.shape[1:], lambda b, h, qi: (ai, 0, 0)),
            pl.BlockSpec((None, 1, hw), lambda b, h, qi: (ai, 0, 0)),
            pl.BlockSpec((None, tq, hw), lambda b, h, qi: (b, qi, h)),
            pl.BlockSpec((None, t, hw), lambda b, h, qi: (b, 0, h)),
            pl.BlockSpec((None, t, hw), lambda b, h, qi: (b, 0, h)),
        ],
        out_specs=pl.BlockSpec((None, tq, hw), lambda b, h, qi: (b, qi, h)),
        out_shape=jax.ShapeDtypeStruct((n_seq, t, d), BF16),
        scratch_shapes=[
            pltpu.VMEM((2, 1, tq), F32),
            pltpu.VMEM((2, SUBLANES, tq), F32),
            pltpu.VMEM((2, hw, tq), F32),
        ],
        compiler_params=_cparams(3),
        name="attn_prompt",
    )(lp, sg, q3, k3, v3)
    return out.reshape(n_seq * t, d)


def _attn_cached_kernel(lp_ref, sg_ref, q_ref, kp_ref, vp_ref, kn_ref, vn_ref, o_ref,
                        m_ref, l_ref, acc_ref, *, past, lam_init):
    n_heads = DIFF_HEADS
    rows_per_t = 2 * n_heads
    hw = q_ref.shape[-1] // n_heads
    for h in range(n_heads):
        cols = slice(h * hw, (h + 1) * hw)
        k = jnp.concatenate([kp_ref[pl.ds(2 * h + c, past, stride=rows_per_t), :] for c in range(2)], axis=-1)
        v = jnp.concatenate([vp_ref[pl.ds(c * n_heads + h, past, stride=rows_per_t), :] for c in range(2)],
                            axis=-1)
        q = q_ref[:, cols]
        _attn_init(m_ref, l_ref, acc_ref)
        _attn_block(q, k.astype(BF16), v.astype(BF16), m_ref, l_ref, acc_ref, None)
        _attn_block(q, kn_ref[:, cols], vn_ref[:, cols], m_ref, l_ref, acc_ref, None)
        o_ref[:, cols] = _attn_finish(lp_ref, sg_ref, l_ref, acc_ref, lam_init)


def _attn_cached(qb, kb, vb, k_past, v_past, ai, p, n_seq, t, lam_init):
    d = qb.shape[-1]
    hw = d // DIFF_HEADS
    n_attn, _, past, n_heads, _, dk = k_past.shape
    assert past % CHUNK == 0 and t <= CHUNK and n_heads == DIFF_HEADS and 2 * dk == hw
    q3, k3, v3 = (a.reshape(n_seq, t, d) for a in (qb, kb, vb))
    lp, sg = p["attn_lambda"], p["attn_subln"]
    k_rows = k_past.reshape(n_attn * n_seq, past * 2 * n_heads, dk)
    v_rows = v_past.reshape(n_attn, n_seq, past, n_heads, 2, dk).transpose(0, 1, 2, 4, 3, 5)
    v_rows = v_rows.reshape(n_attn * n_seq, past * 2 * n_heads, dk)
    kern = functools.partial(_attn_cached_kernel, past=past, lam_init=lam_init)
    new_spec = pl.BlockSpec((None, t, d), lambda b: (b, 0, 0))
    past_spec = pl.BlockSpec((None, past * 2 * n_heads, dk), lambda b: (ai * n_seq + b, 0, 0))
    out = pl.pallas_call(
        kern,
        grid=(n_seq,),
        in_specs=[
            pl.BlockSpec((None,) + lp.shape[1:], lambda b: (ai, 0, 0)),
            pl.BlockSpec((None, 1, hw), lambda b: (ai, 0, 0)),
            new_spec, past_spec, past_spec, new_spec, new_spec,
        ],
        out_specs=new_spec,
        out_shape=jax.ShapeDtypeStruct((n_seq, t, d), BF16),
        scratch_shapes=[
            pltpu.VMEM((2, 1, t), F32),
            pltpu.VMEM((2, SUBLANES, t), F32),
            pltpu.VMEM((2, hw, t), F32),
        ],
        compiler_params=_cparams(1),
        name="attn_cached",
    )(lp, sg, q3, k_rows, v_rows, k3, v3)
    return out.reshape(n_seq * t, d)


def _out_kernel(x_ref, mix_ref, mo_ref, w1_ref, w2_ref, o_ref):
    o_ref[...] = x_ref[...] + (_dot(mix_ref[...], w1_ref[...]) + _dot(mo_ref[...], w2_ref[...]))


def _out_proj(x2d, mix, mo, w_out, wi, bm, bn):
    m, d = x2d.shape
    k1, k2 = mix.shape[1], mo.shape[1]
    assert k1 % k2 == 0
    return pl.pallas_call(
        _out_kernel,
        grid=(m // bm, d // bn),
        in_specs=[
            pl.BlockSpec((bm, bn), lambda i, j: (i, j)),
            pl.BlockSpec((bm, k1), lambda i, j: (i, 0)),
            pl.BlockSpec((bm, k2), lambda i, j: (i, 0)),
            pl.BlockSpec((None, k1, bn), lambda i, j: (wi, 0, j)),
            pl.BlockSpec((None, k2, bn), lambda i, j: (wi, k1 // k2, j)),
        ],
        out_specs=pl.BlockSpec((bm, bn), lambda i, j: (i, j)),
        out_shape=jax.ShapeDtypeStruct((m, d), F32),
        compiler_params=_cparams(2),
        name="out_proj",
    )(x2d, mix, mo, w_out, w_out)


def _ffn_kernel(x_ref, gn_ref, wg_ref, wv_ref, cwg_ref, cwv_ref, cbg_ref, cbv_ref, stg_ref, stv_ref, wdn_ref,
                o_ref, nstg_ref, nstv_ref, hn_ref, cg_ref, cv_ref, *, n_seg, seg_len, tiles_per_seq):
    i = pl.program_id(0)
    j = pl.program_id(1)
    carried = tiles_per_seq > 1

    @pl.when(j == 0)
    def _():
        x = x_ref[...]
        hn_ref[...] = _rms(x, gn_ref[...]).astype(BF16)
        o_ref[...] = x

    if carried:
        @pl.when(i % tiles_per_seq == 0)
        def _():
            cg_ref[j] = stg_ref[0]
            cv_ref[j] = stv_ref[0]

    hn = hn_ref[...]

    def half(w_ref, cw_ref, cb_ref, st_ref, carry_ref, nst_ref):
        up = _dot(hn, w_ref[...])
        cw = cw_ref[...]
        cb = cb_ref[...]
        n_prev = cw.shape[0] - 1
        ys = []
        for s in range(n_seg):
            seg = up[s * seg_len:(s + 1) * seg_len]
            ys.append(_causal_conv(seg, carry_ref[j] if carried else st_ref[s], cw, cb))
            tail = seg[seg_len - n_prev:seg_len]
            nst_ref[s] = tail
            if carried:
                carry_ref[j] = tail
        return _cat_rows(ys)

    g = half(wg_ref, cwg_ref, cbg_ref, stg_ref, cg_ref, nstg_ref)
    u = half(wv_ref, cwv_ref, cbv_ref, stv_ref, cv_ref, nstv_ref)
    act = (_gelu(g) * u).astype(BF16)
    o_ref[...] += _dot(act, wdn_ref[...])


def _ffn(x2d, tl, li, p, state, bf):
    m, d = x2d.shape
    dff = p["ffn_w_down"].shape[1]
    nj = dff // bf
    ns, tps = tl.n_seg, tl.tiles_per_seq
    width = p["ffn_conv_w"].shape[1]
    kern = functools.partial(_ffn_kernel, n_seg=ns, seg_len=tl.seg_len, tiles_per_seq=tps)

    def halves(block, index_map):
        gate = pl.BlockSpec(block, lambda i, j: index_map(i, j, j))
        value = pl.BlockSpec(block, lambda i, j: index_map(i, j, nj + j))
        return [gate, value]

    nst_spec = pl.BlockSpec((None, ns, width - 1, bf), lambda i, j: (i, 0, 0, j))
    nst_shape = jax.ShapeDtypeStruct((tl.n_tiles, ns, width - 1, dff), F32)
    return pl.pallas_call(
        kern,
        grid=(tl.n_tiles, nj),
        in_specs=[
            pl.BlockSpec((tl.bm, d), lambda i, j: (i, 0)),
            pl.BlockSpec((None, 1, d), lambda i, j: (li, 0, 0)),
            *halves((None, d, bf), lambda i, j, c: (li, 0, c)),
            *halves((None, width, bf), lambda i, j, c: (li, 0, c)),
            *halves((None, 1, bf), lambda i, j, c: (li, 0, c)),
            *halves((None, ns, width - 1, bf), lambda i, j, c: (li, i // tps, 0, c)),
            pl.BlockSpec((None, bf, d), lambda i, j: (li, j, 0)),
        ],
        out_specs=[pl.BlockSpec((tl.bm, d), lambda i, j: (i, 0)), nst_spec, nst_spec],
        out_shape=[jax.ShapeDtypeStruct((m, d), F32), nst_shape, nst_shape],
        scratch_shapes=[
            pltpu.VMEM((tl.bm, d), BF16),
            pltpu.VMEM((nj, width - 1, bf), F32),
            pltpu.VMEM((nj, width - 1, bf), F32),
        ],
        compiler_params=_cparams(2),
        name="ffn",
    )(x2d, p["norm_ffn"], p["ffn_w_up"], p["ffn_w_up"], p["ffn_conv_w"], p["ffn_conv_w"],
      p["ffn_conv_b"], p["ffn_conv_b"], state, state, p["ffn_w_down"])


def _run_trunk(x, mem_k, mem_v, lru_h, lru_conv, attn_past, ffn_conv, p, cfg):
    n_seq, t, d = x.shape
    depth = p["norm_mix"].shape[0]
    x2d = x.reshape(n_seq * t, d)
    tl = _Tiling(n_seq, t, cfg["bm"])
    tl_ffn = _Tiling(n_seq, t, cfg["bm_ffn"])
    mem_tokens = mem_k.shape[2]
    mk = mem_k.reshape(depth, n_seq, mem_tokens, -1)
    mv = mem_v.reshape(depth, n_seq, mem_tokens, -1)
    h0 = lru_h[:, :, None, :]
    new_h, new_lconv, new_k, new_v, new_fconv = [], [], [], [], []
    for i in range(depth):
        j = i // 2
        if i % 2 == 0:
            mix, mo, h_last, buf = _lru_in(x2d, tl, i, j, p, h0, lru_conv, mk, mv)
            new_h.append(_last_tile_state(h_last, tl)[:, 0, :])
            new_lconv.append(_last_tile_state(buf, tl))
            w_out = p["lru_w_out"]
        else:
            qb, kf, kb, vf, vb, mo = _att_in(x2d, tl, i, j, p, mk, mv, cfg["bc"])
            new_k.append(kf)
            new_v.append(vf)
            lam_init = 0.8 - 0.6 * math.exp(-0.3 * i)
            if attn_past is None:
                mix = _attn_prompt(qb, kb, vb, j, p, n_seq, t, cfg["tq"], cfg["tk"], lam_init)
            else:
                mix = _attn_cached(qb, kb, vb, attn_past[0], attn_past[1], j, p, n_seq, t, lam_init)
            w_out = p["attn_w_out"]
        x2d = _out_proj(x2d, mix, mo, w_out, j, cfg["bm_out"], cfg["bn_out"])
        x2d, fg, fv = _ffn(x2d, tl_ffn, i, p, ffn_conv, cfg["bf"])
        new_fconv.append(jnp.concatenate([_last_tile_state(fg, tl_ffn), _last_tile_state(fv, tl_ffn)], axis=-1))
    return x2d.reshape(n_seq, t, d), new_h, new_lconv, new_k, new_v, new_fconv


def _forward(x_prompt, x_sample, mem_prompt, cache_attn_k, cache_attn_v, cache_mem_k, cache_mem_v,
             state_lru_h, state_lru_conv, state_ffn_conv, p, mem_norm, mem_w_kv, mem_k_norm, cfg_p, cfg_s):
    b, t, d = x_prompt.shape
    depth = p["norm_mix"].shape[0]
    n_lru, n_attn = (depth + 1) // 2, depth // 2

    p = dict(p)
    for name in ("lru_w_in", "lru_gate_a_w", "lru_gate_x_w", "lru_w_out", "attn_w_in", "attn_w_out",
                 "ffn_w_up", "ffn_w_down"):
        p[name] = p[name].astype(BF16)
    for name in ("norm_mix", "norm_ffn", "lru_conv_b", "lru_gate_a_b", "lru_gate_x_b", "lru_lambda",
                 "attn_q_norm", "attn_k_norm", "attn_subln", "mem_q_norm", "ffn_conv_b"):
        p[name] = p[name][:, None, :]

    mem_tokens = mem_prompt.shape[1]
    mk, mv = _mem_kv(mem_prompt.reshape(b * mem_tokens, d), mem_norm[:, None, :], mem_w_kv.astype(BF16),
                     mem_k_norm[:, None, :])
    hd = mem_k_norm.shape[-1]
    p_mem_k = mk.reshape(depth, b, mem_tokens, MEM_HEADS, hd)
    p_mem_v = mv.reshape(depth, b, mem_tokens, MEM_HEADS, hd)

    zeros_h = jnp.zeros((n_lru, b, d), F32)
    zeros_lconv = jnp.zeros((n_lru, b) + state_lru_conv.shape[2:], F32)
    zeros_fconv = jnp.zeros((depth, b) + state_ffn_conv.shape[2:], F32)
    yp, ph, plc, pk, pv, pfc = _run_trunk(x_prompt, p_mem_k, p_mem_v, zeros_h, zeros_lconv, None,
                                          zeros_fconv, p, cfg_p)
    ys, sh, slc, sk, sv, sfc = _run_trunk(x_sample, cache_mem_k, cache_mem_v, state_lru_h, state_lru_conv,
                                          (cache_attn_k, cache_attn_v), state_ffn_conv, p, cfg_s)

    dk = p["attn_q_norm"].shape[-1]
    db, dt = x_sample.shape[0], x_sample.shape[1]

    def kshape(a, n, tt):
        return jnp.stack(a).reshape(n_attn, n, tt, DIFF_HEADS, 2, dk)

    def vshape(a, n, tt):
        return jnp.stack(a).reshape(n_attn, n, tt, DIFF_HEADS, 2 * dk)

    return (yp, ys, jnp.stack(ph), jnp.stack(plc), kshape(pk, b, t), vshape(pv, b, t), p_mem_k, p_mem_v,
            jnp.stack(pfc), jnp.stack(sh), jnp.stack(slc), kshape(sk, db, dt), vshape(sv, db, dt),
            jnp.stack(sfc))


CFG_PROMPT = dict(bm=1024, bc=512, bm_ffn=512, bf=512, bm_out=1024, bn_out=1024, tq=512, tk=1024)
CFG_SAMPLE = dict(bm=512, bc=256, bm_ffn=512, bf=512, bm_out=1024, bn_out=1024)


def kernel(x_prompt, x_sample, mem_prompt, cache_attn_k, cache_attn_v, cache_mem_k, cache_mem_v, state_lru_h, state_lru_conv, state_ffn_conv, norm_mix, norm_ffn, lru_w_in, lru_conv_w, lru_conv_b, lru_gate_a_w, lru_gate_a_b, lru_gate_x_w, lru_gate_x_b, lru_lambda, lru_w_out, attn_w_in, attn_q_norm, attn_k_norm, attn_lambda, attn_subln, attn_w_out, mem_norm, mem_w_kv, mem_q_norm, mem_k_norm, ffn_w_up, ffn_conv_w, ffn_conv_b, ffn_w_down):
    p = {
        "norm_mix": norm_mix, "norm_ffn": norm_ffn,
        "lru_w_in": lru_w_in, "lru_conv_w": lru_conv_w, "lru_conv_b": lru_conv_b,
        "lru_gate_a_w": lru_gate_a_w, "lru_gate_a_b": lru_gate_a_b,
        "lru_gate_x_w": lru_gate_x_w, "lru_gate_x_b": lru_gate_x_b,
        "lru_lambda": lru_lambda, "lru_w_out": lru_w_out,
        "attn_w_in": attn_w_in, "attn_q_norm": attn_q_norm, "attn_k_norm": attn_k_norm,
        "attn_lambda": attn_lambda, "attn_subln": attn_subln, "attn_w_out": attn_w_out,
        "mem_q_norm": mem_q_norm,
        "ffn_w_up": ffn_w_up, "ffn_conv_w": ffn_conv_w, "ffn_conv_b": ffn_conv_b, "ffn_w_down": ffn_w_down,
    }
    return _forward(x_prompt, x_sample, mem_prompt, cache_attn_k, cache_attn_v, cache_mem_k, cache_mem_v,
                    state_lru_h, state_lru_conv, state_ffn_conv, p, mem_norm, mem_w_kv, mem_k_norm,
                    CFG_PROMPT, CFG_SAMPLE)
```

```python
import functools
import math

import jax
import jax.numpy as jnp
from jax import lax
from jax.experimental import pallas as pl
from jax.experimental.pallas import tpu as pltpu

F32 = jnp.float32
BF16 = jnp.bfloat16

CHUNK = 64
LRU_HEADS = 8
LRU_C = 8.0
DIFF_HEADS = 8
MEM_HEADS = 4
RMS_EPS = 1e-6
NEG_BIG = -1e30
LOG2E = 1.4426950408889634
SUBLANES = 8

V7X_VMEM_LIMIT_BYTES = 56 * 1024 * 1024


def _cparams(n_grid_axes):
    return pltpu.CompilerParams(
        dimension_semantics=("arbitrary",) * n_grid_axes,
        vmem_limit_bytes=V7X_VMEM_LIMIT_BYTES,
    )


def _dot(a, b):
    return jnp.dot(a, b, preferred_element_type=F32)


def _dot_nt(a, b):
    return lax.dot_general(a, b, (((1,), (1,)), ((), ())), preferred_element_type=F32)


def _rms(x, g):
    return x * lax.rsqrt(jnp.mean(x * x, axis=-1, keepdims=True) + RMS_EPS) * g


def _gelu(x):
    return x * (0.5 * (1.0 + jnp.tanh(0.7978845608028654 * (x + 0.044715 * (x * x * x)))))


def _cat_rows(parts):
    return parts[0] if len(parts) == 1 else jnp.concatenate(parts, axis=0)


def _delayed(seg, prev, d):
    n_prev = prev.shape[0]
    v = pltpu.roll(seg, d, 0)
    row = lax.broadcasted_iota(jnp.int32, (SUBLANES, 1), 0)
    top = v[0:SUBLANES]
    for r in range(d):
        top = jnp.where(row == r, prev[n_prev - d + r:n_prev - d + r + 1], top)
    return jnp.concatenate([top, v[SUBLANES:]], axis=0)


def _causal_conv(seg, prev, cw, cb):
    n_prev = cw.shape[0] - 1
    y = cb + _delayed(seg, prev, n_prev) * cw[0:1]
    for t in range(1, n_prev):
        y = y + _delayed(seg, prev, n_prev - t) * cw[t:t + 1]
    return y + seg * cw[n_prev:n_prev + 1]


def _scan_rows(a, b):
    n = a.shape[0]
    row = lax.broadcasted_iota(jnp.int32, (n, 1), 0)
    sh = 1
    while sh < min(SUBLANES, n):
        valid = row >= sh
        a_sh = pltpu.roll(a, sh, 0)
        b_sh = pltpu.roll(b, sh, 0)
        b = jnp.where(valid, b + a * b_sh, b)
        a = jnp.where(valid, a * a_sh, a)
        sh *= 2
    while sh < n:
        b = jnp.concatenate([b[:sh], b[sh:] + a[sh:] * b[:n - sh]], axis=0)
        a = jnp.concatenate([a[:sh], a[sh:] * a[:n - sh]], axis=0)
        sh *= 2
    return a, b


def _mem_attn(q, gq, mk_ref, mv_ref, n_seg, seg_len):
    hd = gq.shape[-1]
    outs = []
    for h in range(q.shape[-1] // hd):
        cols = slice(h * hd, (h + 1) * hd)
        qn = _rms(q[:, cols], gq).astype(BF16)
        segs = []
        for s in range(n_seg):
            k = mk_ref[s, :, cols].astype(BF16)
            v = mv_ref[s, :, cols].astype(BF16)
            sc = _dot_nt(qn[s * seg_len:(s + 1) * seg_len], k) * (hd ** -0.5)
            p = jnp.exp(sc - jnp.max(sc, axis=-1, keepdims=True))
            pr = p / jnp.sum(p, axis=-1, keepdims=True)
            segs.append(_dot(pr.astype(BF16), v))
        outs.append(_cat_rows(segs))
    return outs[0] if len(outs) == 1 else jnp.concatenate(outs, axis=-1)


class _Tiling:
    def __init__(self, n_seq, t, bm):
        self.n_seq, self.t = n_seq, t
        if t >= bm:
            assert t % bm == 0
            self.n_seg, self.seg_len, self.tiles_per_seq = 1, bm, t // bm
        else:
            assert bm % t == 0 and n_seq % (bm // t) == 0
            self.n_seg, self.seg_len, self.tiles_per_seq = bm // t, t, 1
        self.bm = self.n_seg * self.seg_len
        self.n_tiles = n_seq * t // self.bm
        assert self.seg_len & (self.seg_len - 1) == 0 and self.seg_len >= SUBLANES


def _last_tile_state(per_tile, tl):
    rows, c = per_tile.shape[2:]
    return per_tile.reshape(tl.n_seq, tl.tiles_per_seq, rows, c)[:, -1]


def _resident(block_shape, index_map):
    return pl.BlockSpec(block_shape, index_map, pipeline_mode=pl.Buffered(1))


def _mem_kv_kernel(mem_ref, gn_ref, wk_ref, wv_ref, gk_ref, k_ref, v_ref, hn_ref):
    @pl.when(pl.program_id(1) == 0)
    def _():
        hn_ref[...] = _rms(mem_ref[...], gn_ref[...]).astype(BF16)

    hn = hn_ref[...]
    k_ref[...] = _rms(_dot(hn, wk_ref[...]), gk_ref[...])
    v_ref[...] = _dot(hn, wv_ref[...])


def _mem_kv(mem2d, mem_norm, w_kv, mem_k_norm):
    depth, d, two_w = w_kv.shape
    mem_w = two_w // 2
    hd = mem_w // MEM_HEADS
    m = mem2d.shape[0]
    return pl.pallas_call(
        _mem_kv_kernel,
        grid=(depth, MEM_HEADS),
        in_specs=[
            pl.BlockSpec((m, d), lambda l, j: (0, 0)),
            pl.BlockSpec((None, 1, d), lambda l, j: (l, 0, 0)),
            pl.BlockSpec((None, d, hd), lambda l, j: (l, 0, j)),
            pl.BlockSpec((None, d, hd), lambda l, j: (l, 0, MEM_HEADS + j)),
            pl.BlockSpec((None, 1, hd), lambda l, j: (l, 0, 0)),
        ],
        out_specs=[
            pl.BlockSpec((None, m, hd), lambda l, j: (l, 0, j)),
            pl.BlockSpec((None, m, hd), lambda l, j: (l, 0, j)),
        ],
        out_shape=[jax.ShapeDtypeStruct((depth, m, mem_w), F32)] * 2,
        scratch_shapes=[pltpu.VMEM((m, d), BF16)],
        compiler_params=_cparams(2),
        name="mem_kv",
    )(mem2d, mem_norm, w_kv, w_kv, mem_k_norm)


def _lru_in_kernel(x_ref, gn_ref, wg_ref, wx_ref, cw_ref, cb_ref, wa_ref, wi_ref, ba_ref, bi_ref,
                   lam_ref, h0_ref, c0_ref, gq_ref, mk_ref, mv_ref,
                   mix_ref, mo_ref, hn_out_ref, cn_out_ref,
                   hn_ref, hc_ref, cc_ref, *, n_seg, seg_len, tiles_per_seq):
    i = pl.program_id(0)
    j = pl.program_id(1)
    carried = tiles_per_seq > 1

    @pl.when(j == 0)
    def _():
        hn_ref[...] = _rms(x_ref[...], gn_ref[...]).astype(BF16)

    @pl.when(j < LRU_HEADS)
    def _():
        hn = hn_ref[...]
        gate = _dot(hn, wg_ref[...])
        xr = _dot(hn, wx_ref[...])
        if carried:
            @pl.when(i % tiles_per_seq == 0)
            def _():
                cc_ref[j] = c0_ref[0]
                hc_ref[j] = h0_ref[0]

        cw = cw_ref[...]
        cb = cb_ref[...]
        n_prev = cw.shape[0] - 1
        xcs = []
        for s in range(n_seg):
            seg = xr[s * seg_len:(s + 1) * seg_len]
            xcs.append(_causal_conv(seg, cc_ref[j] if carried else c0_ref[s], cw, cb))
            tail = seg[seg_len - n_prev:seg_len]
            cn_out_ref[s] = tail
            if carried:
                cc_ref[j] = tail
        xc = _cat_rows(xcs)

        xcb = xc.astype(BF16)
        r = jax.nn.sigmoid(_dot(xcb, wa_ref[...]) + ba_ref[...])
        ig = jax.nn.sigmoid(_dot(xcb, wi_ref[...]) + bi_ref[...])
        log_a = (LRU_C * r) * jax.nn.log_sigmoid(lam_ref[...])
        a = jnp.exp(log_a)
        one_minus_a2 = -jnp.tanh(log_a) * (a * a + 1.0)
        gated = jnp.sqrt(one_minus_a2) * (ig * xc)
        hss = []
        for s in range(n_seg):
            sl = slice(s * seg_len, (s + 1) * seg_len)
            cum_a, cum_b = _scan_rows(a[sl], gated[sl])
            hs = cum_b + cum_a * (hc_ref[j] if carried else h0_ref[s])
            last = hs[seg_len - 1:seg_len]
            hn_out_ref[s] = last
            if carried:
                hc_ref[j] = last
            hss.append(hs)
        mix_ref[...] = (_gelu(gate) * _cat_rows(hss)).astype(BF16)

    @pl.when(j >= LRU_HEADS)
    def _():
        q = _dot(hn_ref[...], wx_ref[...])
        mo_ref[...] = _mem_attn(q, gq_ref[...], mk_ref, mv_ref, n_seg, seg_len).astype(BF16)


def _lru_in(x2d, tl, li, mi, p, h0, c0, mk, mv):
    m, d = x2d.shape
    hw = d // LRU_HEADS
    n_steps = LRU_HEADS + MEM_HEADS
    ns, tps = tl.n_seg, tl.tiles_per_seq
    width = p["lru_conv_w"].shape[1]
    last = LRU_HEADS - 1

    def hcol(j):
        return jnp.minimum(j, last)

    def mcol(j):
        return jnp.maximum(j - LRU_HEADS, 0)

    vec = pl.BlockSpec((None, 1, hw), lambda i, j: (mi, 0, hcol(j)))
    gate_w = pl.BlockSpec((None, None, hw, hw), lambda i, j: (mi, hcol(j), 0, 0))
    mem = pl.BlockSpec((None, ns, mk.shape[2], hw), lambda i, j: (li, i // tps, 0, mcol(j)))
    kern = functools.partial(_lru_in_kernel, n_seg=ns, seg_len=tl.seg_len, tiles_per_seq=tps)
    return pl.pallas_call(
        kern,
        grid=(tl.n_tiles, n_steps),
        in_specs=[
            _resident((tl.bm, d), lambda i, j: (i, 0)),
            pl.BlockSpec((None, 1, d), lambda i, j: (li, 0, 0)),
            pl.BlockSpec((None, d, hw), lambda i, j: (mi, 0, hcol(j))),
            pl.BlockSpec((None, d, hw), lambda i, j: (mi, 0, LRU_HEADS + j)),
            pl.BlockSpec((None, width, hw), lambda i, j: (mi, 0, hcol(j))),
            vec, gate_w, gate_w, vec, vec, vec,
            pl.BlockSpec((None, ns, 1, hw), lambda i, j: (mi, i // tps, 0, hcol(j))),
            pl.BlockSpec((None, ns, width - 1, hw), lambda i, j: (mi, i // tps, 0, hcol(j))),
            pl.BlockSpec((None, 1, hw), lambda i, j: (li, 0, 0)),
            mem, mem,
        ],
        out_specs=[
            pl.BlockSpec((tl.bm, hw), lambda i, j: (i, hcol(j))),
            pl.BlockSpec((tl.bm, hw), lambda i, j: (i, mcol(j))),
            pl.BlockSpec((None, ns, 1, hw), lambda i, j: (i, 0, 0, hcol(j))),
            pl.BlockSpec((None, ns, width - 1, hw), lambda i, j: (i, 0, 0, hcol(j))),
        ],
        out_shape=[
            jax.ShapeDtypeStruct((m, d), BF16),
            jax.ShapeDtypeStruct((m, MEM_HEADS * hw), BF16),
            jax.ShapeDtypeStruct((tl.n_tiles, ns, 1, d), F32),
            jax.ShapeDtypeStruct((tl.n_tiles, ns, width - 1, d), F32),
        ],
        scratch_shapes=[
            pltpu.VMEM((tl.bm, d), BF16),
            pltpu.VMEM((LRU_HEADS, 1, hw), F32),
            pltpu.VMEM((LRU_HEADS, width - 1, hw), F32),
        ],
        compiler_params=_cparams(2),
        name="lru_in",
    )(x2d, p["norm_mix"], p["lru_w_in"], p["lru_w_in"], p["lru_conv_w"], p["lru_conv_b"],
      p["lru_gate_a_w"], p["lru_gate_x_w"], p["lru_gate_a_b"], p["lru_gate_x_b"], p["lru_lambda"],
      h0, c0, p["mem_q_norm"], mk, mv)


def _att_in_kernel(x_ref, gn_ref, w_ref, qg_ref, kg_ref, gq_ref, mk_ref, mv_ref,
                   qb_ref, kf_ref, kb_ref, vf_ref, vb_ref, mo_ref, hn_ref, *, n_seg, seg_len, nq):
    j = pl.program_id(1)

    @pl.when(j == 0)
    def _():
        hn_ref[...] = _rms(x_ref[...], gn_ref[...]).astype(BF16)

    z = _dot(hn_ref[...], w_ref[...])
    dk = qg_ref.shape[-1]

    @pl.when(j < nq)
    def _():
        qscale = dk ** -0.5 * LOG2E
        for c in range(z.shape[-1] // dk):
            sl = slice(c * dk, (c + 1) * dk)
            qb_ref[:, sl] = (_rms(z[:, sl], qg_ref[...]) * qscale).astype(BF16)

    @pl.when((j >= nq) & (j < 2 * nq))
    def _():
        for c in range(z.shape[-1] // dk):
            sl = slice(c * dk, (c + 1) * dk)
            kn = _rms(z[:, sl], kg_ref[...])
            kf_ref[:, sl] = kn
            kb_ref[:, sl] = kn.astype(BF16)

    @pl.when((j >= 2 * nq) & (j < 3 * nq))
    def _():
        vf_ref[...] = z
        vb_ref[...] = z.astype(BF16)

    @pl.when(j >= 3 * nq)
    def _():
        mo_ref[...] = _mem_attn(z, gq_ref[...], mk_ref, mv_ref, n_seg, seg_len).astype(BF16)


def _att_in(x2d, tl, li, ai, p, mk, mv, bc):
    m, d = x2d.shape
    mem_w = mk.shape[-1]
    nq, nm = d // bc, mem_w // bc
    n_steps = 3 * nq + nm
    ns, tps = tl.n_seg, tl.tiles_per_seq
    dk = p["attn_q_norm"].shape[-1]
    hd = p["mem_q_norm"].shape[-1]

    def col(lo, n):
        return lambda i, j: (i, jnp.clip(j - lo, 0, n - 1))

    mem = pl.BlockSpec((None, ns, mk.shape[2], bc),
                       lambda i, j: (li, i // tps, 0, jnp.clip(j - 3 * nq, 0, nm - 1)))
    kern = functools.partial(_att_in_kernel, n_seg=ns, seg_len=tl.seg_len, nq=nq)
    return pl.pallas_call(
        kern,
        grid=(tl.n_tiles, n_steps),
        in_specs=[
            _resident((tl.bm, d), lambda i, j: (i, 0)),
            pl.BlockSpec((None, 1, d), lambda i, j: (li, 0, 0)),
            pl.BlockSpec((None, d, bc), lambda i, j: (ai, 0, j)),
            pl.BlockSpec((None, 1, dk), lambda i, j: (ai, 0, 0)),
            pl.BlockSpec((None, 1, dk), lambda i, j: (ai, 0, 0)),
            pl.BlockSpec((None, 1, hd), lambda i, j: (li, 0, 0)),
            mem, mem,
        ],
        out_specs=[
            pl.BlockSpec((tl.bm, bc), col(0, nq)),
            pl.BlockSpec((tl.bm, bc), col(nq, nq)),
            pl.BlockSpec((tl.bm, bc), col(nq, nq)),
            pl.BlockSpec((tl.bm, bc), col(2 * nq, nq)),
            pl.BlockSpec((tl.bm, bc), col(2 * nq, nq)),
            pl.BlockSpec((tl.bm, bc), col(3 * nq, nm)),
        ],
        out_shape=[
            jax.ShapeDtypeStruct((m, d), BF16),
            jax.ShapeDtypeStruct((m, d), F32),
            jax.ShapeDtypeStruct((m, d), BF16),
            jax.ShapeDtypeStruct((m, d), F32),
            jax.ShapeDtypeStruct((m, d), BF16),
            jax.ShapeDtypeStruct((m, mem_w), BF16),
        ],
        scratch_shapes=[pltpu.VMEM((tl.bm, d), BF16)],
        compiler_params=_cparams(2),
        name="att_in",
    )(x2d, p["norm_mix"], p["attn_w_in"], p["attn_q_norm"], p["attn_k_norm"], p["mem_q_norm"], mk, mv)


def _attn_block(q, k, v, m_ref, l_ref, acc_ref, mask):
    dk = q.shape[-1] // 2
    tk = k.shape[0]
    for c in range(2):
        sl = slice(c * dk, (c + 1) * dk)
        s = _dot_nt(k[:, sl], q[:, sl])
        if mask is not None:
            s = jnp.where(mask, s, NEG_BIG)
        m_old = m_ref[c]
        m_new = jnp.maximum(m_old, jnp.max(s, axis=0, keepdims=True))
        alpha = jnp.exp2(m_old - m_new)
        p = jnp.exp2(s - m_new)
        l_ref[c] = alpha * l_ref[c] + jnp.sum(p.reshape(tk // SUBLANES, SUBLANES, p.shape[-1]), axis=0)
        pv = lax.dot_general(v, p.astype(BF16), (((0,), (0,)), ((), ())), preferred_element_type=F32)
        acc_ref[c] = alpha * acc_ref[c] + pv
        m_ref[c] = m_new


def _attn_init(m_ref, l_ref, acc_ref):
    m_ref[...] = jnp.full(m_ref.shape, NEG_BIG, F32)
    l_ref[...] = jnp.zeros(l_ref.shape, F32)
    acc_ref[...] = jnp.zeros(acc_ref.shape, F32)


def _attn_finish(lp_ref, sg_ref, l_ref, acc_ref, lam_init):
    lp = lp_ref[...]
    lam = (jnp.exp(jnp.sum(lp[0:1] * lp[1:2], axis=-1, keepdims=True))
           - jnp.exp(jnp.sum(lp[2:3] * lp[3:4], axis=-1, keepdims=True)) + lam_init)
    l0 = jnp.sum(l_ref[0], axis=0, keepdims=True)
    l1 = jnp.sum(l_ref[1], axis=0, keepdims=True)
    o = (acc_ref[0] / l0 - lam * (acc_ref[1] / l1)).T
    return (_rms(o, sg_ref[...]) * (1.0 - lam_init)).astype(BF16)


def _attn_prompt_kernel(lp_ref, sg_ref, q_ref, k_ref, v_ref, o_ref, m_ref, l_ref, acc_ref, *, tq, tk, lam_init):
    qi = pl.program_id(2)
    _attn_init(m_ref, l_ref, acc_ref)
    q = q_ref[...]
    per = tk // tq

    def visible(start, size):
        start = pl.multiple_of(start, size)
        _attn_block(q, k_ref[pl.ds(start, size), :], v_ref[pl.ds(start, size), :], m_ref, l_ref, acc_ref, None)

    def body(kv, carry):
        visible(kv * tk, tk)
        return carry

    n_big = qi // per
    lax.fori_loop(0, n_big, body, 0)
    for r in range(1, per):
        @pl.when(qi % per >= r)
        def _():
            visible(n_big * tk + (r - 1) * tq, tq)

    start = pl.multiple_of(qi * tq, tq)
    shift = CHUNK.bit_length() - 1
    keyc = lax.shift_right_logical(lax.broadcasted_iota(jnp.int32, (tq, tq), 0), shift)
    qryc = lax.shift_right_logical(lax.broadcasted_iota(jnp.int32, (tq, tq), 1), shift)
    _attn_block(q, k_ref[pl.ds(start, tq), :], v_ref[pl.ds(start, tq), :], m_ref, l_ref, acc_ref, keyc <= qryc)
    o_ref[...] = _attn_finish(lp_ref, sg_ref, l_ref, acc_ref, lam_init)


def _attn_prompt(qb, kb, vb, ai, p, n_seq, t, tq, tk, lam_init):
    d = qb.shape[-1]
    hw = d // DIFF_HEADS
    assert t % tq == 0 and tq % CHUNK == 0 and tk % tq == 0
    q3, k3, v3 = (a.reshape(n_seq, t, d) for a in (qb, kb, vb))
    lp, sg = p["attn_lambda"], p["attn_subln"]
    kern = functools.partial(_attn_prompt_kernel, tq=tq, tk=tk, lam_init=lam_init)
    out = pl.pallas_call(
        kern,
        grid=(n_seq, DIFF_HEADS, t // tq),
        in_specs=[
            pl.BlockSpec((None,) + lp.shape[1:], lambda b, h, qi: (ai, 0, 0)),
            pl.BlockSpec((None, 1, hw), lambda b, h, qi: (ai, 0, 0)),
            pl.BlockSpec((None, tq, hw), lambda b, h, qi: (b, qi, h)),
            pl.BlockSpec((None, t, hw), lambda b, h, qi: (b, 0, h)),
            pl.BlockSpec((None, t, hw), lambda b, h, qi: (b, 0, h)),
        ],
        out_specs=pl.BlockSpec((None, tq, hw), lambda b, h, qi: (b, qi, h)),
        out_shape=jax.ShapeDtypeStruct((n_seq, t, d), BF16),
        scratch_shapes=[
            pltpu.VMEM((2, 1, tq), F32),
            pltpu.VMEM((2, SUBLANES, tq), F32),
            pltpu.VMEM((2, hw, tq), F32),
        ],
        compiler_params=_cparams(3),
        name="attn_prompt",
    )(lp, sg, q3, k3, v3)
    return out.reshape(n_seq * t, d)


def _attn_cached_kernel(lp_ref, sg_ref, q_ref, kp_ref, vp_ref, kn_ref, vn_ref, o_ref,
                        m_ref, l_ref, acc_ref, *, past, lam_init):
    n_heads = DIFF_HEADS
    rows_per_t = 2 * n_heads
    hw = q_ref.shape[-1] // n_heads
    for h in range(n_heads):
        cols = slice(h * hw, (h + 1) * hw)
        k = jnp.concatenate([kp_ref[pl.ds(2 * h + c, past, stride=rows_per_t), :] for c in range(2)], axis=-1)
        v = jnp.concatenate([vp_ref[pl.ds(c * n_heads + h, past, stride=rows_per_t), :] for c in range(2)],
                            axis=-1)
        q = q_ref[:, cols]
        _attn_init(m_ref, l_ref, acc_ref)
        _attn_block(q, k.astype(BF16), v.astype(BF16), m_ref, l_ref, acc_ref, None)
        _attn_block(q, kn_ref[:, cols], vn_ref[:, cols], m_ref, l_ref, acc_ref, None)
        o_ref[:, cols] = _attn_finish(lp_ref, sg_ref, l_ref, acc_ref, lam_init)


def _attn_cached(qb, kb, vb, k_past, v_past, ai, p, n_seq, t, lam_init):
    d = qb.shape[-1]
    hw = d // DIFF_HEADS
    n_attn, _, past, n_heads, _, dk = k_past.shape
    assert past % CHUNK == 0 and t <= CHUNK and n_heads == DIFF_HEADS and 2 * dk == hw
    q3, k3, v3 = (a.reshape(n_seq, t, d) for a in (qb, kb, vb))
    lp, sg = p["attn_lambda"], p["attn_subln"]
    k_rows = k_past.reshape(n_attn * n_seq, past * 2 * n_heads, dk)
    v_rows = v_past.reshape(n_attn, n_seq, past, n_heads, 2, dk).transpose(0, 1, 2, 4, 3, 5)
    v_rows = v_rows.reshape(n_attn * n_seq, past * 2 * n_heads, dk)
    kern = functools.partial(_attn_cached_kernel, past=past, lam_init=lam_init)
    new_spec = pl.BlockSpec((None, t, d), lambda b: (b, 0, 0))
    past_spec = pl.BlockSpec((None, past * 2 * n_heads, dk), lambda b: (ai * n_seq + b, 0, 0))
    out = pl.pallas_call(
        kern,
        grid=(n_seq,),
        in_specs=[
            pl.BlockSpec((None,) + lp.shape[1:], lambda b: (ai, 0, 0)),
            pl.BlockSpec((None, 1, hw), lambda b: (ai, 0, 0)),
            new_spec, past_spec, past_spec, new_spec, new_spec,
        ],
        out_specs=new_spec,
        out_shape=jax.ShapeDtypeStruct((n_seq, t, d), BF16),
        scratch_shapes=[
            pltpu.VMEM((2, 1, t), F32),
            pltpu.VMEM((2, SUBLANES, t), F32),
            pltpu.VMEM((2, hw, t), F32),
        ],
        compiler_params=_cparams(1),
        name="attn_cached",
    )(lp, sg, q3, k_rows, v_rows, k3, v3)
    return out.reshape(n_seq * t, d)


def _out_kernel(x_ref, mix_ref, mo_ref, w1_ref, w2_ref, o_ref):
    o_ref[...] = x_ref[...] + (_dot(mix_ref[...], w1_ref[...]) + _dot(mo_ref[...], w2_ref[...]))


def _out_proj(x2d, mix, mo, w_out, wi, bm, bn):
    m, d = x2d.shape
    k1, k2 = mix.shape[1], mo.shape[1]
    assert k1 % k2 == 0
    return pl.pallas_call(
        _out_kernel,
        grid=(m // bm, d // bn),
        in_specs=[
            pl.BlockSpec((bm, bn), lambda i, j: (i, j)),
            pl.BlockSpec((bm, k1), lambda i, j: (i, 0)),
            pl.BlockSpec((bm, k2), lambda i, j: (i, 0)),
            pl.BlockSpec((None, k1, bn), lambda i, j: (wi, 0, j)),
            pl.BlockSpec((None, k2, bn), lambda i, j: (wi, k1 // k2, j)),
        ],
        out_specs=pl.BlockSpec((bm, bn), lambda i, j: (i, j)),
        out_shape=jax.ShapeDtypeStruct((m, d), F32),
        compiler_params=_cparams(2),
        name="out_proj",
    )(x2d, mix, mo, w_out, w_out)


def _ffn_kernel(x_ref, gn_ref, wg_ref, wv_ref, cwg_ref, cwv_ref, cbg_ref, cbv_ref, stg_ref, stv_ref, wdn_ref,
                o_ref, nstg_ref, nstv_ref, hn_ref, cg_ref, cv_ref, raw_ref, act_ref,
                *, n_seg, seg_len, tiles_per_seq, row_chunk):
    i = pl.program_id(0)
    j = pl.program_id(1)
    carried = tiles_per_seq > 1

    @pl.when(j == 0)
    def _():
        x = x_ref[...]
        hn_ref[...] = _rms(x, gn_ref[...]).astype(BF16)
        o_ref[...] = x

    if carried:
        @pl.when(i % tiles_per_seq == 0)
        def _():
            cg_ref[j] = stg_ref[0]
            cv_ref[j] = stv_ref[0]

    halves = ((wg_ref, cwg_ref, cbg_ref, stg_ref, cg_ref, nstg_ref),
              (wv_ref, cwv_ref, cbv_ref, stv_ref, cv_ref, nstv_ref))
    n_prev = cwg_ref.shape[0] - 1
    n_chunks = hn_ref.shape[0] // row_chunk
    segs_per_chunk = row_chunk // seg_len if n_seg > 1 else 0
    prevs = [(h[4][j] if carried else h[3][0]) if n_seg == 1 else None for h in halves]

    def rows(c):
        return slice(c * row_chunk, (c + 1) * row_chunk)

    def up_proj(c):
        hn = hn_ref[rows(c), :]
        for hi, h in enumerate(halves):
            raw_ref[hi, rows(c), :] = _dot(hn, h[0][...])

    def conv_gate(c):
        ys = []
        for hi, (_, cw_ref, cb_ref, st_ref, _, nst_ref) in enumerate(halves):
            up = raw_ref[hi, rows(c), :]
            cw = cw_ref[...]
            cb = cb_ref[...]
            if n_seg == 1:
                ys.append(_causal_conv(up, prevs[hi], cw, cb))
                prevs[hi] = up[row_chunk - n_prev:row_chunk]
            else:
                parts = []
                for s in range(segs_per_chunk):
                    seg = up[s * seg_len:(s + 1) * seg_len]
                    parts.append(_causal_conv(seg, st_ref[c * segs_per_chunk + s], cw, cb))
                    nst_ref[c * segs_per_chunk + s] = seg[seg_len - n_prev:seg_len]
                ys.append(_cat_rows(parts))
        act_ref[rows(c), :] = (_gelu(ys[0]) * ys[1]).astype(BF16)

    def down_proj(c):
        o_ref[rows(c), :] += _dot(act_ref[rows(c), :], wdn_ref[...])

    for t in range(n_chunks + 2):
        if t < n_chunks:
            up_proj(t)
        if 0 <= t - 1 < n_chunks:
            conv_gate(t - 1)
        if 0 <= t - 2 < n_chunks:
            down_proj(t - 2)
    if n_seg == 1:
        for hi, (_, _, _, _, carry_ref, nst_ref) in enumerate(halves):
            nst_ref[0] = prevs[hi]
            if carried:
                carry_ref[j] = prevs[hi]


def _ffn(x2d, tl, li, p, state, bf, row_chunk):
    m, d = x2d.shape
    dff = p["ffn_w_down"].shape[1]
    nj = dff // bf
    ns, tps = tl.n_seg, tl.tiles_per_seq
    width = p["ffn_conv_w"].shape[1]
    assert tl.bm % row_chunk == 0 and (row_chunk % tl.seg_len == 0 if ns > 1 else True)
    kern = functools.partial(_ffn_kernel, n_seg=ns, seg_len=tl.seg_len, tiles_per_seq=tps, row_chunk=row_chunk)

    def halves(block, index_map):
        gate = pl.BlockSpec(block, lambda i, j: index_map(i, j, j))
        value = pl.BlockSpec(block, lambda i, j: index_map(i, j, nj + j))
        return [gate, value]

    nst_spec = pl.BlockSpec((None, ns, width - 1, bf), lambda i, j: (i, 0, 0, j))
    nst_shape = jax.ShapeDtypeStruct((tl.n_tiles, ns, width - 1, dff), F32)
    return pl.pallas_call(
        kern,
        grid=(tl.n_tiles, nj),
        in_specs=[
            _resident((tl.bm, d), lambda i, j: (i, 0)),
            pl.BlockSpec((None, 1, d), lambda i, j: (li, 0, 0)),
            *halves((None, d, bf), lambda i, j, c: (li, 0, c)),
            *halves((None, width, bf), lambda i, j, c: (li, 0, c)),
            *halves((None, 1, bf), lambda i, j, c: (li, 0, c)),
            *halves((None, ns, width - 1, bf), lambda i, j, c: (li, i // tps, 0, c)),
            pl.BlockSpec((None, bf, d), lambda i, j: (li, j, 0)),
        ],
        out_specs=[pl.BlockSpec((tl.bm, d), lambda i, j: (i, 0)), nst_spec, nst_spec],
        out_shape=[jax.ShapeDtypeStruct((m, d), F32), nst_shape, nst_shape],
        scratch_shapes=[
            pltpu.VMEM((tl.bm, d), BF16),
            pltpu.VMEM((nj, width - 1, bf), F32),
            pltpu.VMEM((nj, width - 1, bf), F32),
            pltpu.VMEM((2, tl.bm, bf), F32),
            pltpu.VMEM((tl.bm, bf), BF16),
        ],
        compiler_params=_cparams(2),
        name="ffn",
    )(x2d, p["norm_ffn"], p["ffn_w_up"], p["ffn_w_up"], p["ffn_conv_w"], p["ffn_conv_w"],
      p["ffn_conv_b"], p["ffn_conv_b"], state, state, p["ffn_w_down"])


def _run_trunk(x, mem_k, mem_v, lru_h, lru_conv, attn_past, ffn_conv, p, cfg):
    n_seq, t, d = x.shape
    depth = p["norm_mix"].shape[0]
    x2d = x.reshape(n_seq * t, d)
    tl = _Tiling(n_seq, t, cfg["bm"])
    tl_ffn = _Tiling(n_seq, t, cfg["bm_ffn"])
    mem_tokens = mem_k.shape[2]
    mk = mem_k.reshape(depth, n_seq, mem_tokens, -1)
    mv = mem_v.reshape(depth, n_seq, mem_tokens, -1)
    h0 = lru_h[:, :, None, :]
    new_h, new_lconv, new_k, new_v, new_fconv = [], [], [], [], []
    for i in range(depth):
        j = i // 2
        if i % 2 == 0:
            mix, mo, h_last, buf = _lru_in(x2d, tl, i, j, p, h0, lru_conv, mk, mv)
            new_h.append(_last_tile_state(h_last, tl)[:, 0, :])
            new_lconv.append(_last_tile_state(buf, tl))
            w_out = p["lru_w_out"]
        else:
            qb, kf, kb, vf, vb, mo = _att_in(x2d, tl, i, j, p, mk, mv, cfg["bc"])
            new_k.append(kf)
            new_v.append(vf)
            lam_init = 0.8 - 0.6 * math.exp(-0.3 * i)
            if attn_past is None:
                mix = _attn_prompt(qb, kb, vb, j, p, n_seq, t, cfg["tq"], cfg["tk"], lam_init)
            else:
                mix = _attn_cached(qb, kb, vb, attn_past[0], attn_past[1], j, p, n_seq, t, lam_init)
            w_out = p["attn_w_out"]
        x2d = _out_proj(x2d, mix, mo, w_out, j, cfg["bm_out"], cfg["bn_out"])
        x2d, fg, fv = _ffn(x2d, tl_ffn, i, p, ffn_conv, cfg["bf"], cfg["ffn_rows"])
        new_fconv.append(jnp.concatenate([_last_tile_state(fg, tl_ffn), _last_tile_state(fv, tl_ffn)], axis=-1))
    return x2d.reshape(n_seq, t, d), new_h, new_lconv, new_k, new_v, new_fconv


def _forward(x_prompt, x_sample, mem_prompt, cache_attn_k, cache_attn_v, cache_mem_k, cache_mem_v,
             state_lru_h, state_lru_conv, state_ffn_conv, p, mem_norm, mem_w_kv, mem_k_norm, cfg_p, cfg_s):
    b, t, d = x_prompt.shape
    depth = p["norm_mix"].shape[0]
    n_lru, n_attn = (depth + 1) // 2, depth // 2

    p = dict(p)
    for name in ("lru_w_in", "lru_gate_a_w", "lru_gate_x_w", "lru_w_out", "attn_w_in", "attn_w_out",
                 "ffn_w_up", "ffn_w_down"):
        p[name] = p[name].astype(BF16)
    for name in ("norm_mix", "norm_ffn", "lru_conv_b", "lru_gate_a_b", "lru_gate_x_b", "lru_lambda",
                 "attn_q_norm", "attn_k_norm", "attn_subln", "mem_q_norm", "ffn_conv_b"):
        p[name] = p[name][:, None, :]

    mem_tokens = mem_prompt.shape[1]
    mk, mv = _mem_kv(mem_prompt.reshape(b * mem_tokens, d), mem_norm[:, None, :], mem_w_kv.astype(BF16),
                     mem_k_norm[:, None, :])
    hd = mem_k_norm.shape[-1]
    p_mem_k = mk.reshape(depth, b, mem_tokens, MEM_HEADS, hd)
    p_mem_v = mv.reshape(depth, b, mem_tokens, MEM_HEADS, hd)

    zeros_h = jnp.zeros((n_lru, b, d), F32)
    zeros_lconv = jnp.zeros((n_lru, b) + state_lru_conv.shape[2:], F32)
    zeros_fconv = jnp.zeros((depth, b) + state_ffn_conv.shape[2:], F32)
    yp, ph, plc, pk, pv, pfc = _run_trunk(x_prompt, p_mem_k, p_mem_v, zeros_h, zeros_lconv, None,
                                          zeros_fconv, p, cfg_p)
    ys, sh, slc, sk, sv, sfc = _run_trunk(x_sample, cache_mem_k, cache_mem_v, state_lru_h, state_lru_conv,
                                          (cache_attn_k, cache_attn_v), state_ffn_conv, p, cfg_s)

    dk = p["attn_q_norm"].shape[-1]
    db, dt = x_sample.shape[0], x_sample.shape[1]

    def kshape(a, n, tt):
        return jnp.stack(a).reshape(n_attn, n, tt, DIFF_HEADS, 2, dk)

    def vshape(a, n, tt):
        return jnp.stack(a).reshape(n_attn, n, tt, DIFF_HEADS, 2 * dk)

    return (yp, ys, jnp.stack(ph), jnp.stack(plc), kshape(pk, b, t), vshape(pv, b, t), p_mem_k, p_mem_v,
            jnp.stack(pfc), jnp.stack(sh), jnp.stack(slc), kshape(sk, db, dt), vshape(sv, db, dt),
            jnp.stack(sfc))


CFG_PROMPT = dict(bm=1024, bc=512, bm_ffn=1024, bf=512, ffn_rows=512, bm_out=1024, bn_out=1024, tq=512, tk=1024)
CFG_SAMPLE = dict(bm=512, bc=256, bm_ffn=1024, bf=512, ffn_rows=512, bm_out=1024, bn_out=1024)


def kernel(x_prompt, x_sample, mem_prompt, cache_attn_k, cache_attn_v, cache_mem_k, cache_mem_v, state_lru_h, state_lru_conv, state_ffn_conv, norm_mix, norm_ffn, lru_w_in, lru_conv_w, lru_conv_b, lru_gate_a_w, lru_gate_a_b, lru_gate_x_w, lru_gate_x_b, lru_lambda, lru_w_out, attn_w_in, attn_q_norm, attn_k_norm, attn_lambda, attn_subln, attn_w_out, mem_norm, mem_w_kv, mem_q_norm, mem_k_norm, ffn_w_up, ffn_conv_w, ffn_conv_b, ffn_w_down):
    p = {
        "norm_mix": norm_mix, "norm_ffn": norm_ffn,
        "lru_w_in": lru_w_in, "lru_conv_w": lru_conv_w, "lru_conv_b": lru_conv_b,
        "lru_gate_a_w": lru_gate_a_w, "lru_gate_a_b": lru_gate_a_b,
        "lru_gate_x_w": lru_gate_x_w, "lru_gate_x_b": lru_gate_x_b,
        "lru_lambda": lru_lambda, "lru_w_out": lru_w_out,
        "attn_w_in": attn_w_in, "attn_q_norm": attn_q_norm, "attn_k_norm": attn_k_norm,
        "attn_lambda": attn_lambda, "attn_subln": attn_subln, "attn_w_out": attn_w_out,
        "mem_q_norm": mem_q_norm,
        "ffn_w_up": ffn_w_up, "ffn_conv_w": ffn_conv_w, "ffn_conv_b": ffn_conv_b, "ffn_w_down": ffn_w_down,
    }
    return _forward(x_prompt, x_sample, mem_prompt, cache_attn_k, cache_attn_v, cache_mem_k, cache_mem_v,
                    state_lru_h, state_lru_conv, state_ffn_conv, p, mem_norm, mem_w_kv, mem_k_norm,
                    CFG_PROMPT, CFG_SAMPLE)
```

```python
import functools
import math

import jax
import jax.numpy as jnp
from jax import lax
from jax.experimental import pallas as pl
from jax.experimental.pallas import tpu as pltpu

F32 = jnp.float32
BF16 = jnp.bfloat16

CHUNK = 64
LRU_HEADS = 8
LRU_C = 8.0
DIFF_HEADS = 8
MEM_HEADS = 4
RMS_EPS = 1e-6
NEG_BIG = -1e30
LOG2E = 1.4426950408889634
SUBLANES = 8
MEM_ATTN_ROWS = 512

V7X_VMEM_LIMIT_BYTES = 56 * 1024 * 1024


def _cparams(n_grid_axes):
    return pltpu.CompilerParams(
        dimension_semantics=("arbitrary",) * n_grid_axes,
        vmem_limit_bytes=V7X_VMEM_LIMIT_BYTES,
    )


def _dot(a, b):
    return jnp.dot(a, b, preferred_element_type=F32)


def _dot_nt(a, b):
    return lax.dot_general(a, b, (((1,), (1,)), ((), ())), preferred_element_type=F32)


def _rms(x, g):
    return x * lax.rsqrt(jnp.mean(x * x, axis=-1, keepdims=True) + RMS_EPS) * g


def _gelu(x):
    return x * (0.5 * (1.0 + jnp.tanh(0.7978845608028654 * (x + 0.044715 * (x * x * x)))))


def _cat_rows(parts):
    return parts[0] if len(parts) == 1 else jnp.concatenate(parts, axis=0)


def _delayed(seg, prev, d):
    n_prev = prev.shape[0]
    v = pltpu.roll(seg, d, 0)
    row = lax.broadcasted_iota(jnp.int32, (SUBLANES, 1), 0)
    top = v[0:SUBLANES]
    for r in range(d):
        top = jnp.where(row == r, prev[n_prev - d + r:n_prev - d + r + 1], top)
    return jnp.concatenate([top, v[SUBLANES:]], axis=0)


def _causal_conv(seg, prev, cw, cb):
    n_prev = cw.shape[0] - 1
    y = cb + _delayed(seg, prev, n_prev) * cw[0:1]
    for t in range(1, n_prev):
        y = y + _delayed(seg, prev, n_prev - t) * cw[t:t + 1]
    return y + seg * cw[n_prev:n_prev + 1]


def _scan_rows(a, b):
    n = a.shape[0]
    row = lax.broadcasted_iota(jnp.int32, (n, 1), 0)
    sh = 1
    while sh < min(SUBLANES, n):
        valid = row >= sh
        a_sh = pltpu.roll(a, sh, 0)
        b_sh = pltpu.roll(b, sh, 0)
        b = jnp.where(valid, b + a * b_sh, b)
        a = jnp.where(valid, a * a_sh, a)
        sh *= 2
    while sh < n:
        b = jnp.concatenate([b[:sh], b[sh:] + a[sh:] * b[:n - sh]], axis=0)
        a = jnp.concatenate([a[:sh], a[sh:] * a[:n - sh]], axis=0)
        sh *= 2
    return a, b


def _two_stage(n_chunks, first, second):
    for t in range(n_chunks + 1):
        if t < n_chunks:
            first(t)
        if t >= 1:
            second(t - 1)


def _mem_attn(q, gq, mk_ref, mv_ref, seg0, n_seg, seg_len):
    hd = gq.shape[-1]
    outs = []
    for h in range(q.shape[-1] // hd):
        cols = slice(h * hd, (h + 1) * hd)
        qn = _rms(q[:, cols], gq).astype(BF16)
        segs = []
        for s in range(n_seg):
            k = mk_ref[seg0 + s, :, cols].astype(BF16)
            v = mv_ref[seg0 + s, :, cols].astype(BF16)
            sc = _dot_nt(qn[s * seg_len:(s + 1) * seg_len], k) * (hd ** -0.5)
            p = jnp.exp(sc - jnp.max(sc, axis=-1, keepdims=True))
            pr = p / jnp.sum(p, axis=-1, keepdims=True)
            segs.append(_dot(pr.astype(BF16), v))
        outs.append(_cat_rows(segs))
    return outs[0] if len(outs) == 1 else jnp.concatenate(outs, axis=-1)


class _Tiling:
    def __init__(self, n_seq, t, bm):
        self.n_seq, self.t = n_seq, t
        if t >= bm:
            assert t % bm == 0
            self.n_seg, self.seg_len, self.tiles_per_seq = 1, bm, t // bm
        else:
            assert bm % t == 0 and n_seq % (bm // t) == 0
            self.n_seg, self.seg_len, self.tiles_per_seq = bm // t, t, 1
        self.bm = self.n_seg * self.seg_len
        self.n_tiles = n_seq * t // self.bm
        assert self.seg_len & (self.seg_len - 1) == 0 and self.seg_len >= SUBLANES


def _last_tile_state(per_tile, tl):
    rows, c = per_tile.shape[2:]
    return per_tile.reshape(tl.n_seq, tl.tiles_per_seq, rows, c)[:, -1]


def _resident(block_shape, index_map):
    return pl.BlockSpec(block_shape, index_map, pipeline_mode=pl.Buffered(1))


def _mem_kv_kernel(mem_ref, gn_ref, wk_ref, wv_ref, gk_ref, k_ref, v_ref, hn_ref):
    @pl.when(pl.program_id(1) == 0)
    def _():
        hn_ref[...] = _rms(mem_ref[...], gn_ref[...]).astype(BF16)

    hn = hn_ref[...]
    k_ref[...] = _rms(_dot(hn, wk_ref[...]), gk_ref[...])
    v_ref[...] = _dot(hn, wv_ref[...])


def _mem_kv(mem2d, mem_norm, w_kv, mem_k_norm):
    depth, d, two_w = w_kv.shape
    mem_w = two_w // 2
    hd = mem_w // MEM_HEADS
    m = mem2d.shape[0]
    return pl.pallas_call(
        _mem_kv_kernel,
        grid=(depth, MEM_HEADS),
        in_specs=[
            pl.BlockSpec((m, d), lambda l, j: (0, 0)),
            pl.BlockSpec((None, 1, d), lambda l, j: (l, 0, 0)),
            pl.BlockSpec((None, d, hd), lambda l, j: (l, 0, j)),
            pl.BlockSpec((None, d, hd), lambda l, j: (l, 0, MEM_HEADS + j)),
            pl.BlockSpec((None, 1, hd), lambda l, j: (l, 0, 0)),
        ],
        out_specs=[
            pl.BlockSpec((None, m, hd), lambda l, j: (l, 0, j)),
            pl.BlockSpec((None, m, hd), lambda l, j: (l, 0, j)),
        ],
        out_shape=[jax.ShapeDtypeStruct((depth, m, mem_w), F32)] * 2,
        scratch_shapes=[pltpu.VMEM((m, d), BF16)],
        compiler_params=_cparams(2),
        name="mem_kv",
    )(mem2d, mem_norm, w_kv, w_kv, mem_k_norm)


def _lru_in_kernel(x_ref, gn_ref, wg_ref, wx_ref, cw_ref, cb_ref, wa_ref, wi_ref, ba_ref, bi_ref,
                   lam_ref, h0_ref, c0_ref, gq_ref, mk_ref, mv_ref,
                   mix_ref, mo_ref, hn_out_ref, cn_out_ref,
                   hn_ref, hc_ref, cc_ref, raw_ref, *, n_seg, seg_len, tiles_per_seq, row_chunk):
    i = pl.program_id(0)
    j = pl.program_id(1)
    carried = tiles_per_seq > 1

    @pl.when(j == 0)
    def _():
        hn_ref[...] = _rms(x_ref[...], gn_ref[...]).astype(BF16)

    @pl.when(j < LRU_HEADS)
    def _():
        if carried:
            @pl.when(i % tiles_per_seq == 0)
            def _():
                cc_ref[j] = c0_ref[0]
                hc_ref[j] = h0_ref[0]

        cw = cw_ref[...]
        cb = cb_ref[...]
        n_prev = cw.shape[0] - 1
        n_chunks = hn_ref.shape[0] // row_chunk
        segs_per_chunk = row_chunk // seg_len if n_seg > 1 else 0
        whole = n_seg == 1
        state = {"conv": (cc_ref[j] if carried else c0_ref[0]) if whole else None,
                 "h": (hc_ref[j] if carried else h0_ref[0]) if whole else None}
        log_lam = jax.nn.log_sigmoid(lam_ref[...])

        def rows(c):
            return slice(c * row_chunk, (c + 1) * row_chunk)

        def in_proj(c):
            hn = hn_ref[rows(c), :]
            raw_ref[0, rows(c), :] = _dot(hn, wg_ref[...])
            raw_ref[1, rows(c), :] = _dot(hn, wx_ref[...])

        def conv_gates(c):
            xr = raw_ref[1, rows(c), :]
            if whole:
                xc = _causal_conv(xr, state["conv"], cw, cb)
                state["conv"] = xr[row_chunk - n_prev:row_chunk]
            else:
                parts = []
                for s in range(segs_per_chunk):
                    sidx = c * segs_per_chunk + s
                    seg = xr[s * seg_len:(s + 1) * seg_len]
                    parts.append(_causal_conv(seg, c0_ref[sidx], cw, cb))
                    cn_out_ref[sidx] = seg[seg_len - n_prev:seg_len]
                xc = _cat_rows(parts)
            xcb = xc.astype(BF16)
            raw_ref[1, rows(c), :] = xc
            raw_ref[2, rows(c), :] = _dot(xcb, wa_ref[...])
            raw_ref[3, rows(c), :] = _dot(xcb, wi_ref[...])

        def recur(c):
            xc = raw_ref[1, rows(c), :]
            r = jax.nn.sigmoid(raw_ref[2, rows(c), :] + ba_ref[...])
            ig = jax.nn.sigmoid(raw_ref[3, rows(c), :] + bi_ref[...])
            log_a = (LRU_C * r) * log_lam
            a = jnp.exp(log_a)
            one_minus_a2 = -jnp.tanh(log_a) * (a * a + 1.0)
            gated = jnp.sqrt(one_minus_a2) * (ig * xc)
            if whole:
                cum_a, cum_b = _scan_rows(a, gated)
                hs = cum_b + cum_a * state["h"]
                state["h"] = hs[row_chunk - 1:row_chunk]
            else:
                parts = []
                for s in range(segs_per_chunk):
                    sidx = c * segs_per_chunk + s
                    sl = slice(s * seg_len, (s + 1) * seg_len)
                    cum_a, cum_b = _scan_rows(a[sl], gated[sl])
                    seg_hs = cum_b + cum_a * h0_ref[sidx]
                    hn_out_ref[sidx] = seg_hs[seg_len - 1:seg_len]
                    parts.append(seg_hs)
                hs = _cat_rows(parts)
            mix_ref[rows(c), :] = (_gelu(raw_ref[0, rows(c), :]) * hs).astype(BF16)

        for t in range(n_chunks + 2):
            if t < n_chunks:
                in_proj(t)
            if 0 <= t - 1 < n_chunks:
                conv_gates(t - 1)
            if 0 <= t - 2 < n_chunks:
                recur(t - 2)
        if whole:
            cn_out_ref[0] = state["conv"]
            hn_out_ref[0] = state["h"]
            if carried:
                cc_ref[j] = state["conv"]
                hc_ref[j] = state["h"]

    @pl.when(j >= LRU_HEADS)
    def _():
        chunk = max(row_chunk, min(hn_ref.shape[0], MEM_ATTN_ROWS))
        n_chunks = hn_ref.shape[0] // chunk
        segs = (chunk // seg_len, seg_len) if n_seg > 1 else (1, chunk)

        def rows(c):
            return slice(c * chunk, (c + 1) * chunk)

        def project(c):
            raw_ref[0, rows(c), :] = _dot(hn_ref[rows(c), :], wx_ref[...])

        def attend(c):
            seg0 = c * segs[0] if n_seg > 1 else 0
            mo_ref[rows(c), :] = _mem_attn(raw_ref[0, rows(c), :], gq_ref[...], mk_ref, mv_ref,
                                           seg0, *segs).astype(BF16)

        _two_stage(n_chunks, project, attend)


def _lru_in(x2d, tl, li, mi, p, h0, c0, mk, mv, row_chunk):
    m, d = x2d.shape
    hw = d // LRU_HEADS
    n_steps = LRU_HEADS + MEM_HEADS
    ns, tps = tl.n_seg, tl.tiles_per_seq
    width = p["lru_conv_w"].shape[1]
    last = LRU_HEADS - 1

    def hcol(j):
        return jnp.minimum(j, last)

    def mcol(j):
        return jnp.maximum(j - LRU_HEADS, 0)

    vec = pl.BlockSpec((None, 1, hw), lambda i, j: (mi, 0, hcol(j)))
    gate_w = pl.BlockSpec((None, None, hw, hw), lambda i, j: (mi, hcol(j), 0, 0))
    mem = pl.BlockSpec((None, ns, mk.shape[2], hw), lambda i, j: (li, i // tps, 0, mcol(j)))
    assert tl.bm % row_chunk == 0 and (row_chunk % tl.seg_len == 0 if ns > 1 else True)
    kern = functools.partial(_lru_in_kernel, n_seg=ns, seg_len=tl.seg_len, tiles_per_seq=tps,
                             row_chunk=row_chunk)
    return pl.pallas_call(
        kern,
        grid=(tl.n_tiles, n_steps),
        in_specs=[
            _resident((tl.bm, d), lambda i, j: (i, 0)),
            pl.BlockSpec((None, 1, d), lambda i, j: (li, 0, 0)),
            pl.BlockSpec((None, d, hw), lambda i, j: (mi, 0, hcol(j))),
            pl.BlockSpec((None, d, hw), lambda i, j: (mi, 0, LRU_HEADS + j)),
            pl.BlockSpec((None, width, hw), lambda i, j: (mi, 0, hcol(j))),
            vec, gate_w, gate_w, vec, vec, vec,
            pl.BlockSpec((None, ns, 1, hw), lambda i, j: (mi, i // tps, 0, hcol(j))),
            pl.BlockSpec((None, ns, width - 1, hw), lambda i, j: (mi, i // tps, 0, hcol(j))),
            pl.BlockSpec((None, 1, hw), lambda i, j: (li, 0, 0)),
            mem, mem,
        ],
        out_specs=[
            pl.BlockSpec((tl.bm, hw), lambda i, j: (i, hcol(j))),
            pl.BlockSpec((tl.bm, hw), lambda i, j: (i, mcol(j))),
            pl.BlockSpec((None, ns, 1, hw), lambda i, j: (i, 0, 0, hcol(j))),
            pl.BlockSpec((None, ns, width - 1, hw), lambda i, j: (i, 0, 0, hcol(j))),
        ],
        out_shape=[
            jax.ShapeDtypeStruct((m, d), BF16),
            jax.ShapeDtypeStruct((m, MEM_HEADS * hw), BF16),
            jax.ShapeDtypeStruct((tl.n_tiles, ns, 1, d), F32),
            jax.ShapeDtypeStruct((tl.n_tiles, ns, width - 1, d), F32),
        ],
        scratch_shapes=[
            pltpu.VMEM((tl.bm, d), BF16),
            pltpu.VMEM((LRU_HEADS, 1, hw), F32),
            pltpu.VMEM((LRU_HEADS, width - 1, hw), F32),
            pltpu.VMEM((4, tl.bm, hw), F32),
        ],
        compiler_params=_cparams(2),
        name="lru_in",
    )(x2d, p["norm_mix"], p["lru_w_in"], p["lru_w_in"], p["lru_conv_w"], p["lru_conv_b"],
      p["lru_gate_a_w"], p["lru_gate_x_w"], p["lru_gate_a_b"], p["lru_gate_x_b"], p["lru_lambda"],
      h0, c0, p["mem_q_norm"], mk, mv)


def _att_in_kernel(x_ref, gn_ref, w_ref, qg_ref, kg_ref, gq_ref, mk_ref, mv_ref,
                   qb_ref, kf_ref, kb_ref, vf_ref, vb_ref, mo_ref, hn_ref, z_ref,
                   *, n_seg, seg_len, nq, row_chunk):
    j = pl.program_id(1)

    @pl.when(j == 0)
    def _():
        hn_ref[...] = _rms(x_ref[...], gn_ref[...]).astype(BF16)

    dk = qg_ref.shape[-1]
    n_groups = z_ref.shape[-1] // dk
    n_chunks = hn_ref.shape[0] // row_chunk
    segs = (row_chunk // seg_len, seg_len) if n_seg > 1 else (1, row_chunk)

    def rows(c):
        return slice(c * row_chunk, (c + 1) * row_chunk)

    def project(c):
        z_ref[rows(c), :] = _dot(hn_ref[rows(c), :], w_ref[...])

    def queries(c):
        qscale = dk ** -0.5 * LOG2E
        for g in range(n_groups):
            sl = slice(g * dk, (g + 1) * dk)
            qb_ref[rows(c), sl] = (_rms(z_ref[rows(c), sl], qg_ref[...]) * qscale).astype(BF16)

    def keys(c):
        for g in range(n_groups):
            sl = slice(g * dk, (g + 1) * dk)
            kn = _rms(z_ref[rows(c), sl], kg_ref[...])
            kf_ref[rows(c), sl] = kn
            kb_ref[rows(c), sl] = kn.astype(BF16)

    def values(c):
        z = z_ref[rows(c), :]
        vf_ref[rows(c), :] = z
        vb_ref[rows(c), :] = z.astype(BF16)

    mem_chunk = max(row_chunk, min(hn_ref.shape[0], MEM_ATTN_ROWS))
    mem_segs = (mem_chunk // seg_len, seg_len) if n_seg > 1 else (1, mem_chunk)

    def mem_rows(c):
        return slice(c * mem_chunk, (c + 1) * mem_chunk)

    def mem_project(c):
        z_ref[mem_rows(c), :] = _dot(hn_ref[mem_rows(c), :], w_ref[...])

    def memory(c):
        seg0 = c * mem_segs[0] if n_seg > 1 else 0
        mo_ref[mem_rows(c), :] = _mem_attn(z_ref[mem_rows(c), :], gq_ref[...], mk_ref, mv_ref,
                                           seg0, *mem_segs).astype(BF16)

    @pl.when(j < nq)
    def _():
        _two_stage(n_chunks, project, queries)

    @pl.when((j >= nq) & (j < 2 * nq))
    def _():
        _two_stage(n_chunks, project, keys)

    @pl.when((j >= 2 * nq) & (j < 3 * nq))
    def _():
        _two_stage(n_chunks, project, values)

    @pl.when(j >= 3 * nq)
    def _():
        _two_stage(hn_ref.shape[0] // mem_chunk, mem_project, memory)


def _att_in(x2d, tl, li, ai, p, mk, mv, bc, row_chunk):
    m, d = x2d.shape
    mem_w = mk.shape[-1]
    nq, nm = d // bc, mem_w // bc
    n_steps = 3 * nq + nm
    ns, tps = tl.n_seg, tl.tiles_per_seq
    dk = p["attn_q_norm"].shape[-1]
    hd = p["mem_q_norm"].shape[-1]

    def col(lo, n):
        return lambda i, j: (i, jnp.clip(j - lo, 0, n - 1))

    mem = pl.BlockSpec((None, ns, mk.shape[2], bc),
                       lambda i, j: (li, i // tps, 0, jnp.clip(j - 3 * nq, 0, nm - 1)))
    assert tl.bm % row_chunk == 0 and (row_chunk % tl.seg_len == 0 if ns > 1 else True)
    kern = functools.partial(_att_in_kernel, n_seg=ns, seg_len=tl.seg_len, nq=nq, row_chunk=row_chunk)
    return pl.pallas_call(
        kern,
        grid=(tl.n_tiles, n_steps),
        in_specs=[
            _resident((tl.bm, d), lambda i, j: (i, 0)),
            pl.BlockSpec((None, 1, d), lambda i, j: (li, 0, 0)),
            pl.BlockSpec((None, d, bc), lambda i, j: (ai, 0, j)),
            pl.BlockSpec((None, 1, dk), lambda i, j: (ai, 0, 0)),
            pl.BlockSpec((None, 1, dk), lambda i, j: (ai, 0, 0)),
            pl.BlockSpec((None, 1, hd), lambda i, j: (li, 0, 0)),
            mem, mem,
        ],
        out_specs=[
            pl.BlockSpec((tl.bm, bc), col(0, nq)),
            pl.BlockSpec((tl.bm, bc), col(nq, nq)),
            pl.BlockSpec((tl.bm, bc), col(nq, nq)),
            pl.BlockSpec((tl.bm, bc), col(2 * nq, nq)),
            pl.BlockSpec((tl.bm, bc), col(2 * nq, nq)),
            pl.BlockSpec((tl.bm, bc), col(3 * nq, nm)),
        ],
        out_shape=[
            jax.ShapeDtypeStruct((m, d), BF16),
            jax.ShapeDtypeStruct((m, d), F32),
            jax.ShapeDtypeStruct((m, d), BF16),
            jax.ShapeDtypeStruct((m, d), F32),
            jax.ShapeDtypeStruct((m, d), BF16),
            jax.ShapeDtypeStruct((m, mem_w), BF16),
        ],
        scratch_shapes=[pltpu.VMEM((tl.bm, d), BF16), pltpu.VMEM((tl.bm, bc), F32)],
        compiler_params=_cparams(2),
        name="att_in",
    )(x2d, p["norm_mix"], p["attn_w_in"], p["attn_q_norm"], p["attn_k_norm"], p["mem_q_norm"], mk, mv)


def _attn_block(q, k, v, m_ref, l_ref, acc_ref, mask):
    dk = q.shape[-1] // 2
    tk = k.shape[0]
    scores = [_dot_nt(k[:, c * dk:(c + 1) * dk], q[:, c * dk:(c + 1) * dk]) for c in range(2)]
    for c in range(2):
        s = scores[c]
        if mask is not None:
            s = jnp.where(mask, s, NEG_BIG)
        m_old = m_ref[c]
        m_new = jnp.maximum(m_old, jnp.max(s, axis=0, keepdims=True))
        alpha = jnp.exp2(m_old - m_new)
        p = jnp.exp2(s - m_new)
        l_ref[c] = alpha * l_ref[c] + jnp.sum(p.reshape(tk // SUBLANES, SUBLANES, p.shape[-1]), axis=0)
        pv = lax.dot_general(v, p.astype(BF16), (((0,), (0,)), ((), ())), preferred_element_type=F32)
        acc_ref[c] = alpha * acc_ref[c] + pv
        m_ref[c] = m_new


def _attn_init(m_ref, l_ref, acc_ref):
    m_ref[...] = jnp.full(m_ref.shape, NEG_BIG, F32)
    l_ref[...] = jnp.zeros(l_ref.shape, F32)
    acc_ref[...] = jnp.zeros(acc_ref.shape, F32)


def _attn_finish(lp_ref, sg_ref, l_ref, acc_ref, lam_init):
    lp = lp_ref[...]
    lam = (jnp.exp(jnp.sum(lp[0:1] * lp[1:2], axis=-1, keepdims=True))
           - jnp.exp(jnp.sum(lp[2:3] * lp[3:4], axis=-1, keepdims=True)) + lam_init)
    l0 = jnp.sum(l_ref[0], axis=0, keepdims=True)
    l1 = jnp.sum(l_ref[1], axis=0, keepdims=True)
    o = (acc_ref[0] / l0 - lam * (acc_ref[1] / l1)).T
    return (_rms(o, sg_ref[...]) * (1.0 - lam_init)).astype(BF16)


def _attn_prompt_kernel(lp_ref, sg_ref, q_ref, k_ref, v_ref, o_ref, m_ref, l_ref, acc_ref, *, tq, tk, lam_init):
    qi = pl.program_id(2)
    _attn_init(m_ref, l_ref, acc_ref)
    q = q_ref[...]
    per = tk // tq

    def visible(start, size):
        start = pl.multiple_of(start, size)
        _attn_block(q, k_ref[pl.ds(start, size), :], v_ref[pl.ds(start, size), :], m_ref, l_ref, acc_ref, None)

    def body(kv, carry):
        visible(kv * tk, tk)
        return carry

    n_big = qi // per
    lax.fori_loop(0, n_big, body, 0)
    for r in range(1, per):
        @pl.when(qi % per >= r)
        def _():
            visible(n_big * tk + (r - 1) * tq, tq)

    start = pl.multiple_of(qi * tq, tq)
    shift = CHUNK.bit_length() - 1
    keyc = lax.shift_right_logical(lax.broadcasted_iota(jnp.int32, (tq, tq), 0), shift)
    qryc = lax.shift_right_logical(lax.broadcasted_iota(jnp.int32, (tq, tq), 1), shift)
    _attn_block(q, k_ref[pl.ds(start, tq), :], v_ref[pl.ds(start, tq), :], m_ref, l_ref, acc_ref, keyc <= qryc)
    o_ref[...] = _attn_finish(lp_ref, sg_ref, l_ref, acc_ref, lam_init)


def _attn_prompt(qb, kb, vb, ai, p, n_seq, t, tq, tk, lam_init):
    d = qb.shape[-1]
    hw = d // DIFF_HEADS
    assert t % tq == 0 and tq % CHUNK == 0 and tk % tq == 0
    q3, k3, v3 = (a.reshape(n_seq, t, d) for a in (qb, kb, vb))
    lp, sg = p["attn_lambda"], p["attn_subln"]
    kern = functools.partial(_attn_prompt_kernel, tq=tq, tk=tk, lam_init=lam_init)
    out = pl.pallas_call(
        kern,
        grid=(n_seq, DIFF_HEADS, t // tq),
        in_specs=[
            pl.BlockSpec((None,) + lp.shape[1:], lambda b, h, qi: (ai, 0, 0)),
            pl.BlockSpec((None, 1, hw), lambda b, h, qi: (ai, 0, 0)),
            pl.BlockSpec((None, tq, hw), lambda b, h, qi: (b, qi, h)),
            pl.BlockSpec((None, t, hw), lambda b, h, qi: (b, 0, h)),
            pl.BlockSpec((None, t, hw), lambda b, h, qi: (b, 0, h)),
        ],
        out_specs=pl.BlockSpec((None, tq, hw), lambda b, h, qi: (b, qi, h)),
        out_shape=jax.ShapeDtypeStruct((n_seq, t, d), BF16),
        scratch_shapes=[
            pltpu.VMEM((2, 1, tq), F32),
            pltpu.VMEM((2, SUBLANES, tq), F32),
            pltpu.VMEM((2, hw, tq), F32),
        ],
        compiler_params=_cparams(3),
        name="attn_prompt",
    )(lp, sg, q3, k3, v3)
    return out.reshape(n_seq * t, d)


def _attn_cached_kernel(lp_ref, sg_ref, q_ref, kp_ref, vp_ref, kn_ref, vn_ref, o_ref,
                        m_ref, l_ref, acc_ref, *, past, lam_init):
    n_heads = DIFF_HEADS
    rows_per_t = 2 * n_heads
    hw = q_ref.shape[-1] // n_heads
    for h in range(n_heads):
        cols = slice(h * hw, (h + 1) * hw)
        k = jnp.concatenate([kp_ref[pl.ds(2 * h + c, past, stride=rows_per_t), :] for c in range(2)], axis=-1)
        v = jnp.concatenate([vp_ref[pl.ds(c * n_heads + h, past, stride=rows_per_t), :] for c in range(2)],
                            axis=-1)
        q = q_ref[:, cols]
        _attn_init(m_ref, l_ref, acc_ref)
        _attn_block(q, k.astype(BF16), v.astype(BF16), m_ref, l_ref, acc_ref, None)
        _attn_block(q, kn_ref[:, cols], vn_ref[:, cols], m_ref, l_ref, acc_ref, None)
        o_ref[:, cols] = _attn_finish(lp_ref, sg_ref, l_ref, acc_ref, lam_init)


def _attn_cached(qb, kb, vb, k_past, v_past, ai, p, n_seq, t, lam_init):
    d = qb.shape[-1]
    hw = d // DIFF_HEADS
    n_attn, _, past, n_heads, _, dk = k_past.shape
    assert past % CHUNK == 0 and t <= CHUNK and n_heads == DIFF_HEADS and 2 * dk == hw
    q3, k3, v3 = (a.reshape(n_seq, t, d) for a in (qb, kb, vb))
    lp, sg = p["attn_lambda"], p["attn_subln"]
    k_rows = k_past.reshape(n_attn * n_seq, past * 2 * n_heads, dk)
    v_rows = v_past.reshape(n_attn, n_seq, past, n_heads, 2, dk).transpose(0, 1, 2, 4, 3, 5)
    v_rows = v_rows.reshape(n_attn * n_seq, past * 2 * n_heads, dk)
    kern = functools.partial(_attn_cached_kernel, past=past, lam_init=lam_init)
    new_spec = pl.BlockSpec((None, t, d), lambda b: (b, 0, 0))
    past_spec = pl.BlockSpec((None, past * 2 * n_heads, dk), lambda b: (ai * n_seq + b, 0, 0))
    out = pl.pallas_call(
        kern,
        grid=(n_seq,),
        in_specs=[
            pl.BlockSpec((None,) + lp.shape[1:], lambda b: (ai, 0, 0)),
            pl.BlockSpec((None, 1, hw), lambda b: (ai, 0, 0)),
            new_spec, past_spec, past_spec, new_spec, new_spec,
        ],
        out_specs=new_spec,
        out_shape=jax.ShapeDtypeStruct((n_seq, t, d), BF16),
        scratch_shapes=[
            pltpu.VMEM((2, 1, t), F32),
            pltpu.VMEM((2, SUBLANES, t), F32),
            pltpu.VMEM((2, hw, t), F32),
        ],
        compiler_params=_cparams(1),
        name="attn_cached",
    )(lp, sg, q3, k_rows, v_rows, k3, v3)
    return out.reshape(n_seq * t, d)


def _out_kernel(x_ref, mix_ref, mo_ref, w1_ref, w2_ref, o_ref):
    o_ref[...] = x_ref[...] + (_dot(mix_ref[...], w1_ref[...]) + _dot(mo_ref[...], w2_ref[...]))


def _out_proj(x2d, mix, mo, w_out, wi, bm, bn):
    m, d = x2d.shape
    k1, k2 = mix.shape[1], mo.shape[1]
    assert k1 % k2 == 0
    return pl.pallas_call(
        _out_kernel,
        grid=(m // bm, d // bn),
        in_specs=[
            pl.BlockSpec((bm, bn), lambda i, j: (i, j)),
            pl.BlockSpec((bm, k1), lambda i, j: (i, 0)),
            pl.BlockSpec((bm, k2), lambda i, j: (i, 0)),
            pl.BlockSpec((None, k1, bn), lambda i, j: (wi, 0, j)),
            pl.BlockSpec((None, k2, bn), lambda i, j: (wi, k1 // k2, j)),
        ],
        out_specs=pl.BlockSpec((bm, bn), lambda i, j: (i, j)),
        out_shape=jax.ShapeDtypeStruct((m, d), F32),
        compiler_params=_cparams(2),
        name="out_proj",
    )(x2d, mix, mo, w_out, w_out)


def _ffn_kernel(x_ref, gn_ref, wg_ref, wv_ref, cwg_ref, cwv_ref, cbg_ref, cbv_ref, stg_ref, stv_ref, wdn_ref,
                o_ref, nstg_ref, nstv_ref, hn_ref, cg_ref, cv_ref, raw_ref, act_ref,
                *, n_seg, seg_len, tiles_per_seq, row_chunk):
    i = pl.program_id(0)
    j = pl.program_id(1)
    carried = tiles_per_seq > 1

    @pl.when(j == 0)
    def _():
        x = x_ref[...]
        hn_ref[...] = _rms(x, gn_ref[...]).astype(BF16)
        o_ref[...] = x

    if carried:
        @pl.when(i % tiles_per_seq == 0)
        def _():
            cg_ref[j] = stg_ref[0]
            cv_ref[j] = stv_ref[0]

    halves = ((wg_ref, cwg_ref, cbg_ref, stg_ref, cg_ref, nstg_ref),
              (wv_ref, cwv_ref, cbv_ref, stv_ref, cv_ref, nstv_ref))
    n_prev = cwg_ref.shape[0] - 1
    n_chunks = hn_ref.shape[0] // row_chunk
    segs_per_chunk = row_chunk // seg_len if n_seg > 1 else 0
    prevs = [(h[4][j] if carried else h[3][0]) if n_seg == 1 else None for h in halves]

    def rows(c):
        return slice(c * row_chunk, (c + 1) * row_chunk)

    def up_proj(c):
        hn = hn_ref[rows(c), :]
        for hi, h in enumerate(halves):
            raw_ref[hi, rows(c), :] = _dot(hn, h[0][...])

    def conv_gate(c):
        ys = []
        for hi, (_, cw_ref, cb_ref, st_ref, _, nst_ref) in enumerate(halves):
            up = raw_ref[hi, rows(c), :]
            cw = cw_ref[...]
            cb = cb_ref[...]
            if n_seg == 1:
                ys.append(_causal_conv(up, prevs[hi], cw, cb))
                prevs[hi] = up[row_chunk - n_prev:row_chunk]
            else:
                parts = []
                for s in range(segs_per_chunk):
                    seg = up[s * seg_len:(s + 1) * seg_len]
                    parts.append(_causal_conv(seg, st_ref[c * segs_per_chunk + s], cw, cb))
                    nst_ref[c * segs_per_chunk + s] = seg[seg_len - n_prev:seg_len]
                ys.append(_cat_rows(parts))
        act_ref[rows(c), :] = (_gelu(ys[0]) * ys[1]).astype(BF16)

    def down_proj(c):
        o_ref[rows(c), :] += _dot(act_ref[rows(c), :], wdn_ref[...])

    for t in range(n_chunks + 2):
        if t < n_chunks:
            up_proj(t)
        if 0 <= t - 1 < n_chunks:
            conv_gate(t - 1)
        if 0 <= t - 2 < n_chunks:
            down_proj(t - 2)
    if n_seg == 1:
        for hi, (_, _, _, _, carry_ref, nst_ref) in enumerate(halves):
            nst_ref[0] = prevs[hi]
            if carried:
                carry_ref[j] = prevs[hi]


def _ffn(x2d, tl, li, p, state, bf, row_chunk):
    m, d = x2d.shape
    dff = p["ffn_w_down"].shape[1]
    nj = dff // bf
    ns, tps = tl.n_seg, tl.tiles_per_seq
    width = p["ffn_conv_w"].shape[1]
    assert tl.bm % row_chunk == 0 and (row_chunk % tl.seg_len == 0 if ns > 1 else True)
    kern = functools.partial(_ffn_kernel, n_seg=ns, seg_len=tl.seg_len, tiles_per_seq=tps, row_chunk=row_chunk)

    def halves(block, index_map):
        gate = pl.BlockSpec(block, lambda i, j: index_map(i, j, j))
        value = pl.BlockSpec(block, lambda i, j: index_map(i, j, nj + j))
        return [gate, value]

    nst_spec = pl.BlockSpec((None, ns, width - 1, bf), lambda i, j: (i, 0, 0, j))
    nst_shape = jax.ShapeDtypeStruct((tl.n_tiles, ns, width - 1, dff), F32)
    return pl.pallas_call(
        kern,
        grid=(tl.n_tiles, nj),
        in_specs=[
            _resident((tl.bm, d), lambda i, j: (i, 0)),
            pl.BlockSpec((None, 1, d), lambda i, j: (li, 0, 0)),
            *halves((None, d, bf), lambda i, j, c: (li, 0, c)),
            *halves((None, width, bf), lambda i, j, c: (li, 0, c)),
            *halves((None, 1, bf), lambda i, j, c: (li, 0, c)),
            *halves((None, ns, width - 1, bf), lambda i, j, c: (li, i // tps, 0, c)),
            pl.BlockSpec((None, bf, d), lambda i, j: (li, j, 0)),
        ],
        out_specs=[pl.BlockSpec((tl.bm, d), lambda i, j: (i, 0)), nst_spec, nst_spec],
        out_shape=[jax.ShapeDtypeStruct((m, d), F32), nst_shape, nst_shape],
        scratch_shapes=[
            pltpu.VMEM((tl.bm, d), BF16),
            pltpu.VMEM((nj, width - 1, bf), F32),
            pltpu.VMEM((nj, width - 1, bf), F32),
            pltpu.VMEM((2, tl.bm, bf), F32),
            pltpu.VMEM((tl.bm, bf), BF16),
        ],
        compiler_params=_cparams(2),
        name="ffn",
    )(x2d, p["norm_ffn"], p["ffn_w_up"], p["ffn_w_up"], p["ffn_conv_w"], p["ffn_conv_w"],
      p["ffn_conv_b"], p["ffn_conv_b"], state, state, p["ffn_w_down"])


def _run_trunk(x, mem_k, mem_v, lru_h, lru_conv, attn_past, ffn_conv, p, cfg):
    n_seq, t, d = x.shape
    depth = p["norm_mix"].shape[0]
    x2d = x.reshape(n_seq * t, d)
    tl = _Tiling(n_seq, t, cfg["bm"])
    tl_ffn = _Tiling(n_seq, t, cfg["bm_ffn"])
    mem_tokens = mem_k.shape[2]
    mk = mem_k.reshape(depth, n_seq, mem_tokens, -1)
    mv = mem_v.reshape(depth, n_seq, mem_tokens, -1)
    h0 = lru_h[:, :, None, :]
    new_h, new_lconv, new_k, new_v, new_fconv = [], [], [], [], []
    for i in range(depth):
        j = i // 2
        if i % 2 == 0:
            mix, mo, h_last, buf = _lru_in(x2d, tl, i, j, p, h0, lru_conv, mk, mv, cfg["lru_rows"])
            new_h.append(_last_tile_state(h_last, tl)[:, 0, :])
            new_lconv.append(_last_tile_state(buf, tl))
            w_out = p["lru_w_out"]
        else:
            qb, kf, kb, vf, vb, mo = _att_in(x2d, tl, i, j, p, mk, mv, cfg["bc"], cfg["att_rows"])
            new_k.append(kf)
            new_v.append(vf)
            lam_init = 0.8 - 0.6 * math.exp(-0.3 * i)
            if attn_past is None:
                mix = _attn_prompt(qb, kb, vb, j, p, n_seq, t, cfg["tq"], cfg["tk"], lam_init)
            else:
                mix = _attn_cached(qb, kb, vb, attn_past[0], attn_past[1], j, p, n_seq, t, lam_init)
            w_out = p["attn_w_out"]
        x2d = _out_proj(x2d, mix, mo, w_out, j, cfg["bm_out"], cfg["bn_out"])
        x2d, fg, fv = _ffn(x2d, tl_ffn, i, p, ffn_conv, cfg["bf"], cfg["ffn_rows"])
        new_fconv.append(jnp.concatenate([_last_tile_state(fg, tl_ffn), _last_tile_state(fv, tl_ffn)], axis=-1))
    return x2d.reshape(n_seq, t, d), new_h, new_lconv, new_k, new_v, new_fconv


def _forward(x_prompt, x_sample, mem_prompt, cache_attn_k, cache_attn_v, cache_mem_k, cache_mem_v,
             state_lru_h, state_lru_conv, state_ffn_conv, p, mem_norm, mem_w_kv, mem_k_norm, cfg_p, cfg_s):
    b, t, d = x_prompt.shape
    depth = p["norm_mix"].shape[0]
    n_lru, n_attn = (depth + 1) // 2, depth // 2

    p = dict(p)
    for name in ("lru_w_in", "lru_gate_a_w", "lru_gate_x_w", "lru_w_out", "attn_w_in", "attn_w_out",
                 "ffn_w_up", "ffn_w_down"):
        p[name] = p[name].astype(BF16)
    for name in ("norm_mix", "norm_ffn", "lru_conv_b", "lru_gate_a_b", "lru_gate_x_b", "lru_lambda",
                 "attn_q_norm", "attn_k_norm", "attn_subln", "mem_q_norm", "ffn_conv_b"):
        p[name] = p[name][:, None, :]

    mem_tokens = mem_prompt.shape[1]
    mk, mv = _mem_kv(mem_prompt.reshape(b * mem_tokens, d), mem_norm[:, None, :], mem_w_kv.astype(BF16),
                     mem_k_norm[:, None, :])
    hd = mem_k_norm.shape[-1]
    p_mem_k = mk.reshape(depth, b, mem_tokens, MEM_HEADS, hd)
    p_mem_v = mv.reshape(depth, b, mem_tokens, MEM_HEADS, hd)

    zeros_h = jnp.zeros((n_lru, b, d), F32)
    zeros_lconv = jnp.zeros((n_lru, b) + state_lru_conv.shape[2:], F32)
    zeros_fconv = jnp.zeros((depth, b) + state_ffn_conv.shape[2:], F32)
    yp, ph, plc, pk, pv, pfc = _run_trunk(x_prompt, p_mem_k, p_mem_v, zeros_h, zeros_lconv, None,
                                          zeros_fconv, p, cfg_p)
    ys, sh, slc, sk, sv, sfc = _run_trunk(x_sample, cache_mem_k, cache_mem_v, state_lru_h, state_lru_conv,
                                          (cache_attn_k, cache_attn_v), state_ffn_conv, p, cfg_s)

    dk = p["attn_q_norm"].shape[-1]
    db, dt = x_sample.shape[0], x_sample.shape[1]

    def kshape(a, n, tt):
        return jnp.stack(a).reshape(n_attn, n, tt, DIFF_HEADS, 2, dk)

    def vshape(a, n, tt):
        return jnp.stack(a).reshape(n_attn, n, tt, DIFF_HEADS, 2 * dk)

    return (yp, ys, jnp.stack(ph), jnp.stack(plc), kshape(pk, b, t), vshape(pv, b, t), p_mem_k, p_mem_v,
            jnp.stack(pfc), jnp.stack(sh), jnp.stack(slc), kshape(sk, db, dt), vshape(sv, db, dt),
            jnp.stack(sfc))


CFG_PROMPT = dict(bm=1024, bc=512, lru_rows=128, att_rows=256, bm_ffn=1024, bf=512, ffn_rows=512,
                  bm_out=1024, bn_out=1024, tq=512, tk=1024)
CFG_SAMPLE = dict(bm=512, bc=256, lru_rows=128, att_rows=256, bm_ffn=1024, bf=512, ffn_rows=512,
                  bm_out=1024, bn_out=1024)


def kernel(x_prompt, x_sample, mem_prompt, cache_attn_k, cache_attn_v, cache_mem_k, cache_mem_v, state_lru_h, state_lru_conv, state_ffn_conv, norm_mix, norm_ffn, lru_w_in, lru_conv_w, lru_conv_b, lru_gate_a_w, lru_gate_a_b, lru_gate_x_w, lru_gate_x_b, lru_lambda, lru_w_out, attn_w_in, attn_q_norm, attn_k_norm, attn_lambda, attn_subln, attn_w_out, mem_norm, mem_w_kv, mem_q_norm, mem_k_norm, ffn_w_up, ffn_conv_w, ffn_conv_b, ffn_w_down):
    p = {
        "norm_mix": norm_mix, "norm_ffn": norm_ffn,
        "lru_w_in": lru_w_in, "lru_conv_w": lru_conv_w, "lru_conv_b": lru_conv_b,
        "lru_gate_a_w": lru_gate_a_w, "lru_gate_a_b": lru_gate_a_b,
        "lru_gate_x_w": lru_gate_x_w, "lru_gate_x_b": lru_gate_x_b,
        "lru_lambda": lru_lambda, "lru_w_out": lru_w_out,
        "attn_w_in": attn_w_in, "attn_q_norm": attn_q_norm, "attn_k_norm": attn_k_norm,
        "attn_lambda": attn_lambda, "attn_subln": attn_subln, "attn_w_out": attn_w_out,
        "mem_q_norm": mem_q_norm,
        "ffn_w_up": ffn_w_up, "ffn_conv_w": ffn_conv_w, "ffn_conv_b": ffn_conv_b, "ffn_w_down": ffn_w_down,
    }
    return _forward(x_prompt, x_sample, mem_prompt, cache_attn_k, cache_attn_v, cache_mem_k, cache_mem_v,
                    state_lru_h, state_lru_conv, state_ffn_conv, p, mem_norm, mem_w_kv, mem_k_norm,
                    CFG_PROMPT, CFG_SAMPLE)
```

```python
import functools
import math

import jax
import jax.numpy as jnp
from jax import lax
from jax.experimental import pallas as pl
from jax.experimental.pallas import tpu as pltpu

F32 = jnp.float32
BF16 = jnp.bfloat16

CHUNK = 64
LRU_HEADS = 8
LRU_C = 8.0
DIFF_HEADS = 8
MEM_HEADS = 4
RMS_EPS = 1e-6
NEG_BIG = -1e30
LOG2E = 1.4426950408889634
SUBLANES = 8
MEM_ATTN_ROWS = 512

V7X_VMEM_LIMIT_BYTES = 56 * 1024 * 1024


def _cparams(n_grid_axes):
    return pltpu.CompilerParams(
        dimension_semantics=("arbitrary",) * n_grid_axes,
        vmem_limit_bytes=V7X_VMEM_LIMIT_BYTES,
    )


def _dot(a, b):
    return jnp.dot(a, b, preferred_element_type=F32)


def _dot_nt(a, b):
    return lax.dot_general(a, b, (((1,), (1,)), ((), ())), preferred_element_type=F32)


def _rms(x, g):
    return x * lax.rsqrt(jnp.mean(x * x, axis=-1, keepdims=True) + RMS_EPS) * g


def _gelu(x):
    return x * (0.5 * (1.0 + jnp.tanh(0.7978845608028654 * (x + 0.044715 * (x * x * x)))))


def _cat_rows(parts):
    return parts[0] if len(parts) == 1 else jnp.concatenate(parts, axis=0)


def _delayed(seg, prev, d):
    n_prev = prev.shape[0]
    v = pltpu.roll(seg, d, 0)
    row = lax.broadcasted_iota(jnp.int32, (SUBLANES, 1), 0)
    top = v[0:SUBLANES]
    for r in range(d):
        top = jnp.where(row == r, prev[n_prev - d + r:n_prev - d + r + 1], top)
    return jnp.concatenate([top, v[SUBLANES:]], axis=0)


def _causal_conv(seg, prev, cw, cb):
    n_prev = cw.shape[0] - 1
    y = cb + _delayed(seg, prev, n_prev) * cw[0:1]
    for t in range(1, n_prev):
        y = y + _delayed(seg, prev, n_prev - t) * cw[t:t + 1]
    return y + seg * cw[n_prev:n_prev + 1]


def _scan_rows(a, b):
    n = a.shape[0]
    row = lax.broadcasted_iota(jnp.int32, (n, 1), 0)
    sh = 1
    while sh < min(SUBLANES, n):
        valid = row >= sh
        a_sh = pltpu.roll(a, sh, 0)
        b_sh = pltpu.roll(b, sh, 0)
        b = jnp.where(valid, b + a * b_sh, b)
        a = jnp.where(valid, a * a_sh, a)
        sh *= 2
    while sh < n:
        b = jnp.concatenate([b[:sh], b[sh:] + a[sh:] * b[:n - sh]], axis=0)
        a = jnp.concatenate([a[:sh], a[sh:] * a[:n - sh]], axis=0)
        sh *= 2
    return a, b


def _two_stage(n_chunks, first, second):
    for t in range(n_chunks + 1):
        if t < n_chunks:
            first(t)
        if t >= 1:
            second(t - 1)


def _mem_attn(q, gq, mk_ref, mv_ref, seg0, n_seg, seg_len):
    hd = gq.shape[-1]
    outs = []
    for h in range(q.shape[-1] // hd):
        cols = slice(h * hd, (h + 1) * hd)
        qn = _rms(q[:, cols], gq).astype(BF16)
        segs = []
        for s in range(n_seg):
            k = mk_ref[seg0 + s, :, cols].astype(BF16)
            v = mv_ref[seg0 + s, :, cols].astype(BF16)
            sc = _dot_nt(qn[s * seg_len:(s + 1) * seg_len], k) * (hd ** -0.5)
            p = jnp.exp(sc - jnp.max(sc, axis=-1, keepdims=True))
            pr = p / jnp.sum(p, axis=-1, keepdims=True)
            segs.append(_dot(pr.astype(BF16), v))
        outs.append(_cat_rows(segs))
    return outs[0] if len(outs) == 1 else jnp.concatenate(outs, axis=-1)


class _Tiling:
    def __init__(self, n_seq, t, bm):
        self.n_seq, self.t = n_seq, t
        if t >= bm:
            assert t % bm == 0
            self.n_seg, self.seg_len, self.tiles_per_seq = 1, bm, t // bm
        else:
            assert bm % t == 0 and n_seq % (bm // t) == 0
            self.n_seg, self.seg_len, self.tiles_per_seq = bm // t, t, 1
        self.bm = self.n_seg * self.seg_len
        self.n_tiles = n_seq * t // self.bm
        assert self.seg_len & (self.seg_len - 1) == 0 and self.seg_len >= SUBLANES


def _last_tile_state(per_tile, tl):
    rows, c = per_tile.shape[2:]
    return per_tile.reshape(tl.n_seq, tl.tiles_per_seq, rows, c)[:, -1]


def _resident(block_shape, index_map):
    return pl.BlockSpec(block_shape, index_map, pipeline_mode=pl.Buffered(1))


def _mem_kv_kernel(mem_ref, gn_ref, wk_ref, wv_ref, gk_ref, k_ref, v_ref, hn_ref):
    @pl.when(pl.program_id(1) == 0)
    def _():
        hn_ref[...] = _rms(mem_ref[...], gn_ref[...]).astype(BF16)

    hn = hn_ref[...]
    k_ref[...] = _rms(_dot(hn, wk_ref[...].astype(BF16)), gk_ref[...])
    v_ref[...] = _dot(hn, wv_ref[...].astype(BF16))


def _mem_kv(mem2d, mem_norm, w_kv, mem_k_norm):
    depth, d, two_w = w_kv.shape
    mem_w = two_w // 2
    hd = mem_w // MEM_HEADS
    m = mem2d.shape[0]
    return pl.pallas_call(
        _mem_kv_kernel,
        grid=(depth, MEM_HEADS),
        in_specs=[
            pl.BlockSpec((m, d), lambda l, j: (0, 0)),
            pl.BlockSpec((None, 1, d), lambda l, j: (l, 0, 0)),
            pl.BlockSpec((None, d, hd), lambda l, j: (l, 0, j)),
            pl.BlockSpec((None, d, hd), lambda l, j: (l, 0, MEM_HEADS + j)),
            pl.BlockSpec((None, 1, hd), lambda l, j: (l, 0, 0)),
        ],
        out_specs=[
            pl.BlockSpec((None, m, hd), lambda l, j: (l, 0, j)),
            pl.BlockSpec((None, m, hd), lambda l, j: (l, 0, j)),
        ],
        out_shape=[jax.ShapeDtypeStruct((depth, m, mem_w), F32)] * 2,
        scratch_shapes=[pltpu.VMEM((m, d), BF16)],
        compiler_params=_cparams(2),
        name="mem_kv",
    )(mem2d, mem_norm, w_kv, w_kv, mem_k_norm)


def _lru_in_kernel(x_ref, gn_ref, wg_ref, wx_ref, cw_ref, cb_ref, wa_ref, wi_ref, ba_ref, bi_ref,
                   lam_ref, h0_ref, c0_ref, gq_ref, mk_ref, mv_ref,
                   mix_ref, mo_ref, hn_out_ref, cn_out_ref,
                   hn_ref, hc_ref, cc_ref, raw_ref, *, n_seg, seg_len, tiles_per_seq, row_chunk):
    i = pl.program_id(0)
    j = pl.program_id(1)
    carried = tiles_per_seq > 1

    @pl.when(j == 0)
    def _():
        hn_ref[...] = _rms(x_ref[...], gn_ref[...]).astype(BF16)

    @pl.when(j < LRU_HEADS)
    def _():
        if carried:
            @pl.when(i % tiles_per_seq == 0)
            def _():
                cc_ref[j] = c0_ref[0]
                hc_ref[j] = h0_ref[0]

        cw = cw_ref[...]
        cb = cb_ref[...]
        wg, wx, wa, wi = (r[...].astype(BF16) for r in (wg_ref, wx_ref, wa_ref, wi_ref))
        n_prev = cw.shape[0] - 1
        n_chunks = hn_ref.shape[0] // row_chunk
        segs_per_chunk = row_chunk // seg_len if n_seg > 1 else 0
        whole = n_seg == 1
        state = {"conv": (cc_ref[j] if carried else c0_ref[0]) if whole else None,
                 "h": (hc_ref[j] if carried else h0_ref[0]) if whole else None}
        log_lam = jax.nn.log_sigmoid(lam_ref[...])

        def rows(c):
            return slice(c * row_chunk, (c + 1) * row_chunk)

        def in_proj(c):
            hn = hn_ref[rows(c), :]
            raw_ref[0, rows(c), :] = _dot(hn, wg)
            raw_ref[1, rows(c), :] = _dot(hn, wx)

        def conv_gates(c):
            xr = raw_ref[1, rows(c), :]
            if whole:
                xc = _causal_conv(xr, state["conv"], cw, cb)
                state["conv"] = xr[row_chunk - n_prev:row_chunk]
            else:
                parts = []
                for s in range(segs_per_chunk):
                    sidx = c * segs_per_chunk + s
                    seg = xr[s * seg_len:(s + 1) * seg_len]
                    parts.append(_causal_conv(seg, c0_ref[sidx], cw, cb))
                    cn_out_ref[sidx] = seg[seg_len - n_prev:seg_len]
                xc = _cat_rows(parts)
            xcb = xc.astype(BF16)
            raw_ref[1, rows(c), :] = xc
            raw_ref[2, rows(c), :] = _dot(xcb, wa)
            raw_ref[3, rows(c), :] = _dot(xcb, wi)

        def recur(c):
            xc = raw_ref[1, rows(c), :]
            r = jax.nn.sigmoid(raw_ref[2, rows(c), :] + ba_ref[...])
            ig = jax.nn.sigmoid(raw_ref[3, rows(c), :] + bi_ref[...])
            log_a = (LRU_C * r) * log_lam
            a = jnp.exp(log_a)
            one_minus_a2 = -jnp.tanh(log_a) * (a * a + 1.0)
            gated = jnp.sqrt(one_minus_a2) * (ig * xc)
            if whole:
                cum_a, cum_b = _scan_rows(a, gated)
                hs = cum_b + cum_a * state["h"]
                state["h"] = hs[row_chunk - 1:row_chunk]
            else:
                parts = []
                for s in range(segs_per_chunk):
                    sidx = c * segs_per_chunk + s
                    sl = slice(s * seg_len, (s + 1) * seg_len)
                    cum_a, cum_b = _scan_rows(a[sl], gated[sl])
                    seg_hs = cum_b + cum_a * h0_ref[sidx]
                    hn_out_ref[sidx] = seg_hs[seg_len - 1:seg_len]
                    parts.append(seg_hs)
                hs = _cat_rows(parts)
            mix_ref[rows(c), :] = (_gelu(raw_ref[0, rows(c), :]) * hs).astype(BF16)

        for t in range(n_chunks + 2):
            if t < n_chunks:
                in_proj(t)
            if 0 <= t - 1 < n_chunks:
                conv_gates(t - 1)
            if 0 <= t - 2 < n_chunks:
                recur(t - 2)
        if whole:
            cn_out_ref[0] = state["conv"]
            hn_out_ref[0] = state["h"]
            if carried:
                cc_ref[j] = state["conv"]
                hc_ref[j] = state["h"]

    @pl.when(j >= LRU_HEADS)
    def _():
        chunk = max(row_chunk, min(hn_ref.shape[0], MEM_ATTN_ROWS))
        n_chunks = hn_ref.shape[0] // chunk
        segs = (chunk // seg_len, seg_len) if n_seg > 1 else (1, chunk)

        wq = wx_ref[...].astype(BF16)

        def rows(c):
            return slice(c * chunk, (c + 1) * chunk)

        def project(c):
            raw_ref[0, rows(c), :] = _dot(hn_ref[rows(c), :], wq)

        def attend(c):
            seg0 = c * segs[0] if n_seg > 1 else 0
            mo_ref[rows(c), :] = _mem_attn(raw_ref[0, rows(c), :], gq_ref[...], mk_ref, mv_ref,
                                           seg0, *segs).astype(BF16)

        _two_stage(n_chunks, project, attend)


def _lru_in(x2d, tl, li, mi, p, h0, c0, mk, mv, row_chunk):
    m, d = x2d.shape
    hw = d // LRU_HEADS
    n_steps = LRU_HEADS + MEM_HEADS
    ns, tps = tl.n_seg, tl.tiles_per_seq
    width = p["lru_conv_w"].shape[1]
    last = LRU_HEADS - 1

    def hcol(j):
        return jnp.minimum(j, last)

    def mcol(j):
        return jnp.maximum(j - LRU_HEADS, 0)

    vec = pl.BlockSpec((None, 1, hw), lambda i, j: (mi, 0, hcol(j)))
    gate_w = pl.BlockSpec((None, None, hw, hw), lambda i, j: (mi, hcol(j), 0, 0))
    mem = pl.BlockSpec((None, ns, mk.shape[2], hw), lambda i, j: (li, i // tps, 0, mcol(j)))
    assert tl.bm % row_chunk == 0 and (row_chunk % tl.seg_len == 0 if ns > 1 else True)
    kern = functools.partial(_lru_in_kernel, n_seg=ns, seg_len=tl.seg_len, tiles_per_seq=tps,
                             row_chunk=row_chunk)
    return pl.pallas_call(
        kern,
        grid=(tl.n_tiles, n_steps),
        in_specs=[
            _resident((tl.bm, d), lambda i, j: (i, 0)),
            pl.BlockSpec((None, 1, d), lambda i, j: (li, 0, 0)),
            pl.BlockSpec((None, d, hw), lambda i, j: (mi, 0, hcol(j))),
            pl.BlockSpec((None, d, hw), lambda i, j: (mi, 0, LRU_HEADS + j)),
            pl.BlockSpec((None, width, hw), lambda i, j: (mi, 0, hcol(j))),
            vec, gate_w, gate_w, vec, vec, vec,
            pl.BlockSpec((None, ns, 1, hw), lambda i, j: (mi, i // tps, 0, hcol(j))),
            pl.BlockSpec((None, ns, width - 1, hw), lambda i, j: (mi, i // tps, 0, hcol(j))),
            pl.BlockSpec((None, 1, hw), lambda i, j: (li, 0, 0)),
            mem, mem,
        ],
        out_specs=[
            pl.BlockSpec((tl.bm, hw), lambda i, j: (i, hcol(j))),
            pl.BlockSpec((tl.bm, hw), lambda i, j: (i, mcol(j))),
            pl.BlockSpec((None, ns, 1, hw), lambda i, j: (i, 0, 0, hcol(j))),
            pl.BlockSpec((None, ns, width - 1, hw), lambda i, j: (i, 0, 0, hcol(j))),
        ],
        out_shape=[
            jax.ShapeDtypeStruct((m, d), BF16),
            jax.ShapeDtypeStruct((m, MEM_HEADS * hw), BF16),
            jax.ShapeDtypeStruct((tl.n_tiles, ns, 1, d), F32),
            jax.ShapeDtypeStruct((tl.n_tiles, ns, width - 1, d), F32),
        ],
        scratch_shapes=[
            pltpu.VMEM((tl.bm, d), BF16),
            pltpu.VMEM((LRU_HEADS, 1, hw), F32),
            pltpu.VMEM((LRU_HEADS, width - 1, hw), F32),
            pltpu.VMEM((4, tl.bm, hw), F32),
        ],
        compiler_params=_cparams(2),
        name="lru_in",
    )(x2d, p["norm_mix"], p["lru_w_in"], p["lru_w_in"], p["lru_conv_w"], p["lru_conv_b"],
      p["lru_gate_a_w"], p["lru_gate_x_w"], p["lru_gate_a_b"], p["lru_gate_x_b"], p["lru_lambda"],
      h0, c0, p["mem_q_norm"], mk, mv)


def _att_in_kernel(*refs, n_seg, seg_len, nq, row_chunk, aliased):
    if aliased:
        refs = refs[:8] + refs[10:]
    (x_ref, gn_ref, w_ref, qg_ref, kg_ref, gq_ref, mk_ref, mv_ref,
     qb_ref, kf_ref, kb_ref, vf_ref, vb_ref, mo_ref, hn_ref, z_ref) = refs
    j = pl.program_id(1)

    @pl.when(j == 0)
    def _():
        hn_ref[...] = _rms(x_ref[...], gn_ref[...]).astype(BF16)

    dk = qg_ref.shape[-1]
    n_groups = z_ref.shape[-1] // dk
    rows_per_t = hn_ref.shape[-1] // dk
    n_heads = rows_per_t // 2
    n_chunks = hn_ref.shape[0] // row_chunk
    w = w_ref[...].astype(BF16)

    def rows(c):
        return slice(c * row_chunk, (c + 1) * row_chunk)

    def project(c):
        z_ref[rows(c), :] = _dot(hn_ref[rows(c), :], w)

    def queries(c):
        qscale = dk ** -0.5 * LOG2E
        for g in range(n_groups):
            sl = slice(g * dk, (g + 1) * dk)
            qb_ref[rows(c), sl] = (_rms(z_ref[rows(c), sl], qg_ref[...]) * qscale).astype(BF16)

    def keys(c):
        for g in range(n_groups):
            sl = slice(g * dk, (g + 1) * dk)
            kn = _rms(z_ref[rows(c), sl], kg_ref[...])
            group = (j - nq) * n_groups + g
            kf_ref[pl.ds(c * row_chunk * rows_per_t + group, row_chunk, stride=rows_per_t), :] = kn
            kb_ref[rows(c), sl] = kn.astype(BF16)

    def values(c):
        z = z_ref[rows(c), :]
        vb_ref[rows(c), :] = z.astype(BF16)
        for g in range(n_groups):
            group = (j - 2 * nq) * n_groups + g
            row = (group & 1) * n_heads + (group >> 1)
            vf_ref[pl.ds(c * row_chunk * rows_per_t + row, row_chunk, stride=rows_per_t), :] = (
                z[:, g * dk:(g + 1) * dk])

    mem_chunk = max(row_chunk, min(hn_ref.shape[0], MEM_ATTN_ROWS))
    mem_segs = (mem_chunk // seg_len, seg_len) if n_seg > 1 else (1, mem_chunk)

    def mem_rows(c):
        return slice(c * mem_chunk, (c + 1) * mem_chunk)

    def mem_project(c):
        z_ref[mem_rows(c), :] = _dot(hn_ref[mem_rows(c), :], w)

    def memory(c):
        seg0 = c * mem_segs[0] if n_seg > 1 else 0
        mo_ref[mem_rows(c), :] = _mem_attn(z_ref[mem_rows(c), :], gq_ref[...], mk_ref, mv_ref,
                                           seg0, *mem_segs).astype(BF16)

    @pl.when(j < nq)
    def _():
        _two_stage(n_chunks, project, queries)

    @pl.when((j >= nq) & (j < 2 * nq))
    def _():
        _two_stage(n_chunks, project, keys)

    @pl.when((j >= 2 * nq) & (j < 3 * nq))
    def _():
        _two_stage(n_chunks, project, values)

    @pl.when(j >= 3 * nq)
    def _():
        _two_stage(hn_ref.shape[0] // mem_chunk, mem_project, memory)


def _att_in(x2d, tl, li, ai, p, mk, mv, bc, row_chunk, kv_all):
    m, d = x2d.shape
    mem_w = mk.shape[-1]
    nq, nm = d // bc, mem_w // bc
    n_steps = 3 * nq + nm
    ns, tps = tl.n_seg, tl.tiles_per_seq
    dk = p["attn_q_norm"].shape[-1]
    hd = p["mem_q_norm"].shape[-1]

    def col(lo, n):
        return lambda i, j: (i, jnp.clip(j - lo, 0, n - 1))

    mem = pl.BlockSpec((None, ns, mk.shape[2], bc),
                       lambda i, j: (li, i // tps, 0, jnp.clip(j - 3 * nq, 0, nm - 1)))
    assert tl.bm % row_chunk == 0 and (row_chunk % tl.seg_len == 0 if ns > 1 else True)
    n_attn = p["attn_w_in"].shape[0]
    rows_per_t = d // dk
    aliased = kv_all is not None
    kern = functools.partial(_att_in_kernel, n_seg=ns, seg_len=tl.seg_len, nq=nq, row_chunk=row_chunk,
                             aliased=aliased)
    kv_spec = _resident((None, tl.bm * rows_per_t, dk), lambda i, j: (ai, i, 0))
    kv_shape = jax.ShapeDtypeStruct((n_attn, m * rows_per_t, dk), F32)
    extra_in = [pl.BlockSpec(memory_space=pl.ANY)] * 2 if aliased else []
    return pl.pallas_call(
        kern,
        grid=(tl.n_tiles, n_steps),
        in_specs=[
            _resident((tl.bm, d), lambda i, j: (i, 0)),
            pl.BlockSpec((None, 1, d), lambda i, j: (li, 0, 0)),
            pl.BlockSpec((None, d, bc), lambda i, j: (ai, 0, j)),
            pl.BlockSpec((None, 1, dk), lambda i, j: (ai, 0, 0)),
            pl.BlockSpec((None, 1, dk), lambda i, j: (ai, 0, 0)),
            pl.BlockSpec((None, 1, hd), lambda i, j: (li, 0, 0)),
            mem, mem, *extra_in,
        ],
        out_specs=[
            pl.BlockSpec((tl.bm, bc), col(0, nq)),
            kv_spec,
            pl.BlockSpec((tl.bm, bc), col(nq, nq)),
            kv_spec,
            pl.BlockSpec((tl.bm, bc), col(2 * nq, nq)),
            pl.BlockSpec((tl.bm, bc), col(3 * nq, nm)),
        ],
        out_shape=[
            jax.ShapeDtypeStruct((m, d), BF16),
            kv_shape,
            jax.ShapeDtypeStruct((m, d), BF16),
            kv_shape,
            jax.ShapeDtypeStruct((m, d), BF16),
            jax.ShapeDtypeStruct((m, mem_w), BF16),
        ],
        input_output_aliases={8: 1, 9: 3} if aliased else {},
        scratch_shapes=[pltpu.VMEM((tl.bm, d), BF16), pltpu.VMEM((tl.bm, bc), F32)],
        compiler_params=_cparams(2),
        name="att_in",
    )(x2d, p["norm_mix"], p["attn_w_in"], p["attn_q_norm"], p["attn_k_norm"], p["mem_q_norm"], mk, mv,
      *(kv_all if aliased else ()))


def _attn_block(q, k, v, m_ref, l_ref, acc_ref, mask):
    dk = q.shape[-1] // 2
    tk = k.shape[0]
    scores = [_dot_nt(k[:, c * dk:(c + 1) * dk], q[:, c * dk:(c + 1) * dk]) for c in range(2)]
    for c in range(2):
        s = scores[c]
        if mask is not None:
            s = jnp.where(mask, s, NEG_BIG)
        m_old = m_ref[c]
        m_new = jnp.maximum(m_old, jnp.max(s, axis=0, keepdims=True))
        alpha = jnp.exp2(m_old - m_new)
        p = jnp.exp2(s - m_new)
        l_ref[c] = alpha * l_ref[c] + jnp.sum(p.reshape(tk // SUBLANES, SUBLANES, p.shape[-1]), axis=0)
        pv = lax.dot_general(v, p.astype(BF16), (((0,), (0,)), ((), ())), preferred_element_type=F32)
        acc_ref[c] = alpha * acc_ref[c] + pv
        m_ref[c] = m_new


def _attn_init(m_ref, l_ref, acc_ref):
    m_ref[...] = jnp.full(m_ref.shape, NEG_BIG, F32)
    l_ref[...] = jnp.zeros(l_ref.shape, F32)
    acc_ref[...] = jnp.zeros(acc_ref.shape, F32)


def _attn_finish(lp_ref, sg_ref, l_ref, acc_ref, lam_init):
    lp = lp_ref[...]
    lam = (jnp.exp(jnp.sum(lp[0:1] * lp[1:2], axis=-1, keepdims=True))
           - jnp.exp(jnp.sum(lp[2:3] * lp[3:4], axis=-1, keepdims=True)) + lam_init)
    l0 = jnp.sum(l_ref[0], axis=0, keepdims=True)
    l1 = jnp.sum(l_ref[1], axis=0, keepdims=True)
    o = (acc_ref[0] / l0 - lam * (acc_ref[1] / l1)).T
    return (_rms(o, sg_ref[...]) * (1.0 - lam_init)).astype(BF16)


def _attn_prompt_kernel(lp_ref, sg_ref, q_ref, k_ref, v_ref, o_ref, m_ref, l_ref, acc_ref, *, tq, tk, lam_init):
    qi = pl.program_id(2)
    _attn_init(m_ref, l_ref, acc_ref)
    q = q_ref[...]
    per = tk // tq

    def visible(start, size):
        start = pl.multiple_of(start, size)
        _attn_block(q, k_ref[pl.ds(start, size), :], v_ref[pl.ds(start, size), :], m_ref, l_ref, acc_ref, None)

    def body(kv, carry):
        visible(kv * tk, tk)
        return carry

    n_big = qi // per
    lax.fori_loop(0, n_big, body, 0)
    for r in range(1, per):
        @pl.when(qi % per >= r)
        def _():
            visible(n_big * tk + (r - 1) * tq, tq)

    start = pl.multiple_of(qi * tq, tq)
    shift = CHUNK.bit_length() - 1
    keyc = lax.shift_right_logical(lax.broadcasted_iota(jnp.int32, (tq, tq), 0), shift)
    qryc = lax.shift_right_logical(lax.broadcasted_iota(jnp.int32, (tq, tq), 1), shift)
    _attn_block(q, k_ref[pl.ds(start, tq), :], v_ref[pl.ds(start, tq), :], m_ref, l_ref, acc_ref, keyc <= qryc)
    o_ref[...] = _attn_finish(lp_ref, sg_ref, l_ref, acc_ref, lam_init)


def _attn_prompt(qb, kb, vb, ai, p, n_seq, t, tq, tk, lam_init):
    d = qb.shape[-1]
    hw = d // DIFF_HEADS
    assert t % tq == 0 and tq % CHUNK == 0 and tk % tq == 0
    q3, k3, v3 = (a.reshape(n_seq, t, d) for a in (qb, kb, vb))
    lp, sg = p["attn_lambda"], p["attn_subln"]
    kern = functools.partial(_attn_prompt_kernel, tq=tq, tk=tk, lam_init=lam_init)
    out = pl.pallas_call(
        kern,
        grid=(n_seq, DIFF_HEADS, t // tq),
        in_specs=[
            pl.BlockSpec((None,) + lp.shape[1:], lambda b, h, qi: (ai, 0, 0)),
            pl.BlockSpec((None, 1, hw), lambda b, h, qi: (ai, 0, 0)),
            pl.BlockSpec((None, tq, hw), lambda b, h, qi: (b, qi, h)),
            pl.BlockSpec((None, t, hw), lambda b, h, qi: (b, 0, h)),
            pl.BlockSpec((None, t, hw), lambda b, h, qi: (b, 0, h)),
        ],
        out_specs=pl.BlockSpec((None, tq, hw), lambda b, h, qi: (b, qi, h)),
        out_shape=jax.ShapeDtypeStruct((n_seq, t, d), BF16),
        scratch_shapes=[
            pltpu.VMEM((2, 1, tq), F32),
            pltpu.VMEM((2, SUBLANES, tq), F32),
            pltpu.VMEM((2, hw, tq), F32),
        ],
        compiler_params=_cparams(3),
        name="attn_prompt",
    )(lp, sg, q3, k3, v3)
    return out.reshape(n_seq * t, d)


def _attn_cached_kernel(lp_ref, sg_ref, q_ref, kp_ref, vp_ref, kn_ref, vn_ref, o_ref,
                        m_ref, l_ref, acc_ref, *, past, lam_init):
    n_heads = DIFF_HEADS
    rows_per_t = 2 * n_heads
    hw = q_ref.shape[-1] // n_heads
    for h in range(n_heads):
        cols = slice(h * hw, (h + 1) * hw)
        k = jnp.concatenate([kp_ref[pl.ds(2 * h + c, past, stride=rows_per_t), :] for c in range(2)], axis=-1)
        v = jnp.concatenate([vp_ref[pl.ds(c * n_heads + h, past, stride=rows_per_t), :] for c in range(2)],
                            axis=-1)
        q = q_ref[:, cols]
        _attn_init(m_ref, l_ref, acc_ref)
        _attn_block(q, k.astype(BF16), v.astype(BF16), m_ref, l_ref, acc_ref, None)
        _attn_block(q, kn_ref[:, cols], vn_ref[:, cols], m_ref, l_ref, acc_ref, None)
        o_ref[:, cols] = _attn_finish(lp_ref, sg_ref, l_ref, acc_ref, lam_init)


def _attn_cached(qb, kb, vb, k_past, v_past, ai, p, n_seq, t, lam_init):
    d = qb.shape[-1]
    hw = d // DIFF_HEADS
    n_attn, _, past, n_heads, _, dk = k_past.shape
    assert past % CHUNK == 0 and t <= CHUNK and n_heads == DIFF_HEADS and 2 * dk == hw
    q3, k3, v3 = (a.reshape(n_seq, t, d) for a in (qb, kb, vb))
    lp, sg = p["attn_lambda"], p["attn_subln"]
    k_rows = k_past.reshape(n_attn * n_seq, past * 2 * n_heads, dk)
    v_rows = v_past.reshape(n_attn, n_seq, past, n_heads, 2, dk).transpose(0, 1, 2, 4, 3, 5)
    v_rows = v_rows.reshape(n_attn * n_seq, past * 2 * n_heads, dk)
    kern = functools.partial(_attn_cached_kernel, past=past, lam_init=lam_init)
    new_spec = pl.BlockSpec((None, t, d), lambda b: (b, 0, 0))
    past_spec = pl.BlockSpec((None, past * 2 * n_heads, dk), lambda b: (ai * n_seq + b, 0, 0))
    out = pl.pallas_call(
        kern,
        grid=(n_seq,),
        in_specs=[
            pl.BlockSpec((None,) + lp.shape[1:], lambda b: (ai, 0, 0)),
            pl.BlockSpec((None, 1, hw), lambda b: (ai, 0, 0)),
            new_spec, past_spec, past_spec, new_spec, new_spec,
        ],
        out_specs=new_spec,
        out_shape=jax.ShapeDtypeStruct((n_seq, t, d), BF16),
        scratch_shapes=[
            pltpu.VMEM((2, 1, t), F32),
            pltpu.VMEM((2, SUBLANES, t), F32),
            pltpu.VMEM((2, hw, t), F32),
        ],
        compiler_params=_cparams(1),
        name="attn_cached",
    )(lp, sg, q3, k_rows, v_rows, k3, v3)
    return out.reshape(n_seq * t, d)


def _out_kernel(x_ref, mix_ref, mo_ref, w1_ref, w2_ref, o_ref):
    o_ref[...] = x_ref[...] + (_dot(mix_ref[...], w1_ref[...].astype(BF16))
                               + _dot(mo_ref[...], w2_ref[...].astype(BF16)))


def _out_proj(x2d, mix, mo, w_out, wi, bm, bn):
    m, d = x2d.shape
    k1, k2 = mix.shape[1], mo.shape[1]
    assert k1 % k2 == 0
    return pl.pallas_call(
        _out_kernel,
        grid=(m // bm, d // bn),
        in_specs=[
            pl.BlockSpec((bm, bn), lambda i, j: (i, j)),
            pl.BlockSpec((bm, k1), lambda i, j: (i, 0)),
            pl.BlockSpec((bm, k2), lambda i, j: (i, 0)),
            pl.BlockSpec((None, k1, bn), lambda i, j: (wi, 0, j)),
            pl.BlockSpec((None, k2, bn), lambda i, j: (wi, k1 // k2, j)),
        ],
        out_specs=pl.BlockSpec((bm, bn), lambda i, j: (i, j)),
        out_shape=jax.ShapeDtypeStruct((m, d), F32),
        compiler_params=_cparams(2),
        name="out_proj",
    )(x2d, mix, mo, w_out, w_out)


def _ffn_kernel(x_ref, gn_ref, wg_ref, wv_ref, cwg_ref, cwv_ref, cbg_ref, cbv_ref, stg_ref, stv_ref, wdn_ref,
                o_ref, nstg_ref, nstv_ref, hn_ref, cg_ref, cv_ref, raw_ref, act_ref,
                *, n_seg, seg_len, tiles_per_seq, row_chunk):
    i = pl.program_id(0)
    j = pl.program_id(1)
    carried = tiles_per_seq > 1

    @pl.when(j == 0)
    def _():
        x = x_ref[...]
        hn_ref[...] = _rms(x, gn_ref[...]).astype(BF16)
        o_ref[...] = x

    if carried:
        @pl.when(i % tiles_per_seq == 0)
        def _():
            cg_ref[j] = stg_ref[0]
            cv_ref[j] = stv_ref[0]

    halves = ((wg_ref, cwg_ref, cbg_ref, stg_ref, cg_ref, nstg_ref),
              (wv_ref, cwv_ref, cbv_ref, stv_ref, cv_ref, nstv_ref))
    n_prev = cwg_ref.shape[0] - 1
    n_chunks = hn_ref.shape[0] // row_chunk
    segs_per_chunk = row_chunk // seg_len if n_seg > 1 else 0
    prevs = [(h[4][j] if carried else h[3][0]) if n_seg == 1 else None for h in halves]

    def rows(c):
        return slice(c * row_chunk, (c + 1) * row_chunk)

    def up_proj(c):
        hn = hn_ref[rows(c), :]
        for hi, h in enumerate(halves):
            raw_ref[hi, rows(c), :] = _dot(hn, h[0][...])

    def conv_gate(c):
        ys = []
        for hi, (_, cw_ref, cb_ref, st_ref, _, nst_ref) in enumerate(halves):
            up = raw_ref[hi, rows(c), :]
            cw = cw_ref[...]
            cb = cb_ref[...]
            if n_seg == 1:
                ys.append(_causal_conv(up, prevs[hi], cw, cb))
                prevs[hi] = up[row_chunk - n_prev:row_chunk]
            else:
                parts = []
                for s in range(segs_per_chunk):
                    seg = up[s * seg_len:(s + 1) * seg_len]
                    parts.append(_causal_conv(seg, st_ref[c * segs_per_chunk + s], cw, cb))
                    nst_ref[c * segs_per_chunk + s] = seg[seg_len - n_prev:seg_len]
                ys.append(_cat_rows(parts))
        act_ref[rows(c), :] = (_gelu(ys[0]) * ys[1]).astype(BF16)

    def down_proj(c):
        o_ref[rows(c), :] += _dot(act_ref[rows(c), :], wdn_ref[...])

    for t in range(n_chunks + 2):
        if t < n_chunks:
            up_proj(t)
        if 0 <= t - 1 < n_chunks:
            conv_gate(t - 1)
        if 0 <= t - 2 < n_chunks:
            down_proj(t - 2)
    if n_seg == 1:
        for hi, (_, _, _, _, carry_ref, nst_ref) in enumerate(halves):
            nst_ref[0] = prevs[hi]
            if carried:
                carry_ref[j] = prevs[hi]


def _ffn(x2d, tl, li, p, state, bf, row_chunk):
    m, d = x2d.shape
    dff = p["ffn_w_down"].shape[1]
    nj = dff // bf
    ns, tps = tl.n_seg, tl.tiles_per_seq
    width = p["ffn_conv_w"].shape[1]
    assert tl.bm % row_chunk == 0 and (row_chunk % tl.seg_len == 0 if ns > 1 else True)
    kern = functools.partial(_ffn_kernel, n_seg=ns, seg_len=tl.seg_len, tiles_per_seq=tps, row_chunk=row_chunk)

    def halves(block, index_map):
        gate = pl.BlockSpec(block, lambda i, j: index_map(i, j, j))
        value = pl.BlockSpec(block, lambda i, j: index_map(i, j, nj + j))
        return [gate, value]

    nst_spec = pl.BlockSpec((None, ns, width - 1, bf), lambda i, j: (i, 0, 0, j))
    nst_shape = jax.ShapeDtypeStruct((tl.n_tiles, ns, width - 1, dff), F32)
    return pl.pallas_call(
        kern,
        grid=(tl.n_tiles, nj),
        in_specs=[
            _resident((tl.bm, d), lambda i, j: (i, 0)),
            pl.BlockSpec((None, 1, d), lambda i, j: (li, 0, 0)),
            *halves((None, d, bf), lambda i, j, c: (li, 0, c)),
            *halves((None, width, bf), lambda i, j, c: (li, 0, c)),
            *halves((None, 1, bf), lambda i, j, c: (li, 0, c)),
            *halves((None, ns, width - 1, bf), lambda i, j, c: (li, i // tps, 0, c)),
            pl.BlockSpec((None, bf, d), lambda i, j: (li, j, 0)),
        ],
        out_specs=[pl.BlockSpec((tl.bm, d), lambda i, j: (i, 0)), nst_spec, nst_spec],
        out_shape=[jax.ShapeDtypeStruct((m, d), F32), nst_shape, nst_shape],
        scratch_shapes=[
            pltpu.VMEM((tl.bm, d), BF16),
            pltpu.VMEM((nj, width - 1, bf), F32),
            pltpu.VMEM((nj, width - 1, bf), F32),
            pltpu.VMEM((2, tl.bm, bf), F32),
            pltpu.VMEM((tl.bm, bf), BF16),
        ],
        compiler_params=_cparams(2),
        name="ffn",
    )(x2d, p["norm_ffn"], p["ffn_w_up"], p["ffn_w_up"], p["ffn_conv_w"], p["ffn_conv_w"],
      p["ffn_conv_b"], p["ffn_conv_b"], state, state, p["ffn_w_down"])


def _run_trunk(x, mem_k, mem_v, lru_h, lru_conv, attn_past, ffn_conv, p, cfg):
    n_seq, t, d = x.shape
    depth = p["norm_mix"].shape[0]
    x2d = x.reshape(n_seq * t, d)
    tl = _Tiling(n_seq, t, cfg["bm"])
    tl_ffn = _Tiling(n_seq, t, cfg["bm_ffn"])
    mem_tokens = mem_k.shape[2]
    mk = mem_k.reshape(depth, n_seq, mem_tokens, -1)
    mv = mem_v.reshape(depth, n_seq, mem_tokens, -1)
    h0 = lru_h[:, :, None, :]
    new_h, new_lconv, new_fconv = [], [], []
    kv_all = None
    for i in range(depth):
        j = i // 2
        if i % 2 == 0:
            mix, mo, h_last, buf = _lru_in(x2d, tl, i, j, p, h0, lru_conv, mk, mv, cfg["lru_rows"])
            new_h.append(_last_tile_state(h_last, tl)[:, 0, :])
            new_lconv.append(_last_tile_state(buf, tl))
            w_out = p["lru_w_out"]
        else:
            qb, k_all, kb, v_all, vb, mo = _att_in(x2d, tl, i, j, p, mk, mv, cfg["bc"], cfg["att_rows"], kv_all)
            kv_all = (k_all, v_all)
            lam_init = 0.8 - 0.6 * math.exp(-0.3 * i)
            if attn_past is None:
                mix = _attn_prompt(qb, kb, vb, j, p, n_seq, t, cfg["tq"], cfg["tk"], lam_init)
            else:
                mix = _attn_cached(qb, kb, vb, attn_past[0], attn_past[1], j, p, n_seq, t, lam_init)
            w_out = p["attn_w_out"]
        x2d = _out_proj(x2d, mix, mo, w_out, j, cfg["bm_out"], cfg["bn_out"])
        x2d, fg, fv = _ffn(x2d, tl_ffn, i, p, ffn_conv, cfg["bf"], cfg["ffn_rows"])
        new_fconv.append(jnp.concatenate([_last_tile_state(fg, tl_ffn), _last_tile_state(fv, tl_ffn)], axis=-1))
    return x2d.reshape(n_seq, t, d), new_h, new_lconv, kv_all[0], kv_all[1], new_fconv


def _forward(x_prompt, x_sample, mem_prompt, cache_attn_k, cache_attn_v, cache_mem_k, cache_mem_v,
             state_lru_h, state_lru_conv, state_ffn_conv, p, mem_norm, mem_w_kv, mem_k_norm, cfg_p, cfg_s):
    b, t, d = x_prompt.shape
    depth = p["norm_mix"].shape[0]
    n_lru, n_attn = (depth + 1) // 2, depth // 2

    p = dict(p)
    for name in ("ffn_w_up", "ffn_w_down"):
        p[name] = p[name].astype(BF16)
    for name in ("norm_mix", "norm_ffn", "lru_conv_b", "lru_gate_a_b", "lru_gate_x_b", "lru_lambda",
                 "attn_q_norm", "attn_k_norm", "attn_subln", "mem_q_norm", "ffn_conv_b"):
        p[name] = p[name][:, None, :]

    mem_tokens = mem_prompt.shape[1]
    mk, mv = _mem_kv(mem_prompt.reshape(b * mem_tokens, d), mem_norm[:, None, :], mem_w_kv,
                     mem_k_norm[:, None, :])
    hd = mem_k_norm.shape[-1]
    p_mem_k = mk.reshape(depth, b, mem_tokens, MEM_HEADS, hd)
    p_mem_v = mv.reshape(depth, b, mem_tokens, MEM_HEADS, hd)

    zeros_h = jnp.zeros((n_lru, b, d), F32)
    zeros_lconv = jnp.zeros((n_lru, b) + state_lru_conv.shape[2:], F32)
    zeros_fconv = jnp.zeros((depth, b) + state_ffn_conv.shape[2:], F32)
    yp, ph, plc, pk, pv, pfc = _run_trunk(x_prompt, p_mem_k, p_mem_v, zeros_h, zeros_lconv, None,
                                          zeros_fconv, p, cfg_p)
    ys, sh, slc, sk, sv, sfc = _run_trunk(x_sample, cache_mem_k, cache_mem_v, state_lru_h, state_lru_conv,
                                          (cache_attn_k, cache_attn_v), state_ffn_conv, p, cfg_s)

    dk = p["attn_q_norm"].shape[-1]
    db, dt = x_sample.shape[0], x_sample.shape[1]

    def kshape(a, n, tt):
        return a.reshape(n_attn, n, tt, DIFF_HEADS, 2, dk)

    def vshape(a, n, tt):
        a = a.reshape(n_attn, n, tt, 2, DIFF_HEADS, dk)
        return a.transpose(0, 1, 2, 4, 3, 5).reshape(n_attn, n, tt, DIFF_HEADS, 2 * dk)

    return (yp, ys, jnp.stack(ph), jnp.stack(plc), kshape(pk, b, t), vshape(pv, b, t), p_mem_k, p_mem_v,
            jnp.stack(pfc), jnp.stack(sh), jnp.stack(slc), kshape(sk, db, dt), vshape(sv, db, dt),
            jnp.stack(sfc))


CFG_PROMPT = dict(bm=1024, bc=512, lru_rows=128, att_rows=256, bm_ffn=1024, bf=512, ffn_rows=512,
                  bm_out=1024, bn_out=512, tq=512, tk=1024)
CFG_SAMPLE = dict(bm=512, bc=256, lru_rows=128, att_rows=256, bm_ffn=1024, bf=512, ffn_rows=512,
                  bm_out=1024, bn_out=512)


def kernel(x_prompt, x_sample, mem_prompt, cache_attn_k, cache_attn_v, cache_mem_k, cache_mem_v, state_lru_h, state_lru_conv, state_ffn_conv, norm_mix, norm_ffn, lru_w_in, lru_conv_w, lru_conv_b, lru_gate_a_w, lru_gate_a_b, lru_gate_x_w, lru_gate_x_b, lru_lambda, lru_w_out, attn_w_in, attn_q_norm, attn_k_norm, attn_lambda, attn_subln, attn_w_out, mem_norm, mem_w_kv, mem_q_norm, mem_k_norm, ffn_w_up, ffn_conv_w, ffn_conv_b, ffn_w_down):
    p = {
        "norm_mix": norm_mix, "norm_ffn": norm_ffn,
        "lru_w_in": lru_w_in, "lru_conv_w": lru_conv_w, "lru_conv_b": lru_conv_b,
        "lru_gate_a_w": lru_gate_a_w, "lru_gate_a_b": lru_gate_a_b,
        "lru_gate_x_w": lru_gate_x_w, "lru_gate_x_b": lru_gate_x_b,
        "lru_lambda": lru_lambda, "lru_w_out": lru_w_out,
        "attn_w_in": attn_w_in, "attn_q_norm": attn_q_norm, "attn_k_norm": attn_k_norm,
        "attn_lambda": attn_lambda, "attn_subln": attn_subln, "attn_w_out": attn_w_out,
        "mem_q_norm": mem_q_norm,
        "ffn_w_up": ffn_w_up, "ffn_conv_w": ffn_conv_w, "ffn_conv_b": ffn_conv_b, "ffn_w_down": ffn_w_down,
    }
    return _forward(x_prompt, x_sample, mem_prompt, cache_attn_k, cache_attn_v, cache_mem_k, cache_mem_v,
                    state_lru_h, state_lru_conv, state_ffn_conv, p, mem_norm, mem_w_kv, mem_k_norm,
                    CFG_PROMPT, CFG_SAMPLE)
```

```python
import functools
import math

import jax
import jax.numpy as jnp
from jax import lax
from jax.experimental import pallas as pl
from jax.experimental.pallas import tpu as pltpu

F32 = jnp.float32
BF16 = jnp.bfloat16

CHUNK = 64
LRU_HEADS = 8
LRU_C = 8.0
DIFF_HEADS = 8
MEM_HEADS = 4
RMS_EPS = 1e-6
NEG_BIG = -1e30
LOG2E = 1.4426950408889634
SUBLANES = 8
MEM_ATTN_ROWS = 512

V7X_VMEM_LIMIT_BYTES = 58 * 1024 * 1024


def _cparams(n_grid_axes):
    return pltpu.CompilerParams(
        dimension_semantics=("arbitrary",) * n_grid_axes,
        vmem_limit_bytes=V7X_VMEM_LIMIT_BYTES,
    )


def _dot(a, b):
    return jnp.dot(a, b, preferred_element_type=F32)


def _dot_nt(a, b):
    return lax.dot_general(a, b, (((1,), (1,)), ((), ())), preferred_element_type=F32)


def _rms(x, g):
    return x * lax.rsqrt(jnp.mean(x * x, axis=-1, keepdims=True) + RMS_EPS) * g


def _gelu(x):
    return x * (0.5 * (1.0 + jnp.tanh(0.7978845608028654 * (x + 0.044715 * (x * x * x)))))


def _cat_rows(parts):
    return parts[0] if len(parts) == 1 else jnp.concatenate(parts, axis=0)


def _delayed(seg, prev, d):
    n_prev = prev.shape[0]
    v = pltpu.roll(seg, d, 0)
    row = lax.broadcasted_iota(jnp.int32, (SUBLANES, 1), 0)
    top = v[0:SUBLANES]
    for r in range(d):
        top = jnp.where(row == r, prev[n_prev - d + r:n_prev - d + r + 1], top)
    return jnp.concatenate([top, v[SUBLANES:]], axis=0)


def _causal_conv(seg, prev, cw, cb):
    n_prev = cw.shape[0] - 1
    y = cb + _delayed(seg, prev, n_prev) * cw[0:1]
    for t in range(1, n_prev):
        y = y + _delayed(seg, prev, n_prev - t) * cw[t:t + 1]
    return y + seg * cw[n_prev:n_prev + 1]


def _scan_rows(a, b):
    n = a.shape[0]
    row = lax.broadcasted_iota(jnp.int32, (n, 1), 0)
    sh = 1
    while sh < min(SUBLANES, n):
        valid = row >= sh
        a_sh = pltpu.roll(a, sh, 0)
        b_sh = pltpu.roll(b, sh, 0)
        b = jnp.where(valid, b + a * b_sh, b)
        a = jnp.where(valid, a * a_sh, a)
        sh *= 2
    while sh < n:
        b = jnp.concatenate([b[:sh], b[sh:] + a[sh:] * b[:n - sh]], axis=0)
        a = jnp.concatenate([a[:sh], a[sh:] * a[:n - sh]], axis=0)
        sh *= 2
    return a, b


def _two_stage(n_chunks, first, second):
    for t in range(n_chunks + 1):
        if t < n_chunks:
            first(t)
        if t >= 1:
            second(t - 1)


def _mem_attn(q, gq, mk_ref, mv_ref, seg0, n_seg, seg_len):
    hd = gq.shape[-1]
    outs = []
    for h in range(q.shape[-1] // hd):
        cols = slice(h * hd, (h + 1) * hd)
        qn = _rms(q[:, cols], gq).astype(BF16)
        segs = []
        for s in range(n_seg):
            k = mk_ref[seg0 + s, :, cols]
            v = mv_ref[seg0 + s, :, cols]
            sc = _dot_nt(qn[s * seg_len:(s + 1) * seg_len], k) * (hd ** -0.5)
            p = jnp.exp(sc - jnp.max(sc, axis=-1, keepdims=True))
            pr = p / jnp.sum(p, axis=-1, keepdims=True)
            segs.append(_dot(pr.astype(BF16), v))
        outs.append(_cat_rows(segs))
    return outs[0] if len(outs) == 1 else jnp.concatenate(outs, axis=-1)


class _Tiling:
    def __init__(self, n_seq, t, bm):
        self.n_seq, self.t = n_seq, t
        if t >= bm:
            assert t % bm == 0
            self.n_seg, self.seg_len, self.tiles_per_seq = 1, bm, t // bm
        else:
            assert bm % t == 0 and n_seq % (bm // t) == 0
            self.n_seg, self.seg_len, self.tiles_per_seq = bm // t, t, 1
        self.bm = self.n_seg * self.seg_len
        self.n_tiles = n_seq * t // self.bm
        assert self.seg_len & (self.seg_len - 1) == 0 and self.seg_len >= SUBLANES


def _last_tile_state(per_tile, tl):
    rows, c = per_tile.shape[2:]
    return per_tile.reshape(tl.n_seq, tl.tiles_per_seq, rows, c)[:, -1]


def _resident(block_shape, index_map):
    return pl.BlockSpec(block_shape, index_map, pipeline_mode=pl.Buffered(1))


def _mem_kv_kernel(mem_ref, gn_ref, wk_ref, wv_ref, gk_ref, k_ref, v_ref, hn_ref):
    @pl.when(pl.program_id(1) == 0)
    def _():
        hn_ref[...] = _rms(mem_ref[...], gn_ref[...]).astype(BF16)

    hn = hn_ref[...]
    k_ref[...] = _rms(_dot(hn, wk_ref[...].astype(BF16)), gk_ref[...])
    v_ref[...] = _dot(hn, wv_ref[...].astype(BF16))


def _mem_kv(mem2d, mem_norm, w_kv, mem_k_norm):
    depth, d, two_w = w_kv.shape
    mem_w = two_w // 2
    hd = mem_w // MEM_HEADS
    m = mem2d.shape[0]
    return pl.pallas_call(
        _mem_kv_kernel,
        grid=(depth, MEM_HEADS),
        in_specs=[
            pl.BlockSpec((m, d), lambda l, j: (0, 0)),
            pl.BlockSpec((None, 1, d), lambda l, j: (l, 0, 0)),
            pl.BlockSpec((None, d, hd), lambda l, j: (l, 0, j)),
            pl.BlockSpec((None, d, hd), lambda l, j: (l, 0, MEM_HEADS + j)),
            pl.BlockSpec((None, 1, hd), lambda l, j: (l, 0, 0)),
        ],
        out_specs=[
            pl.BlockSpec((None, m, hd), lambda l, j: (l, 0, j)),
            pl.BlockSpec((None, m, hd), lambda l, j: (l, 0, j)),
        ],
        out_shape=[jax.ShapeDtypeStruct((depth, m, mem_w), F32)] * 2,
        scratch_shapes=[pltpu.VMEM((m, d), BF16)],
        compiler_params=_cparams(2),
        name="mem_kv",
    )(mem2d, mem_norm, w_kv, w_kv, mem_k_norm)


def _lru_in_kernel(x_ref, gn_ref, wg_ref, wx_ref, cw_ref, cb_ref, wa_ref, wi_ref, ba_ref, bi_ref,
                   lam_ref, h0_ref, c0_ref, gq_ref, mk_ref, mv_ref,
                   mix_ref, mo_ref, hn_out_ref, cn_out_ref,
                   hn_ref, hc_ref, cc_ref, raw_ref, *, n_seg, seg_len, tiles_per_seq, row_chunk):
    i = pl.program_id(0)
    j = pl.program_id(1)
    carried = tiles_per_seq > 1

    @pl.when(j == 0)
    def _():
        hn_ref[...] = _rms(x_ref[...], gn_ref[...]).astype(BF16)

    @pl.when(j < LRU_HEADS)
    def _():
        if carried:
            @pl.when(i % tiles_per_seq == 0)
            def _():
                cc_ref[j] = c0_ref[0]
                hc_ref[j] = h0_ref[0]

        cw = cw_ref[...]
        cb = cb_ref[...]
        wg, wx, wa, wi = (r[...].astype(BF16) for r in (wg_ref, wx_ref, wa_ref, wi_ref))
        n_prev = cw.shape[0] - 1
        n_chunks = hn_ref.shape[0] // row_chunk
        segs_per_chunk = row_chunk // seg_len if n_seg > 1 else 0
        whole = n_seg == 1
        state = {"conv": (cc_ref[j] if carried else c0_ref[0]) if whole else None,
                 "h": (hc_ref[j] if carried else h0_ref[0]) if whole else None}
        log_lam = jax.nn.log_sigmoid(lam_ref[...])

        def rows(c):
            return slice(c * row_chunk, (c + 1) * row_chunk)

        def in_proj(c):
            hn = hn_ref[rows(c), :]
            raw_ref[0, rows(c), :] = _dot(hn, wg)
            raw_ref[1, rows(c), :] = _dot(hn, wx)

        def conv_gates(c):
            xr = raw_ref[1, rows(c), :]
            if whole:
                xc = _causal_conv(xr, state["conv"], cw, cb)
                state["conv"] = xr[row_chunk - n_prev:row_chunk]
            else:
                parts = []
                for s in range(segs_per_chunk):
                    sidx = c * segs_per_chunk + s
                    seg = xr[s * seg_len:(s + 1) * seg_len]
                    parts.append(_causal_conv(seg, c0_ref[sidx], cw, cb))
                    cn_out_ref[sidx] = seg[seg_len - n_prev:seg_len]
                xc = _cat_rows(parts)
            xcb = xc.astype(BF16)
            raw_ref[1, rows(c), :] = xc
            raw_ref[2, rows(c), :] = _dot(xcb, wa)
            raw_ref[3, rows(c), :] = _dot(xcb, wi)

        def recur(c):
            xc = raw_ref[1, rows(c), :]
            r = jax.nn.sigmoid(raw_ref[2, rows(c), :] + ba_ref[...])
            ig = jax.nn.sigmoid(raw_ref[3, rows(c), :] + bi_ref[...])
            log_a = (LRU_C * r) * log_lam
            a = jnp.exp(log_a)
            one_minus_a2 = -jnp.tanh(log_a) * (a * a + 1.0)
            gated = jnp.sqrt(one_minus_a2) * (ig * xc)
            if whole:
                cum_a, cum_b = _scan_rows(a, gated)
                hs = cum_b + cum_a * state["h"]
                state["h"] = hs[row_chunk - 1:row_chunk]
            else:
                parts = []
                for s in range(segs_per_chunk):
                    sidx = c * segs_per_chunk + s
                    sl = slice(s * seg_len, (s + 1) * seg_len)
                    cum_a, cum_b = _scan_rows(a[sl], gated[sl])
                    seg_hs = cum_b + cum_a * h0_ref[sidx]
                    hn_out_ref[sidx] = seg_hs[seg_len - 1:seg_len]
                    parts.append(seg_hs)
                hs = _cat_rows(parts)
            mix_ref[rows(c), :] = (_gelu(raw_ref[0, rows(c), :]) * hs).astype(BF16)

        for t in range(n_chunks + 2):
            if t < n_chunks:
                in_proj(t)
            if 0 <= t - 1 < n_chunks:
                conv_gates(t - 1)
            if 0 <= t - 2 < n_chunks:
                recur(t - 2)
        if whole:
            cn_out_ref[0] = state["conv"]
            hn_out_ref[0] = state["h"]
            if carried:
                cc_ref[j] = state["conv"]
                hc_ref[j] = state["h"]

    @pl.when(j >= LRU_HEADS)
    def _():
        chunk = max(row_chunk, min(hn_ref.shape[0], MEM_ATTN_ROWS))
        n_chunks = hn_ref.shape[0] // chunk
        segs = (chunk // seg_len, seg_len) if n_seg > 1 else (1, chunk)

        wq = wx_ref[...].astype(BF16)

        def rows(c):
            return slice(c * chunk, (c + 1) * chunk)

        def project(c):
            raw_ref[0, rows(c), :] = _dot(hn_ref[rows(c), :], wq)

        def attend(c):
            seg0 = c * segs[0] if n_seg > 1 else 0
            mo_ref[rows(c), :] = _mem_attn(raw_ref[0, rows(c), :], gq_ref[...], mk_ref, mv_ref,
                                           seg0, *segs).astype(BF16)

        _two_stage(n_chunks, project, attend)


def _lru_in(x2d, tl, li, mi, p, h0, c0, mk, mv, row_chunk):
    m, d = x2d.shape
    hw = d // LRU_HEADS
    n_steps = LRU_HEADS + MEM_HEADS
    ns, tps = tl.n_seg, tl.tiles_per_seq
    width = p["lru_conv_w"].shape[1]
    last = LRU_HEADS - 1

    def hcol(j):
        return jnp.minimum(j, last)

    def mcol(j):
        return jnp.maximum(j - LRU_HEADS, 0)

    vec = pl.BlockSpec((None, 1, hw), lambda i, j: (mi, 0, hcol(j)))
    gate_w = pl.BlockSpec((None, None, hw, hw), lambda i, j: (mi, hcol(j), 0, 0))
    mem = pl.BlockSpec((None, ns, mk.shape[2], hw), lambda i, j: (li, i // tps, 0, mcol(j)))
    assert tl.bm % row_chunk == 0 and (row_chunk % tl.seg_len == 0 if ns > 1 else True)
    kern = functools.partial(_lru_in_kernel, n_seg=ns, seg_len=tl.seg_len, tiles_per_seq=tps,
                             row_chunk=row_chunk)
    return pl.pallas_call(
        kern,
        grid=(tl.n_tiles, n_steps),
        in_specs=[
            _resident((tl.bm, d), lambda i, j: (i, 0)),
            pl.BlockSpec((None, 1, d), lambda i, j: (li, 0, 0)),
            pl.BlockSpec((None, d, hw), lambda i, j: (mi, 0, hcol(j))),
            pl.BlockSpec((None, d, hw), lambda i, j: (mi, 0, LRU_HEADS + j)),
            pl.BlockSpec((None, width, hw), lambda i, j: (mi, 0, hcol(j))),
            vec, gate_w, gate_w, vec, vec, vec,
            pl.BlockSpec((None, ns, 1, hw), lambda i, j: (mi, i // tps, 0, hcol(j))),
            pl.BlockSpec((None, ns, width - 1, hw), lambda i, j: (mi, i // tps, 0, hcol(j))),
            pl.BlockSpec((None, 1, hw), lambda i, j: (li, 0, 0)),
            mem, mem,
        ],
        out_specs=[
            pl.BlockSpec((tl.bm, hw), lambda i, j: (i, hcol(j))),
            pl.BlockSpec((tl.bm, hw), lambda i, j: (i, mcol(j))),
            pl.BlockSpec((None, ns, 1, hw), lambda i, j: (i, 0, 0, hcol(j))),
            pl.BlockSpec((None, ns, width - 1, hw), lambda i, j: (i, 0, 0, hcol(j))),
        ],
        out_shape=[
            jax.ShapeDtypeStruct((m, d), BF16),
            jax.ShapeDtypeStruct((m, MEM_HEADS * hw), BF16),
            jax.ShapeDtypeStruct((tl.n_tiles, ns, 1, d), F32),
            jax.ShapeDtypeStruct((tl.n_tiles, ns, width - 1, d), F32),
        ],
        scratch_shapes=[
            pltpu.VMEM((tl.bm, d), BF16),
            pltpu.VMEM((LRU_HEADS, 1, hw), F32),
            pltpu.VMEM((LRU_HEADS, width - 1, hw), F32),
            pltpu.VMEM((4, tl.bm, hw), F32),
        ],
        compiler_params=_cparams(2),
        name="lru_in",
    )(x2d, p["norm_mix"], p["lru_w_in"], p["lru_w_in"], p["lru_conv_w"], p["lru_conv_b"],
      p["lru_gate_a_w"], p["lru_gate_x_w"], p["lru_gate_a_b"], p["lru_gate_x_b"], p["lru_lambda"],
      h0, c0, p["mem_q_norm"], mk, mv)


def _att_in_kernel(*refs, n_seg, seg_len, nq, row_chunk, aliased):
    if aliased:
        refs = refs[:8] + refs[10:]
    (x_ref, gn_ref, w_ref, qg_ref, kg_ref, gq_ref, mk_ref, mv_ref,
     qb_ref, kf_ref, kb_ref, vf_ref, vb_ref, mo_ref, hn_ref, z_ref) = refs
    j = pl.program_id(1)

    @pl.when(j == 0)
    def _():
        hn_ref[...] = _rms(x_ref[...], gn_ref[...]).astype(BF16)

    dk = qg_ref.shape[-1]
    n_groups = z_ref.shape[-1] // dk
    rows_per_t = hn_ref.shape[-1] // dk
    n_heads = rows_per_t // 2
    n_chunks = hn_ref.shape[0] // row_chunk
    w = w_ref[...].astype(BF16)

    def rows(c):
        return slice(c * row_chunk, (c + 1) * row_chunk)

    def project(c):
        z_ref[rows(c), :] = _dot(hn_ref[rows(c), :], w)

    def queries(c):
        qscale = dk ** -0.5 * LOG2E
        for g in range(n_groups):
            sl = slice(g * dk, (g + 1) * dk)
            qb_ref[rows(c), sl] = (_rms(z_ref[rows(c), sl], qg_ref[...]) * qscale).astype(BF16)

    def keys(c):
        for g in range(n_groups):
            sl = slice(g * dk, (g + 1) * dk)
            kn = _rms(z_ref[rows(c), sl], kg_ref[...])
            group = (j - nq) * n_groups + g
            kf_ref[pl.ds(c * row_chunk * rows_per_t + group, row_chunk, stride=rows_per_t), :] = kn
            kb_ref[rows(c), sl] = kn.astype(BF16)

    def values(c):
        z = z_ref[rows(c), :]
        vb_ref[rows(c), :] = z.astype(BF16)
        for g in range(n_groups):
            group = (j - 2 * nq) * n_groups + g
            row = (group & 1) * n_heads + (group >> 1)
            vf_ref[pl.ds(c * row_chunk * rows_per_t + row, row_chunk, stride=rows_per_t), :] = (
                z[:, g * dk:(g + 1) * dk])

    mem_chunk = max(row_chunk, min(hn_ref.shape[0], MEM_ATTN_ROWS))
    mem_segs = (mem_chunk // seg_len, seg_len) if n_seg > 1 else (1, mem_chunk)

    def mem_rows(c):
        return slice(c * mem_chunk, (c + 1) * mem_chunk)

    def mem_project(c):
        z_ref[mem_rows(c), :] = _dot(hn_ref[mem_rows(c), :], w)

    def memory(c):
        seg0 = c * mem_segs[0] if n_seg > 1 else 0
        mo_ref[mem_rows(c), :] = _mem_attn(z_ref[mem_rows(c), :], gq_ref[...], mk_ref, mv_ref,
                                           seg0, *mem_segs).astype(BF16)

    @pl.when(j < nq)
    def _():
        _two_stage(n_chunks, project, queries)

    @pl.when((j >= nq) & (j < 2 * nq))
    def _():
        _two_stage(n_chunks, project, keys)

    @pl.when((j >= 2 * nq) & (j < 3 * nq))
    def _():
        _two_stage(n_chunks, project, values)

    @pl.when(j >= 3 * nq)
    def _():
        _two_stage(hn_ref.shape[0] // mem_chunk, mem_project, memory)


def _att_in(x2d, tl, li, ai, p, mk, mv, bc, row_chunk, kv_all):
    m, d = x2d.shape
    mem_w = mk.shape[-1]
    nq, nm = d // bc, mem_w // bc
    n_steps = 3 * nq + nm
    ns, tps = tl.n_seg, tl.tiles_per_seq
    dk = p["attn_q_norm"].shape[-1]
    hd = p["mem_q_norm"].shape[-1]

    def col(lo, n):
        return lambda i, j: (i, jnp.clip(j - lo, 0, n - 1))

    mem = _resident((None, ns, mk.shape[2], bc),
                    lambda i, j: (li, i // tps, 0, jnp.clip(j - 3 * nq, 0, nm - 1)))
    assert tl.bm % row_chunk == 0 and (row_chunk % tl.seg_len == 0 if ns > 1 else True)
    n_attn = p["attn_w_in"].shape[0]
    rows_per_t = d // dk
    aliased = kv_all is not None
    kern = functools.partial(_att_in_kernel, n_seg=ns, seg_len=tl.seg_len, nq=nq, row_chunk=row_chunk,
                             aliased=aliased)
    kv_spec = _resident((None, tl.bm * rows_per_t, dk), lambda i, j: (ai, i, 0))
    kv_shape = jax.ShapeDtypeStruct((n_attn, m * rows_per_t, dk), F32)
    extra_in = [pl.BlockSpec(memory_space=pl.ANY)] * 2 if aliased else []
    return pl.pallas_call(
        kern,
        grid=(tl.n_tiles, n_steps),
        in_specs=[
            _resident((tl.bm, d), lambda i, j: (i, 0)),
            pl.BlockSpec((None, 1, d), lambda i, j: (li, 0, 0)),
            pl.BlockSpec((None, d, bc), lambda i, j: (ai, 0, j)),
            pl.BlockSpec((None, 1, dk), lambda i, j: (ai, 0, 0)),
            pl.BlockSpec((None, 1, dk), lambda i, j: (ai, 0, 0)),
            pl.BlockSpec((None, 1, hd), lambda i, j: (li, 0, 0)),
            mem, mem, *extra_in,
        ],
        out_specs=[
            pl.BlockSpec((tl.bm, bc), col(0, nq)),
            kv_spec,
            pl.BlockSpec((tl.bm, bc), col(nq, nq)),
            kv_spec,
            pl.BlockSpec((tl.bm, bc), col(2 * nq, nq)),
            pl.BlockSpec((tl.bm, bc), col(3 * nq, nm)),
        ],
        out_shape=[
            jax.ShapeDtypeStruct((m, d), BF16),
            kv_shape,
            jax.ShapeDtypeStruct((m, d), BF16),
            kv_shape,
            jax.ShapeDtypeStruct((m, d), BF16),
            jax.ShapeDtypeStruct((m, mem_w), BF16),
        ],
        input_output_aliases={8: 1, 9: 3} if aliased else {},
        scratch_shapes=[pltpu.VMEM((tl.bm, d), BF16), pltpu.VMEM((tl.bm, bc), F32)],
        compiler_params=_cparams(2),
        name="att_in",
    )(x2d, p["norm_mix"], p["attn_w_in"], p["attn_q_norm"], p["attn_k_norm"], p["mem_q_norm"], mk, mv,
      *(kv_all if aliased else ()))


def _attn_block(q, k, v, m_ref, l_ref, acc_ref, mask):
    dk = q.shape[-1] // 2
    tk = k.shape[0]
    scores = [_dot_nt(k[:, c * dk:(c + 1) * dk], q[:, c * dk:(c + 1) * dk]) for c in range(2)]
    for c in range(2):
        s = scores[c]
        if mask is not None:
            s = jnp.where(mask, s, NEG_BIG)
        m_old = m_ref[c]
        m_new = jnp.maximum(m_old, jnp.max(s, axis=0, keepdims=True))
        alpha = jnp.exp2(m_old - m_new)
        p = jnp.exp2(s - m_new)
        l_ref[c] = alpha * l_ref[c] + jnp.sum(p.reshape(tk // SUBLANES, SUBLANES, p.shape[-1]), axis=0)
        pv = lax.dot_general(v, p.astype(BF16), (((0,), (0,)), ((), ())), preferred_element_type=F32)
        acc_ref[c] = alpha * acc_ref[c] + pv
        m_ref[c] = m_new


def _attn_init(m_ref, l_ref, acc_ref):
    m_ref[...] = jnp.full(m_ref.shape, NEG_BIG, F32)
    l_ref[...] = jnp.zeros(l_ref.shape, F32)
    acc_ref[...] = jnp.zeros(acc_ref.shape, F32)


def _attn_finish(lp_ref, sg_ref, l_ref, acc_ref, lam_init):
    lp = lp_ref[...]
    lam = (jnp.exp(jnp.sum(lp[0:1] * lp[1:2], axis=-1, keepdims=True))
           - jnp.exp(jnp.sum(lp[2:3] * lp[3:4], axis=-1, keepdims=True)) + lam_init)
    l0 = jnp.sum(l_ref[0], axis=0, keepdims=True)
    l1 = jnp.sum(l_ref[1], axis=0, keepdims=True)
    o = (acc_ref[0] / l0 - lam * (acc_ref[1] / l1)).T
    return (_rms(o, sg_ref[...]) * (1.0 - lam_init)).astype(BF16)


def _attn_prompt_kernel(lp_ref, sg_ref, q_ref, k_ref, v_ref, o_ref, m_ref, l_ref, acc_ref, *, tq, tk, lam_init):
    qi = pl.program_id(2)
    _attn_init(m_ref, l_ref, acc_ref)
    q = q_ref[...]
    per = tk // tq

    def visible(start, size):
        start = pl.multiple_of(start, size)
        _attn_block(q, k_ref[pl.ds(start, size), :], v_ref[pl.ds(start, size), :], m_ref, l_ref, acc_ref, None)

    def body(kv, carry):
        visible(kv * tk, tk)
        return carry

    n_big = qi // per
    lax.fori_loop(0, n_big, body, 0)
    for r in range(1, per):
        @pl.when(qi % per >= r)
        def _():
            visible(n_big * tk + (r - 1) * tq, tq)

    start = pl.multiple_of(qi * tq, tq)
    shift = CHUNK.bit_length() - 1
    keyc = lax.shift_right_logical(lax.broadcasted_iota(jnp.int32, (tq, tq), 0), shift)
    qryc = lax.shift_right_logical(lax.broadcasted_iota(jnp.int32, (tq, tq), 1), shift)
    _attn_block(q, k_ref[pl.ds(start, tq), :], v_ref[pl.ds(start, tq), :], m_ref, l_ref, acc_ref, keyc <= qryc)
    o_ref[...] = _attn_finish(lp_ref, sg_ref, l_ref, acc_ref, lam_init)


def _attn_prompt(qb, kb, vb, ai, p, n_seq, t, tq, tk, lam_init):
    d = qb.shape[-1]
    hw = d // DIFF_HEADS
    assert t % tq == 0 and tq % CHUNK == 0 and tk % tq == 0
    q3, k3, v3 = (a.reshape(n_seq, t, d) for a in (qb, kb, vb))
    lp, sg = p["attn_lambda"], p["attn_subln"]
    kern = functools.partial(_attn_prompt_kernel, tq=tq, tk=tk, lam_init=lam_init)
    out = pl.pallas_call(
        kern,
        grid=(n_seq, DIFF_HEADS, t // tq),
        in_specs=[
            pl.BlockSpec((None,) + lp.shape[1:], lambda b, h, qi: (ai, 0, 0)),
            pl.BlockSpec((None, 1, hw), lambda b, h, qi: (ai, 0, 0)),
            pl.BlockSpec((None, tq, hw), lambda b, h, qi: (b, qi, h)),
            pl.BlockSpec((None, t, hw), lambda b, h, qi: (b, 0, h)),
            pl.BlockSpec((None, t, hw), lambda b, h, qi: (b, 0, h)),
        ],
        out_specs=pl.BlockSpec((None, tq, hw), lambda b, h, qi: (b, qi, h)),
        out_shape=jax.ShapeDtypeStruct((n_seq, t, d), BF16),
        scratch_shapes=[
            pltpu.VMEM((2, 1, tq), F32),
            pltpu.VMEM((2, SUBLANES, tq), F32),
            pltpu.VMEM((2, hw, tq), F32),
        ],
        compiler_params=_cparams(3),
        name="attn_prompt",
    )(lp, sg, q3, k3, v3)
    return out.reshape(n_seq * t, d)


def _attn_cached_kernel(lp_ref, sg_ref, q_ref, kp_ref, vp_ref, kn_ref, vn_ref, o_ref,
                        m_ref, l_ref, acc_ref, *, past, lam_init):
    n_heads = DIFF_HEADS
    rows_per_t = 2 * n_heads
    hw = q_ref.shape[-1] // n_heads
    for h in range(n_heads):
        cols = slice(h * hw, (h + 1) * hw)
        k = jnp.concatenate([kp_ref[pl.ds(2 * h + c, past, stride=rows_per_t), :] for c in range(2)], axis=-1)
        v = jnp.concatenate([vp_ref[pl.ds(c * n_heads + h, past, stride=rows_per_t), :] for c in range(2)],
                            axis=-1)
        q = q_ref[:, cols]
        _attn_init(m_ref, l_ref, acc_ref)
        _attn_block(q, k.astype(BF16), v.astype(BF16), m_ref, l_ref, acc_ref, None)
        _attn_block(q, kn_ref[:, cols], vn_ref[:, cols], m_ref, l_ref, acc_ref, None)
        o_ref[:, cols] = _attn_finish(lp_ref, sg_ref, l_ref, acc_ref, lam_init)


def _attn_cached(qb, kb, vb, k_past, v_past, ai, p, n_seq, t, lam_init):
    d = qb.shape[-1]
    hw = d // DIFF_HEADS
    n_attn, _, past, n_heads, _, dk = k_past.shape
    assert past % CHUNK == 0 and t <= CHUNK and n_heads == DIFF_HEADS and 2 * dk == hw
    q3, k3, v3 = (a.reshape(n_seq, t, d) for a in (qb, kb, vb))
    lp, sg = p["attn_lambda"], p["attn_subln"]
    k_rows = k_past.reshape(n_attn * n_seq, past * 2 * n_heads, dk)
    v_rows = v_past.reshape(n_attn, n_seq, past, n_heads, 2, dk).transpose(0, 1, 2, 4, 3, 5)
    v_rows = v_rows.reshape(n_attn * n_seq, past * 2 * n_heads, dk)
    kern = functools.partial(_attn_cached_kernel, past=past, lam_init=lam_init)
    new_spec = pl.BlockSpec((None, t, d), lambda b: (b, 0, 0))
    past_spec = pl.BlockSpec((None, past * 2 * n_heads, dk), lambda b: (ai * n_seq + b, 0, 0))
    out = pl.pallas_call(
        kern,
        grid=(n_seq,),
        in_specs=[
            pl.BlockSpec((None,) + lp.shape[1:], lambda b: (ai, 0, 0)),
            pl.BlockSpec((None, 1, hw), lambda b: (ai, 0, 0)),
            new_spec, past_spec, past_spec, new_spec, new_spec,
        ],
        out_specs=new_spec,
        out_shape=jax.ShapeDtypeStruct((n_seq, t, d), BF16),
        scratch_shapes=[
            pltpu.VMEM((2, 1, t), F32),
            pltpu.VMEM((2, SUBLANES, t), F32),
            pltpu.VMEM((2, hw, t), F32),
        ],
        compiler_params=_cparams(1),
        name="attn_cached",
    )(lp, sg, q3, k_rows, v_rows, k3, v3)
    return out.reshape(n_seq * t, d)


def _out_kernel(x_ref, mix_ref, mo_ref, w1_ref, w2_ref, o_ref):
    o_ref[...] = x_ref[...] + (_dot(mix_ref[...], w1_ref[...]) + _dot(mo_ref[...], w2_ref[...]))


def _out_proj(x2d, mix, mo, w_out, wi, bm, bn):
    m, d = x2d.shape
    k1, k2 = mix.shape[1], mo.shape[1]
    assert k1 % k2 == 0
    return pl.pallas_call(
        _out_kernel,
        grid=(m // bm, d // bn),
        in_specs=[
            pl.BlockSpec((bm, bn), lambda i, j: (i, j)),
            pl.BlockSpec((bm, k1), lambda i, j: (i, 0)),
            pl.BlockSpec((bm, k2), lambda i, j: (i, 0)),
            pl.BlockSpec((None, k1, bn), lambda i, j: (wi, 0, j)),
            pl.BlockSpec((None, k2, bn), lambda i, j: (wi, k1 // k2, j)),
        ],
        out_specs=pl.BlockSpec((bm, bn), lambda i, j: (i, j)),
        out_shape=jax.ShapeDtypeStruct((m, d), F32),
        compiler_params=_cparams(2),
        name="out_proj",
    )(x2d, mix, mo, w_out, w_out)


def _ffn_kernel(x_ref, gn_ref, wg_ref, wv_ref, cwg_ref, cwv_ref, cbg_ref, cbv_ref, stg_ref, stv_ref, wdn_ref,
                o_ref, nstg_ref, nstv_ref, hn_ref, cg_ref, cv_ref, raw_ref, act_ref,
                *, n_seg, seg_len, tiles_per_seq, row_chunk):
    i = pl.program_id(0)
    j = pl.program_id(1)
    carried = tiles_per_seq > 1

    @pl.when(j == 0)
    def _():
        x = x_ref[...]
        hn_ref[...] = _rms(x, gn_ref[...]).astype(BF16)
        o_ref[...] = x

    if carried:
        @pl.when(i % tiles_per_seq == 0)
        def _():
            cg_ref[j] = stg_ref[0]
            cv_ref[j] = stv_ref[0]

    halves = ((wg_ref, cwg_ref, cbg_ref, stg_ref, cg_ref, nstg_ref),
              (wv_ref, cwv_ref, cbv_ref, stv_ref, cv_ref, nstv_ref))
    n_prev = cwg_ref.shape[0] - 1
    n_chunks = hn_ref.shape[0] // row_chunk
    segs_per_chunk = row_chunk // seg_len if n_seg > 1 else 0
    prevs = [(h[4][j] if carried else h[3][0]) if n_seg == 1 else None for h in halves]

    def rows(c):
        return slice(c * row_chunk, (c + 1) * row_chunk)

    def up_proj(c):
        hn = hn_ref[rows(c), :]
        for hi, h in enumerate(halves):
            raw_ref[hi, rows(c), :] = _dot(hn, h[0][...])

    def conv_gate(c):
        ys = []
        for hi, (_, cw_ref, cb_ref, st_ref, _, nst_ref) in enumerate(halves):
            up = raw_ref[hi, rows(c), :]
            cw = cw_ref[...]
            cb = cb_ref[...]
            if n_seg == 1:
                ys.append(_causal_conv(up, prevs[hi], cw, cb))
                prevs[hi] = up[row_chunk - n_prev:row_chunk]
            else:
                parts = []
                for s in range(segs_per_chunk):
                    seg = up[s * seg_len:(s + 1) * seg_len]
                    parts.append(_causal_conv(seg, st_ref[c * segs_per_chunk + s], cw, cb))
                    nst_ref[c * segs_per_chunk + s] = seg[seg_len - n_prev:seg_len]
                ys.append(_cat_rows(parts))
        act_ref[rows(c), :] = (_gelu(ys[0]) * ys[1]).astype(BF16)

    def down_proj(c):
        o_ref[rows(c), :] += _dot(act_ref[rows(c), :], wdn_ref[...])

    for t in range(n_chunks + 2):
        if t < n_chunks:
            up_proj(t)
        if 0 <= t - 1 < n_chunks:
            conv_gate(t - 1)
        if 0 <= t - 2 < n_chunks:
            down_proj(t - 2)
    if n_seg == 1:
        for hi, (_, _, _, _, carry_ref, nst_ref) in enumerate(halves):
            nst_ref[0] = prevs[hi]
            if carried:
                carry_ref[j] = prevs[hi]


def _ffn(x2d, tl, li, p, state, bf, row_chunk):
    m, d = x2d.shape
    dff = p["ffn_w_down"].shape[1]
    nj = dff // bf
    ns, tps = tl.n_seg, tl.tiles_per_seq
    width = p["ffn_conv_w"].shape[1]
    assert tl.bm % row_chunk == 0 and (row_chunk % tl.seg_len == 0 if ns > 1 else True)
    kern = functools.partial(_ffn_kernel, n_seg=ns, seg_len=tl.seg_len, tiles_per_seq=tps, row_chunk=row_chunk)

    def halves(block, index_map):
        gate = pl.BlockSpec(block, lambda i, j: index_map(i, j, j))
        value = pl.BlockSpec(block, lambda i, j: index_map(i, j, nj + j))
        return [gate, value]

    nst_spec = pl.BlockSpec((None, ns, width - 1, bf), lambda i, j: (i, 0, 0, j))
    nst_shape = jax.ShapeDtypeStruct((tl.n_tiles, ns, width - 1, dff), F32)
    return pl.pallas_call(
        kern,
        grid=(tl.n_tiles, nj),
        in_specs=[
            _resident((tl.bm, d), lambda i, j: (i, 0)),
            pl.BlockSpec((None, 1, d), lambda i, j: (li, 0, 0)),
            *halves((None, d, bf), lambda i, j, c: (li, 0, c)),
            *halves((None, width, bf), lambda i, j, c: (li, 0, c)),
            *halves((None, 1, bf), lambda i, j, c: (li, 0, c)),
            *halves((None, ns, width - 1, bf), lambda i, j, c: (li, i // tps, 0, c)),
            pl.BlockSpec((None, bf, d), lambda i, j: (li, j, 0)),
        ],
        out_specs=[pl.BlockSpec((tl.bm, d), lambda i, j: (i, 0)), nst_spec, nst_spec],
        out_shape=[jax.ShapeDtypeStruct((m, d), F32), nst_shape, nst_shape],
        scratch_shapes=[
            pltpu.VMEM((tl.bm, d), BF16),
            pltpu.VMEM((nj, width - 1, bf), F32),
            pltpu.VMEM((nj, width - 1, bf), F32),
            pltpu.VMEM((2, tl.bm, bf), F32),
            pltpu.VMEM((tl.bm, bf), BF16),
        ],
        compiler_params=_cparams(2),
        name="ffn",
    )(x2d, p["norm_ffn"], p["ffn_w_up"], p["ffn_w_up"], p["ffn_conv_w"], p["ffn_conv_w"],
      p["ffn_conv_b"], p["ffn_conv_b"], state, state, p["ffn_w_down"])


def _run_trunk(x, mem_k, mem_v, lru_h, lru_conv, attn_past, ffn_conv, p, cfg):
    n_seq, t, d = x.shape
    depth = p["norm_mix"].shape[0]
    x2d = x.reshape(n_seq * t, d)
    tl = _Tiling(n_seq, t, cfg["bm"])
    tl_ffn = _Tiling(n_seq, t, cfg["bm_ffn"])
    mem_tokens = mem_k.shape[2]
    mk = mem_k.reshape(depth, n_seq, mem_tokens, -1).astype(BF16)
    mv = mem_v.reshape(depth, n_seq, mem_tokens, -1).astype(BF16)
    h0 = lru_h[:, :, None, :]
    new_h, new_lconv, new_fconv = [], [], []
    kv_all = None
    for i in range(depth):
        j = i // 2
        if i % 2 == 0:
            mix, mo, h_last, buf = _lru_in(x2d, tl, i, j, p, h0, lru_conv, mk, mv, cfg["lru_rows"])
            new_h.append(_last_tile_state(h_last, tl)[:, 0, :])
            new_lconv.append(_last_tile_state(buf, tl))
            w_out = p["lru_w_out"]
        else:
            qb, k_all, kb, v_all, vb, mo = _att_in(x2d, tl, i, j, p, mk, mv, cfg["bc"], cfg["att_rows"], kv_all)
            kv_all = (k_all, v_all)
            lam_init = 0.8 - 0.6 * math.exp(-0.3 * i)
            if attn_past is None:
                mix = _attn_prompt(qb, kb, vb, j, p, n_seq, t, cfg["tq"], cfg["tk"], lam_init)
            else:
                mix = _attn_cached(qb, kb, vb, attn_past[0], attn_past[1], j, p, n_seq, t, lam_init)
            w_out = p["attn_w_out"]
        x2d = _out_proj(x2d, mix, mo, w_out, j, cfg["bm_out"], cfg["bn_out"])
        x2d, fg, fv = _ffn(x2d, tl_ffn, i, p, ffn_conv, cfg["bf"], cfg["ffn_rows"])
        new_fconv.append(jnp.concatenate([_last_tile_state(fg, tl_ffn), _last_tile_state(fv, tl_ffn)], axis=-1))
    return x2d.reshape(n_seq, t, d), new_h, new_lconv, kv_all[0], kv_all[1], new_fconv


def _forward(x_prompt, x_sample, mem_prompt, cache_attn_k, cache_attn_v, cache_mem_k, cache_mem_v,
             state_lru_h, state_lru_conv, state_ffn_conv, p, mem_norm, mem_w_kv, mem_k_norm, cfg_p, cfg_s):
    b, t, d = x_prompt.shape
    depth = p["norm_mix"].shape[0]
    n_lru, n_attn = (depth + 1) // 2, depth // 2

    p = dict(p)
    for name in ("ffn_w_up", "ffn_w_down", "lru_w_out", "attn_w_out"):
        p[name] = p[name].astype(BF16)
    for name in ("norm_mix", "norm_ffn", "lru_conv_b", "lru_gate_a_b", "lru_gate_x_b", "lru_lambda",
                 "attn_q_norm", "attn_k_norm", "attn_subln", "mem_q_norm", "ffn_conv_b"):
        p[name] = p[name][:, None, :]

    mem_tokens = mem_prompt.shape[1]
    mk, mv = _mem_kv(mem_prompt.reshape(b * mem_tokens, d), mem_norm[:, None, :], mem_w_kv,
                     mem_k_norm[:, None, :])
    hd = mem_k_norm.shape[-1]
    p_mem_k = mk.reshape(depth, b, mem_tokens, MEM_HEADS, hd)
    p_mem_v = mv.reshape(depth, b, mem_tokens, MEM_HEADS, hd)

    zeros_h = jnp.zeros((n_lru, b, d), F32)
    zeros_lconv = jnp.zeros((n_lru, b) + state_lru_conv.shape[2:], F32)
    zeros_fconv = jnp.zeros((depth, b) + state_ffn_conv.shape[2:], F32)
    yp, ph, plc, pk, pv, pfc = _run_trunk(x_prompt, p_mem_k, p_mem_v, zeros_h, zeros_lconv, None,
                                          zeros_fconv, p, cfg_p)
    ys, sh, slc, sk, sv, sfc = _run_trunk(x_sample, cache_mem_k, cache_mem_v, state_lru_h, state_lru_conv,
                                          (cache_attn_k, cache_attn_v), state_ffn_conv, p, cfg_s)

    dk = p["attn_q_norm"].shape[-1]
    db, dt = x_sample.shape[0], x_sample.shape[1]

    def kshape(a, n, tt):
        return a.reshape(n_attn, n, tt, DIFF_HEADS, 2, dk)

    def vshape(a, n, tt):
        a = a.reshape(n_attn, n, tt, 2, DIFF_HEADS, dk)
        return a.transpose(0, 1, 2, 4, 3, 5).reshape(n_attn, n, tt, DIFF_HEADS, 2 * dk)

    return (yp, ys, jnp.stack(ph), jnp.stack(plc), kshape(pk, b, t), vshape(pv, b, t), p_mem_k, p_mem_v,
            jnp.stack(pfc), jnp.stack(sh), jnp.stack(slc), kshape(sk, db, dt), vshape(sv, db, dt),
            jnp.stack(sfc))


CFG_PROMPT = dict(bm=1024, bc=512, lru_rows=128, att_rows=256, bm_ffn=1024, bf=512, ffn_rows=512,
                  bm_out=1024, bn_out=1024, tq=512, tk=1024)
CFG_SAMPLE = dict(bm=1024, bc=512, lru_rows=128, att_rows=256, bm_ffn=1024, bf=512, ffn_rows=512,
                  bm_out=1024, bn_out=1024)


def kernel(x_prompt, x_sample, mem_prompt, cache_attn_k, cache_attn_v, cache_mem_k, cache_mem_v, state_lru_h, state_lru_conv, state_ffn_conv, norm_mix, norm_ffn, lru_w_in, lru_conv_w, lru_conv_b, lru_gate_a_w, lru_gate_a_b, lru_gate_x_w, lru_gate_x_b, lru_lambda, lru_w_out, attn_w_in, attn_q_norm, attn_k_norm, attn_lambda, attn_subln, attn_w_out, mem_norm, mem_w_kv, mem_q_norm, mem_k_norm, ffn_w_up, ffn_conv_w, ffn_conv_b, ffn_w_down):
    p = {
        "norm_mix": norm_mix, "norm_ffn": norm_ffn,
        "lru_w_in": lru_w_in, "lru_conv_w": lru_conv_w, "lru_conv_b": lru_conv_b,
        "lru_gate_a_w": lru_gate_a_w, "lru_gate_a_b": lru_gate_a_b,
        "lru_gate_x_w": lru_gate_x_w, "lru_gate_x_b": lru_gate_x_b,
        "lru_lambda": lru_lambda, "lru_w_out": lru_w_out,
        "attn_w_in": attn_w_in, "attn_q_norm": attn_q_norm, "attn_k_norm": attn_k_norm,
        "attn_lambda": attn_lambda, "attn_subln": attn_subln, "attn_w_out": attn_w_out,
        "mem_q_norm": mem_q_norm,
        "ffn_w_up": ffn_w_up, "ffn_conv_w": ffn_conv_w, "ffn_conv_b": ffn_conv_b, "ffn_w_down": ffn_w_down,
    }
    return _forward(x_prompt, x_sample, mem_prompt, cache_attn_k, cache_attn_v, cache_mem_k, cache_mem_v,
                    state_lru_h, state_lru_conv, state_ffn_conv, p, mem_norm, mem_w_kv, mem_k_norm,
                    CFG_PROMPT, CFG_SAMPLE)
```

```python
import functools
import math

import jax
import jax.numpy as jnp
from jax import lax
from jax.experimental import pallas as pl
from jax.experimental.pallas import tpu as pltpu

F32 = jnp.float32
BF16 = jnp.bfloat16

CHUNK = 64
LRU_HEADS = 8
LRU_C = 8.0
DIFF_HEADS = 8
MEM_HEADS = 4
RMS_EPS = 1e-6
NEG_BIG = -1e30
LOG2E = 1.4426950408889634
SUBLANES = 8
MEM_ATTN_ROWS = 512

V7X_VMEM_LIMIT_BYTES = 58 * 1024 * 1024


def _cparams(n_grid_axes):
    return pltpu.CompilerParams(
        dimension_semantics=("arbitrary",) * n_grid_axes,
        vmem_limit_bytes=V7X_VMEM_LIMIT_BYTES,
    )


def _dot(a, b):
    return jnp.dot(a, b, preferred_element_type=F32)


def _dot_nt(a, b):
    return lax.dot_general(a, b, (((1,), (1,)), ((), ())), preferred_element_type=F32)


def _rms(x, g):
    return x * lax.rsqrt(jnp.mean(x * x, axis=-1, keepdims=True) + RMS_EPS) * g


def _gelu(x):
    return x * (0.5 * (1.0 + jnp.tanh(0.7978845608028654 * (x + 0.044715 * (x * x * x)))))


def _cat_rows(parts):
    return parts[0] if len(parts) == 1 else jnp.concatenate(parts, axis=0)


def _delayed(seg, prev, d):
    n_prev = prev.shape[0]
    v = pltpu.roll(seg, d, 0)
    row = lax.broadcasted_iota(jnp.int32, (SUBLANES, 1), 0)
    top = v[0:SUBLANES]
    for r in range(d):
        top = jnp.where(row == r, prev[n_prev - d + r:n_prev - d + r + 1], top)
    return jnp.concatenate([top, v[SUBLANES:]], axis=0)


def _causal_conv(seg, prev, cw, cb):
    n_prev = cw.shape[0] - 1
    y = cb + _delayed(seg, prev, n_prev) * cw[0:1]
    for t in range(1, n_prev):
        y = y + _delayed(seg, prev, n_prev - t) * cw[t:t + 1]
    return y + seg * cw[n_prev:n_prev + 1]


def _scan_rows(a, b):
    n = a.shape[0]
    row = lax.broadcasted_iota(jnp.int32, (n, 1), 0)
    sh = 1
    while sh < min(SUBLANES, n):
        valid = row >= sh
        a_sh = pltpu.roll(a, sh, 0)
        b_sh = pltpu.roll(b, sh, 0)
        b = jnp.where(valid, b + a * b_sh, b)
        a = jnp.where(valid, a * a_sh, a)
        sh *= 2
    while sh < n:
        b = jnp.concatenate([b[:sh], b[sh:] + a[sh:] * b[:n - sh]], axis=0)
        a = jnp.concatenate([a[:sh], a[sh:] * a[:n - sh]], axis=0)
        sh *= 2
    return a, b


def _two_stage(n_chunks, first, second):
    for t in range(n_chunks + 1):
        if t < n_chunks:
            first(t)
        if t >= 1:
            second(t - 1)


def _mem_attn(q, gq, mk_ref, mv_ref, seg0, n_seg, seg_len):
    hd = gq.shape[-1]
    outs = []
    for h in range(q.shape[-1] // hd):
        cols = slice(h * hd, (h + 1) * hd)
        qn = _rms(q[:, cols], gq).astype(BF16)
        segs = []
        for s in range(n_seg):
            k = mk_ref[seg0 + s, :, cols]
            v = mv_ref[seg0 + s, :, cols]
            sc = _dot_nt(qn[s * seg_len:(s + 1) * seg_len], k) * (hd ** -0.5)
            p = jnp.exp(sc - jnp.max(sc, axis=-1, keepdims=True))
            pr = p / jnp.sum(p, axis=-1, keepdims=True)
            segs.append(_dot(pr.astype(BF16), v))
        outs.append(_cat_rows(segs))
    return outs[0] if len(outs) == 1 else jnp.concatenate(outs, axis=-1)


class _Tiling:
    def __init__(self, n_seq, t, bm):
        self.n_seq, self.t = n_seq, t
        if t >= bm:
            assert t % bm == 0
            self.n_seg, self.seg_len, self.tiles_per_seq = 1, bm, t // bm
        else:
            assert bm % t == 0 and n_seq % (bm // t) == 0
            self.n_seg, self.seg_len, self.tiles_per_seq = bm // t, t, 1
        self.bm = self.n_seg * self.seg_len
        self.n_tiles = n_seq * t // self.bm
        assert self.seg_len & (self.seg_len - 1) == 0 and self.seg_len >= SUBLANES


def _last_tile_state(per_tile, tl):
    rows, c = per_tile.shape[2:]
    return per_tile.reshape(tl.n_seq, tl.tiles_per_seq, rows, c)[:, -1]


def _resident(block_shape, index_map):
    return pl.BlockSpec(block_shape, index_map, pipeline_mode=pl.Buffered(1))


def _mem_kv_kernel(mem_ref, gn_ref, wk_ref, wv_ref, gk_ref, k_ref, v_ref, hn_ref):
    @pl.when(pl.program_id(1) == 0)
    def _():
        hn_ref[...] = _rms(mem_ref[...], gn_ref[...]).astype(BF16)

    hn = hn_ref[...]
    k_ref[...] = _rms(_dot(hn, wk_ref[...].astype(BF16)), gk_ref[...])
    v_ref[...] = _dot(hn, wv_ref[...].astype(BF16))


def _mem_kv(mem2d, mem_norm, w_kv, mem_k_norm):
    depth, d, two_w = w_kv.shape
    mem_w = two_w // 2
    hd = mem_w // MEM_HEADS
    m = mem2d.shape[0]
    return pl.pallas_call(
        _mem_kv_kernel,
        grid=(depth, MEM_HEADS),
        in_specs=[
            pl.BlockSpec((m, d), lambda l, j: (0, 0)),
            pl.BlockSpec((None, 1, d), lambda l, j: (l, 0, 0)),
            pl.BlockSpec((None, d, hd), lambda l, j: (l, 0, j)),
            pl.BlockSpec((None, d, hd), lambda l, j: (l, 0, MEM_HEADS + j)),
            pl.BlockSpec((None, 1, hd), lambda l, j: (l, 0, 0)),
        ],
        out_specs=[
            pl.BlockSpec((None, m, hd), lambda l, j: (l, 0, j)),
            pl.BlockSpec((None, m, hd), lambda l, j: (l, 0, j)),
        ],
        out_shape=[jax.ShapeDtypeStruct((depth, m, mem_w), F32)] * 2,
        scratch_shapes=[pltpu.VMEM((m, d), BF16)],
        compiler_params=_cparams(2),
        name="mem_kv",
    )(mem2d, mem_norm, w_kv, w_kv, mem_k_norm)


def _lru_in_kernel(x_ref, gn_ref, wg_ref, wx_ref, cw_ref, cb_ref, wa_ref, wi_ref, ba_ref, bi_ref,
                   lam_ref, h0_ref, c0_ref, gq_ref, mk_ref, mv_ref,
                   mix_ref, mo_ref, hn_out_ref, cn_out_ref,
                   hn_ref, hc_ref, cc_ref, raw_ref, *, n_seg, seg_len, tiles_per_seq, row_chunk):
    i = pl.program_id(0)
    j = pl.program_id(1)
    carried = tiles_per_seq > 1

    @pl.when(j == 0)
    def _():
        hn_ref[...] = _rms(x_ref[...], gn_ref[...]).astype(BF16)

    @pl.when(j < LRU_HEADS)
    def _():
        if carried:
            @pl.when(i % tiles_per_seq == 0)
            def _():
                cc_ref[j] = c0_ref[0]
                hc_ref[j] = h0_ref[0]

        cw = cw_ref[...]
        cb = cb_ref[...]
        wg, wx, wa, wi = (r[...].astype(BF16) for r in (wg_ref, wx_ref, wa_ref, wi_ref))
        n_prev = cw.shape[0] - 1
        n_chunks = hn_ref.shape[0] // row_chunk
        segs_per_chunk = row_chunk // seg_len if n_seg > 1 else 0
        whole = n_seg == 1
        state = {"conv": (cc_ref[j] if carried else c0_ref[0]) if whole else None,
                 "h": (hc_ref[j] if carried else h0_ref[0]) if whole else None}
        log_lam = jax.nn.log_sigmoid(lam_ref[...])

        def rows(c):
            return slice(c * row_chunk, (c + 1) * row_chunk)

        def in_proj(c):
            hn = hn_ref[rows(c), :]
            raw_ref[0, rows(c), :] = _dot(hn, wg)
            raw_ref[1, rows(c), :] = _dot(hn, wx)

        def conv_gates(c):
            xr = raw_ref[1, rows(c), :]
            if whole:
                xc = _causal_conv(xr, state["conv"], cw, cb)
                state["conv"] = xr[row_chunk - n_prev:row_chunk]
            else:
                parts = []
                for s in range(segs_per_chunk):
                    sidx = c * segs_per_chunk + s
                    seg = xr[s * seg_len:(s + 1) * seg_len]
                    parts.append(_causal_conv(seg, c0_ref[sidx], cw, cb))
                    cn_out_ref[sidx] = seg[seg_len - n_prev:seg_len]
                xc = _cat_rows(parts)
            xcb = xc.astype(BF16)
            raw_ref[1, rows(c), :] = xc
            raw_ref[2, rows(c), :] = _dot(xcb, wa)
            raw_ref[3, rows(c), :] = _dot(xcb, wi)

        def recur(c):
            xc = raw_ref[1, rows(c), :]
            r = jax.nn.sigmoid(raw_ref[2, rows(c), :] + ba_ref[...])
            ig = jax.nn.sigmoid(raw_ref[3, rows(c), :] + bi_ref[...])
            log_a = (LRU_C * r) * log_lam
            a = jnp.exp(log_a)
            one_minus_a2 = -jnp.tanh(log_a) * (a * a + 1.0)
            gated = jnp.sqrt(one_minus_a2) * (ig * xc)
            if whole:
                cum_a, cum_b = _scan_rows(a, gated)
                hs = cum_b + cum_a * state["h"]
                state["h"] = hs[row_chunk - 1:row_chunk]
            else:
                parts = []
                for s in range(segs_per_chunk):
                    sidx = c * segs_per_chunk + s
                    sl = slice(s * seg_len, (s + 1) * seg_len)
                    cum_a, cum_b = _scan_rows(a[sl], gated[sl])
                    seg_hs = cum_b + cum_a * h0_ref[sidx]
                    hn_out_ref[sidx] = seg_hs[seg_len - 1:seg_len]
                    parts.append(seg_hs)
                hs = _cat_rows(parts)
            mix_ref[rows(c), :] = (_gelu(raw_ref[0, rows(c), :]) * hs).astype(BF16)

        for t in range(n_chunks + 2):
            if t < n_chunks:
                in_proj(t)
            if 0 <= t - 1 < n_chunks:
                conv_gates(t - 1)
            if 0 <= t - 2 < n_chunks:
                recur(t - 2)
        if whole:
            cn_out_ref[0] = state["conv"]
            hn_out_ref[0] = state["h"]
            if carried:
                cc_ref[j] = state["conv"]
                hc_ref[j] = state["h"]

    @pl.when(j >= LRU_HEADS)
    def _():
        chunk = max(row_chunk, min(hn_ref.shape[0], MEM_ATTN_ROWS))
        n_chunks = hn_ref.shape[0] // chunk
        segs = (chunk // seg_len, seg_len) if n_seg > 1 else (1, chunk)

        wq = wx_ref[...].astype(BF16)

        def rows(c):
            return slice(c * chunk, (c + 1) * chunk)

        def project(c):
            raw_ref[0, rows(c), :] = _dot(hn_ref[rows(c), :], wq)

        def attend(c):
            seg0 = c * segs[0] if n_seg > 1 else 0
            mo_ref[rows(c), :] = _mem_attn(raw_ref[0, rows(c), :], gq_ref[...], mk_ref, mv_ref,
                                           seg0, *segs).astype(BF16)

        _two_stage(n_chunks, project, attend)


def _lru_in(x2d, tl, li, mi, p, h0, c0, mk, mv, row_chunk):
    m, d = x2d.shape
    hw = d // LRU_HEADS
    n_steps = LRU_HEADS + MEM_HEADS
    ns, tps = tl.n_seg, tl.tiles_per_seq
    width = p["lru_conv_w"].shape[1]
    last = LRU_HEADS - 1

    def hcol(j):
        return jnp.minimum(j, last)

    def mcol(j):
        return jnp.maximum(j - LRU_HEADS, 0)

    vec = pl.BlockSpec((None, 1, hw), lambda i, j: (mi, 0, hcol(j)))
    gate_w = pl.BlockSpec((None, None, hw, hw), lambda i, j: (mi, hcol(j), 0, 0))
    mem = pl.BlockSpec((None, ns, mk.shape[2], hw), lambda i, j: (li, i // tps, 0, mcol(j)))
    assert tl.bm % row_chunk == 0 and (row_chunk % tl.seg_len == 0 if ns > 1 else True)
    kern = functools.partial(_lru_in_kernel, n_seg=ns, seg_len=tl.seg_len, tiles_per_seq=tps,
                             row_chunk=row_chunk)
    return pl.pallas_call(
        kern,
        grid=(tl.n_tiles, n_steps),
        in_specs=[
            _resident((tl.bm, d), lambda i, j: (i, 0)),
            pl.BlockSpec((None, 1, d), lambda i, j: (li, 0, 0)),
            pl.BlockSpec((None, d, hw), lambda i, j: (mi, 0, hcol(j))),
            pl.BlockSpec((None, d, hw), lambda i, j: (mi, 0, LRU_HEADS + j)),
            pl.BlockSpec((None, width, hw), lambda i, j: (mi, 0, hcol(j))),
            vec, gate_w, gate_w, vec, vec, vec,
            pl.BlockSpec((None, ns, 1, hw), lambda i, j: (mi, i // tps, 0, hcol(j))),
            pl.BlockSpec((None, ns, width - 1, hw), lambda i, j: (mi, i // tps, 0, hcol(j))),
            pl.BlockSpec((None, 1, hw), lambda i, j: (li, 0, 0)),
            mem, mem,
        ],
        out_specs=[
            pl.BlockSpec((tl.bm, hw), lambda i, j: (i, hcol(j))),
            pl.BlockSpec((tl.bm, hw), lambda i, j: (i, mcol(j))),
            pl.BlockSpec((None, ns, 1, hw), lambda i, j: (i, 0, 0, hcol(j))),
            pl.BlockSpec((None, ns, width - 1, hw), lambda i, j: (i, 0, 0, hcol(j))),
        ],
        out_shape=[
            jax.ShapeDtypeStruct((m, d), BF16),
            jax.ShapeDtypeStruct((m, MEM_HEADS * hw), BF16),
            jax.ShapeDtypeStruct((tl.n_tiles, ns, 1, d), F32),
            jax.ShapeDtypeStruct((tl.n_tiles, ns, width - 1, d), F32),
        ],
        scratch_shapes=[
            pltpu.VMEM((tl.bm, d), BF16),
            pltpu.VMEM((LRU_HEADS, 1, hw), F32),
            pltpu.VMEM((LRU_HEADS, width - 1, hw), F32),
            pltpu.VMEM((4, tl.bm, hw), F32),
        ],
        compiler_params=_cparams(2),
        name="lru_in",
    )(x2d, p["norm_mix"], p["lru_w_in"], p["lru_w_in"], p["lru_conv_w"], p["lru_conv_b"],
      p["lru_gate_a_w"], p["lru_gate_x_w"], p["lru_gate_a_b"], p["lru_gate_x_b"], p["lru_lambda"],
      h0, c0, p["mem_q_norm"], mk, mv)


def _att_in_kernel(*refs, n_seg, seg_len, nq, row_chunk, aliased):
    if aliased:
        refs = refs[:8] + refs[10:]
    (x_ref, gn_ref, w_ref, qg_ref, kg_ref, gq_ref, mk_ref, mv_ref,
     qb_ref, kf_ref, kb_ref, vf_ref, vb_ref, mo_ref, hn_ref, z_ref) = refs
    j = pl.program_id(1)

    @pl.when(j == 0)
    def _():
        hn_ref[...] = _rms(x_ref[...], gn_ref[...]).astype(BF16)

    dk = qg_ref.shape[-1]
    n_groups = z_ref.shape[-1] // dk
    rows_per_t = hn_ref.shape[-1] // dk
    n_heads = rows_per_t // 2
    n_chunks = hn_ref.shape[0] // row_chunk
    w = w_ref[...]

    def rows(c):
        return slice(c * row_chunk, (c + 1) * row_chunk)

    def project(c):
        z_ref[rows(c), :] = _dot(hn_ref[rows(c), :], w)

    def queries(c):
        qscale = dk ** -0.5 * LOG2E
        for g in range(n_groups):
            sl = slice(g * dk, (g + 1) * dk)
            qb_ref[rows(c), sl] = (_rms(z_ref[rows(c), sl], qg_ref[...]) * qscale).astype(BF16)

    def keys(c):
        for g in range(n_groups):
            sl = slice(g * dk, (g + 1) * dk)
            kn = _rms(z_ref[rows(c), sl], kg_ref[...])
            group = (j - nq) * n_groups + g
            kf_ref[pl.ds(c * row_chunk * rows_per_t + group, row_chunk, stride=rows_per_t), :] = kn
            kb_ref[rows(c), sl] = kn.astype(BF16)

    def values(c):
        z = z_ref[rows(c), :]
        vb_ref[rows(c), :] = z.astype(BF16)
        for g in range(n_groups):
            group = (j - 2 * nq) * n_groups + g
            row = (group & 1) * n_heads + (group >> 1)
            vf_ref[pl.ds(c * row_chunk * rows_per_t + row, row_chunk, stride=rows_per_t), :] = (
                z[:, g * dk:(g + 1) * dk])

    mem_chunk = max(row_chunk, min(hn_ref.shape[0], MEM_ATTN_ROWS))
    mem_segs = (mem_chunk // seg_len, seg_len) if n_seg > 1 else (1, mem_chunk)

    def mem_rows(c):
        return slice(c * mem_chunk, (c + 1) * mem_chunk)

    def mem_project(c):
        z_ref[mem_rows(c), :] = _dot(hn_ref[mem_rows(c), :], w)

    def memory(c):
        seg0 = c * mem_segs[0] if n_seg > 1 else 0
        mo_ref[mem_rows(c), :] = _mem_attn(z_ref[mem_rows(c), :], gq_ref[...], mk_ref, mv_ref,
                                           seg0, *mem_segs).astype(BF16)

    @pl.when(j < nq)
    def _():
        _two_stage(n_chunks, project, queries)

    @pl.when((j >= nq) & (j < 2 * nq))
    def _():
        _two_stage(n_chunks, project, keys)

    @pl.when((j >= 2 * nq) & (j < 3 * nq))
    def _():
        _two_stage(n_chunks, project, values)

    @pl.when(j >= 3 * nq)
    def _():
        _two_stage(hn_ref.shape[0] // mem_chunk, mem_project, memory)


def _att_in(x2d, tl, li, ai, p, mk, mv, bc, row_chunk, kv_all):
    m, d = x2d.shape
    mem_w = mk.shape[-1]
    nq, nm = d // bc, mem_w // bc
    n_steps = 3 * nq + nm
    ns, tps = tl.n_seg, tl.tiles_per_seq
    dk = p["attn_q_norm"].shape[-1]
    hd = p["mem_q_norm"].shape[-1]

    def col(lo, n):
        return lambda i, j: (i, jnp.clip(j - lo, 0, n - 1))

    mem = _resident((None, ns, mk.shape[2], bc),
                    lambda i, j: (li, i // tps, 0, jnp.clip(j - 3 * nq, 0, nm - 1)))
    assert tl.bm % row_chunk == 0 and (row_chunk % tl.seg_len == 0 if ns > 1 else True)
    n_attn = p["attn_w_in"].shape[0]
    rows_per_t = d // dk
    aliased = kv_all is not None
    kern = functools.partial(_att_in_kernel, n_seg=ns, seg_len=tl.seg_len, nq=nq, row_chunk=row_chunk,
                             aliased=aliased)
    kv_spec = _resident((None, tl.bm * rows_per_t, dk), lambda i, j: (ai, i, 0))
    kv_shape = jax.ShapeDtypeStruct((n_attn, m * rows_per_t, dk), F32)
    extra_in = [pl.BlockSpec(memory_space=pl.ANY)] * 2 if aliased else []
    return pl.pallas_call(
        kern,
        grid=(tl.n_tiles, n_steps),
        in_specs=[
            _resident((tl.bm, d), lambda i, j: (i, 0)),
            pl.BlockSpec((None, 1, d), lambda i, j: (li, 0, 0)),
            pl.BlockSpec((None, d, bc), lambda i, j: (ai, 0, j)),
            pl.BlockSpec((None, 1, dk), lambda i, j: (ai, 0, 0)),
            pl.BlockSpec((None, 1, dk), lambda i, j: (ai, 0, 0)),
            pl.BlockSpec((None, 1, hd), lambda i, j: (li, 0, 0)),
            mem, mem, *extra_in,
        ],
        out_specs=[
            pl.BlockSpec((tl.bm, bc), col(0, nq)),
            kv_spec,
            pl.BlockSpec((tl.bm, bc), col(nq, nq)),
            kv_spec,
            pl.BlockSpec((tl.bm, bc), col(2 * nq, nq)),
            pl.BlockSpec((tl.bm, bc), col(3 * nq, nm)),
        ],
        out_shape=[
            jax.ShapeDtypeStruct((m, d), BF16),
            kv_shape,
            jax.ShapeDtypeStruct((m, d), BF16),
            kv_shape,
            jax.ShapeDtypeStruct((m, d), BF16),
            jax.ShapeDtypeStruct((m, mem_w), BF16),
        ],
        input_output_aliases={8: 1, 9: 3} if aliased else {},
        scratch_shapes=[pltpu.VMEM((tl.bm, d), BF16), pltpu.VMEM((tl.bm, bc), F32)],
        compiler_params=_cparams(2),
        name="att_in",
    )(x2d, p["norm_mix"], p["attn_w_in"], p["attn_q_norm"], p["attn_k_norm"], p["mem_q_norm"], mk, mv,
      *(kv_all if aliased else ()))


def _attn_block(q, k, v, m_ref, l_ref, acc_ref, mask):
    dk = q.shape[-1] // 2
    tk = k.shape[0]
    scores = [_dot_nt(k[:, c * dk:(c + 1) * dk], q[:, c * dk:(c + 1) * dk]) for c in range(2)]
    for c in range(2):
        s = scores[c]
        if mask is not None:
            s = jnp.where(mask, s, NEG_BIG)
        m_old = m_ref[c]
        m_new = jnp.maximum(m_old, jnp.max(s, axis=0, keepdims=True))
        alpha = jnp.exp2(m_old - m_new)
        p = jnp.exp2(s - m_new)
        l_ref[c] = alpha * l_ref[c] + jnp.sum(p.reshape(tk // SUBLANES, SUBLANES, p.shape[-1]), axis=0)
        pv = lax.dot_general(v, p.astype(BF16), (((0,), (0,)), ((), ())), preferred_element_type=F32)
        acc_ref[c] = alpha * acc_ref[c] + pv
        m_ref[c] = m_new


def _attn_init(m_ref, l_ref, acc_ref):
    m_ref[...] = jnp.full(m_ref.shape, NEG_BIG, F32)
    l_ref[...] = jnp.zeros(l_ref.shape, F32)
    acc_ref[...] = jnp.zeros(acc_ref.shape, F32)


def _attn_finish(lp_ref, sg_ref, l_ref, acc_ref, lam_init):
    lp = lp_ref[...]
    lam = (jnp.exp(jnp.sum(lp[0:1] * lp[1:2], axis=-1, keepdims=True))
           - jnp.exp(jnp.sum(lp[2:3] * lp[3:4], axis=-1, keepdims=True)) + lam_init)
    l0 = jnp.sum(l_ref[0], axis=0, keepdims=True)
    l1 = jnp.sum(l_ref[1], axis=0, keepdims=True)
    o = (acc_ref[0] / l0 - lam * (acc_ref[1] / l1)).T
    return (_rms(o, sg_ref[...]) * (1.0 - lam_init)).astype(BF16)


def _attn_prompt_kernel(lp_ref, sg_ref, q_ref, k_ref, v_ref, o_ref, m_ref, l_ref, acc_ref, *, tq, tk, lam_init):
    qi = pl.program_id(2)
    _attn_init(m_ref, l_ref, acc_ref)
    q = q_ref[...]
    per = tk // tq

    def visible(start, size):
        start = pl.multiple_of(start, size)
        _attn_block(q, k_ref[pl.ds(start, size), :], v_ref[pl.ds(start, size), :], m_ref, l_ref, acc_ref, None)

    def body(kv, carry):
        visible(kv * tk, tk)
        return carry

    n_big = qi // per
    lax.fori_loop(0, n_big, body, 0)
    for r in range(1, per):
        @pl.when(qi % per >= r)
        def _():
            visible(n_big * tk + (r - 1) * tq, tq)

    start = pl.multiple_of(qi * tq, tq)
    shift = CHUNK.bit_length() - 1
    keyc = lax.shift_right_logical(lax.broadcasted_iota(jnp.int32, (tq, tq), 0), shift)
    qryc = lax.shift_right_logical(lax.broadcasted_iota(jnp.int32, (tq, tq), 1), shift)
    _attn_block(q, k_ref[pl.ds(start, tq), :], v_ref[pl.ds(start, tq), :], m_ref, l_ref, acc_ref, keyc <= qryc)
    o_ref[...] = _attn_finish(lp_ref, sg_ref, l_ref, acc_ref, lam_init)


def _attn_prompt(qb, kb, vb, ai, p, n_seq, t, tq, tk, lam_init):
    d = qb.shape[-1]
    hw = d // DIFF_HEADS
    assert t % tq == 0 and tq % CHUNK == 0 and tk % tq == 0
    q3, k3, v3 = (a.reshape(n_seq, t, d) for a in (qb, kb, vb))
    lp, sg = p["attn_lambda"], p["attn_subln"]
    kern = functools.partial(_attn_prompt_kernel, tq=tq, tk=tk, lam_init=lam_init)
    out = pl.pallas_call(
        kern,
        grid=(n_seq, DIFF_HEADS, t // tq),
        in_specs=[
            pl.BlockSpec((None,) + lp.shape[1:], lambda b, h, qi: (ai, 0, 0)),
            pl.BlockSpec((None, 1, hw), lambda b, h, qi: (ai, 0, 0)),
            pl.BlockSpec((None, tq, hw), lambda b, h, qi: (b, qi, h)),
            pl.BlockSpec((None, t, hw), lambda b, h, qi: (b, 0, h)),
            pl.BlockSpec((None, t, hw), lambda b, h, qi: (b, 0, h)),
        ],
        out_specs=pl.BlockSpec((None, tq, hw), lambda b, h, qi: (b, qi, h)),
        out_shape=jax.ShapeDtypeStruct((n_seq, t, d), BF16),
        scratch_shapes=[
            pltpu.VMEM((2, 1, tq), F32),
            pltpu.VMEM((2, SUBLANES, tq), F32),
            pltpu.VMEM((2, hw, tq), F32),
        ],
        compiler_params=_cparams(3),
        name="attn_prompt",
    )(lp, sg, q3, k3, v3)
    return out.reshape(n_seq * t, d)


def _attn_cached_kernel(lp_ref, sg_ref, q_ref, kp_ref, vp_ref, kn_ref, vn_ref, o_ref,
                        m_ref, l_ref, acc_ref, *, past, lam_init):
    n_heads = DIFF_HEADS
    rows_per_t = 2 * n_heads
    hw = q_ref.shape[-1] // n_heads
    for h in range(n_heads):
        cols = slice(h * hw, (h + 1) * hw)
        k = jnp.concatenate([kp_ref[pl.ds(2 * h + c, past, stride=rows_per_t), :] for c in range(2)], axis=-1)
        v = jnp.concatenate([vp_ref[pl.ds(c * n_heads + h, past, stride=rows_per_t), :] for c in range(2)],
                            axis=-1)
        q = q_ref[:, cols]
        _attn_init(m_ref, l_ref, acc_ref)
        _attn_block(q, k.astype(BF16), v.astype(BF16), m_ref, l_ref, acc_ref, None)
        _attn_block(q, kn_ref[:, cols], vn_ref[:, cols], m_ref, l_ref, acc_ref, None)
        o_ref[:, cols] = _attn_finish(lp_ref, sg_ref, l_ref, acc_ref, lam_init)


def _attn_cached(qb, kb, vb, k_past, v_past, ai, p, n_seq, t, lam_init):
    d = qb.shape[-1]
    hw = d // DIFF_HEADS
    n_attn, _, past, n_heads, _, dk = k_past.shape
    assert past % CHUNK == 0 and t <= CHUNK and n_heads == DIFF_HEADS and 2 * dk == hw
    q3, k3, v3 = (a.reshape(n_seq, t, d) for a in (qb, kb, vb))
    lp, sg = p["attn_lambda"], p["attn_subln"]
    k_rows = k_past.reshape(n_attn * n_seq, past * 2 * n_heads, dk)
    v_rows = v_past.reshape(n_attn, n_seq, past, n_heads, 2, dk).transpose(0, 1, 2, 4, 3, 5)
    v_rows = v_rows.reshape(n_attn * n_seq, past * 2 * n_heads, dk)
    kern = functools.partial(_attn_cached_kernel, past=past, lam_init=lam_init)
    new_spec = pl.BlockSpec((None, t, d), lambda b: (b, 0, 0))
    past_spec = pl.BlockSpec((None, past * 2 * n_heads, dk), lambda b: (ai * n_seq + b, 0, 0))
    out = pl.pallas_call(
        kern,
        grid=(n_seq,),
        in_specs=[
            pl.BlockSpec((None,) + lp.shape[1:], lambda b: (ai, 0, 0)),
            pl.BlockSpec((None, 1, hw), lambda b: (ai, 0, 0)),
            new_spec, past_spec, past_spec, new_spec, new_spec,
        ],
        out_specs=new_spec,
        out_shape=jax.ShapeDtypeStruct((n_seq, t, d), BF16),
        scratch_shapes=[
            pltpu.VMEM((2, 1, t), F32),
            pltpu.VMEM((2, SUBLANES, t), F32),
            pltpu.VMEM((2, hw, t), F32),
        ],
        compiler_params=_cparams(1),
        name="attn_cached",
    )(lp, sg, q3, k_rows, v_rows, k3, v3)
    return out.reshape(n_seq * t, d)


def _out_kernel(x_ref, mix_ref, mo_ref, w1_ref, w2_ref, o_ref):
    o_ref[...] = x_ref[...] + (_dot(mix_ref[...], w1_ref[...]) + _dot(mo_ref[...], w2_ref[...]))


def _out_proj(x2d, mix, mo, w_out, wi, bm, bn):
    m, d = x2d.shape
    k1, k2 = mix.shape[1], mo.shape[1]
    assert k1 % k2 == 0
    return pl.pallas_call(
        _out_kernel,
        grid=(m // bm, d // bn),
        in_specs=[
            pl.BlockSpec((bm, bn), lambda i, j: (i, j)),
            pl.BlockSpec((bm, k1), lambda i, j: (i, 0)),
            pl.BlockSpec((bm, k2), lambda i, j: (i, 0)),
            pl.BlockSpec((None, k1, bn), lambda i, j: (wi, 0, j)),
            pl.BlockSpec((None, k2, bn), lambda i, j: (wi, k1 // k2, j)),
        ],
        out_specs=pl.BlockSpec((bm, bn), lambda i, j: (i, j)),
        out_shape=jax.ShapeDtypeStruct((m, d), F32),
        compiler_params=_cparams(2),
        name="out_proj",
    )(x2d, mix, mo, w_out, w_out)


def _ffn_kernel(x_ref, gn_ref, wg_ref, wv_ref, cwg_ref, cwv_ref, cbg_ref, cbv_ref, stg_ref, stv_ref, wdn_ref,
                o_ref, nstg_ref, nstv_ref, hn_ref, cg_ref, cv_ref, raw_ref, act_ref,
                *, n_seg, seg_len, tiles_per_seq, row_chunk):
    i = pl.program_id(0)
    j = pl.program_id(1)
    carried = tiles_per_seq > 1

    @pl.when(j == 0)
    def _():
        x = x_ref[...]
        hn_ref[...] = _rms(x, gn_ref[...]).astype(BF16)
        o_ref[...] = x

    if carried:
        @pl.when(i % tiles_per_seq == 0)
        def _():
            cg_ref[j] = stg_ref[0]
            cv_ref[j] = stv_ref[0]

    halves = ((wg_ref, cwg_ref, cbg_ref, stg_ref, cg_ref, nstg_ref),
              (wv_ref, cwv_ref, cbv_ref, stv_ref, cv_ref, nstv_ref))
    n_prev = cwg_ref.shape[0] - 1
    n_chunks = hn_ref.shape[0] // row_chunk
    segs_per_chunk = row_chunk // seg_len if n_seg > 1 else 0
    prevs = [(h[4][j] if carried else h[3][0]) if n_seg == 1 else None for h in halves]

    def rows(c):
        return slice(c * row_chunk, (c + 1) * row_chunk)

    def up_proj(c):
        hn = hn_ref[rows(c), :]
        for hi, h in enumerate(halves):
            raw_ref[hi, rows(c), :] = _dot(hn, h[0][...])

    def conv_gate(c):
        ys = []
        for hi, (_, cw_ref, cb_ref, st_ref, _, nst_ref) in enumerate(halves):
            up = raw_ref[hi, rows(c), :]
            cw = cw_ref[...]
            cb = cb_ref[...]
            if n_seg == 1:
                ys.append(_causal_conv(up, prevs[hi], cw, cb))
                prevs[hi] = up[row_chunk - n_prev:row_chunk]
            else:
                parts = []
                for s in range(segs_per_chunk):
                    seg = up[s * seg_len:(s + 1) * seg_len]
                    parts.append(_causal_conv(seg, st_ref[c * segs_per_chunk + s], cw, cb))
                    nst_ref[c * segs_per_chunk + s] = seg[seg_len - n_prev:seg_len]
                ys.append(_cat_rows(parts))
        act_ref[rows(c), :] = (_gelu(ys[0]) * ys[1]).astype(BF16)

    def down_proj(c):
        o_ref[rows(c), :] += _dot(act_ref[rows(c), :], wdn_ref[...])

    for t in range(n_chunks + 2):
        if t < n_chunks:
            up_proj(t)
        if 0 <= t - 1 < n_chunks:
            conv_gate(t - 1)
        if 0 <= t - 2 < n_chunks:
            down_proj(t - 2)
    if n_seg == 1:
        for hi, (_, _, _, _, carry_ref, nst_ref) in enumerate(halves):
            nst_ref[0] = prevs[hi]
            if carried:
                carry_ref[j] = prevs[hi]


def _ffn(x2d, tl, li, p, state, bf, row_chunk):
    m, d = x2d.shape
    dff = p["ffn_w_down"].shape[1]
    nj = dff // bf
    ns, tps = tl.n_seg, tl.tiles_per_seq
    width = p["ffn_conv_w"].shape[1]
    assert tl.bm % row_chunk == 0 and (row_chunk % tl.seg_len == 0 if ns > 1 else True)
    kern = functools.partial(_ffn_kernel, n_seg=ns, seg_len=tl.seg_len, tiles_per_seq=tps, row_chunk=row_chunk)

    def halves(block, index_map):
        gate = pl.BlockSpec(block, lambda i, j: index_map(i, j, j))
        value = pl.BlockSpec(block, lambda i, j: index_map(i, j, nj + j))
        return [gate, value]

    nst_spec = pl.BlockSpec((None, ns, width - 1, bf), lambda i, j: (i, 0, 0, j))
    nst_shape = jax.ShapeDtypeStruct((tl.n_tiles, ns, width - 1, dff), F32)
    return pl.pallas_call(
        kern,
        grid=(tl.n_tiles, nj),
        in_specs=[
            _resident((tl.bm, d), lambda i, j: (i, 0)),
            pl.BlockSpec((None, 1, d), lambda i, j: (li, 0, 0)),
            *halves((None, d, bf), lambda i, j, c: (li, 0, c)),
            *halves((None, width, bf), lambda i, j, c: (li, 0, c)),
            *halves((None, 1, bf), lambda i, j, c: (li, 0, c)),
            *halves((None, ns, width - 1, bf), lambda i, j, c: (li, i // tps, 0, c)),
            pl.BlockSpec((None, bf, d), lambda i, j: (li, j, 0)),
        ],
        out_specs=[pl.BlockSpec((tl.bm, d), lambda i, j: (i, 0)), nst_spec, nst_spec],
        out_shape=[jax.ShapeDtypeStruct((m, d), F32), nst_shape, nst_shape],
        scratch_shapes=[
            pltpu.VMEM((tl.bm, d), BF16),
            pltpu.VMEM((nj, width - 1, bf), F32),
            pltpu.VMEM((nj, width - 1, bf), F32),
            pltpu.VMEM((2, tl.bm, bf), F32),
            pltpu.VMEM((tl.bm, bf), BF16),
        ],
        compiler_params=_cparams(2),
        name="ffn",
    )(x2d, p["norm_ffn"], p["ffn_w_up"], p["ffn_w_up"], p["ffn_conv_w"], p["ffn_conv_w"],
      p["ffn_conv_b"], p["ffn_conv_b"], state, state, p["ffn_w_down"])


def _run_trunk(x, mem_k, mem_v, lru_h, lru_conv, attn_past, ffn_conv, p, cfg):
    n_seq, t, d = x.shape
    depth = p["norm_mix"].shape[0]
    x2d = x.reshape(n_seq * t, d)
    tl = _Tiling(n_seq, t, cfg["bm"])
    tl_ffn = _Tiling(n_seq, t, cfg["bm_ffn"])
    mem_tokens = mem_k.shape[2]
    mk = mem_k.reshape(depth, n_seq, mem_tokens, -1).astype(BF16)
    mv = mem_v.reshape(depth, n_seq, mem_tokens, -1).astype(BF16)
    h0 = lru_h[:, :, None, :]
    new_h, new_lconv, new_fconv = [], [], []
    kv_all = None
    for i in range(depth):
        j = i // 2
        if i % 2 == 0:
            mix, mo, h_last, buf = _lru_in(x2d, tl, i, j, p, h0, lru_conv, mk, mv, cfg["lru_rows"])
            new_h.append(_last_tile_state(h_last, tl)[:, 0, :])
            new_lconv.append(_last_tile_state(buf, tl))
            w_out = p["lru_w_out"]
        else:
            qb, k_all, kb, v_all, vb, mo = _att_in(x2d, tl, i, j, p, mk, mv, cfg["bc"], cfg["att_rows"], kv_all)
            kv_all = (k_all, v_all)
            lam_init = 0.8 - 0.6 * math.exp(-0.3 * i)
            if attn_past is None:
                mix = _attn_prompt(qb, kb, vb, j, p, n_seq, t, cfg["tq"], cfg["tk"], lam_init)
            else:
                mix = _attn_cached(qb, kb, vb, attn_past[0], attn_past[1], j, p, n_seq, t, lam_init)
            w_out = p["attn_w_out"]
        x2d = _out_proj(x2d, mix, mo, w_out, j, cfg["bm_out"], cfg["bn_out"])
        x2d, fg, fv = _ffn(x2d, tl_ffn, i, p, ffn_conv, cfg["bf"], cfg["ffn_rows"])
        new_fconv.append(jnp.concatenate([_last_tile_state(fg, tl_ffn), _last_tile_state(fv, tl_ffn)], axis=-1))
    return x2d.reshape(n_seq, t, d), new_h, new_lconv, kv_all[0], kv_all[1], new_fconv


def _forward(x_prompt, x_sample, mem_prompt, cache_attn_k, cache_attn_v, cache_mem_k, cache_mem_v,
             state_lru_h, state_lru_conv, state_ffn_conv, p, mem_norm, mem_w_kv, mem_k_norm, cfg_p, cfg_s):
    b, t, d = x_prompt.shape
    depth = p["norm_mix"].shape[0]
    n_lru, n_attn = (depth + 1) // 2, depth // 2

    p = dict(p)
    for name in ("ffn_w_up", "ffn_w_down", "lru_w_out", "attn_w_out", "attn_w_in"):
        p[name] = p[name].astype(BF16)
    for name in ("norm_mix", "norm_ffn", "lru_conv_b", "lru_gate_a_b", "lru_gate_x_b", "lru_lambda",
                 "attn_q_norm", "attn_k_norm", "attn_subln", "mem_q_norm", "ffn_conv_b"):
        p[name] = p[name][:, None, :]

    mem_tokens = mem_prompt.shape[1]
    mk, mv = _mem_kv(mem_prompt.reshape(b * mem_tokens, d), mem_norm[:, None, :], mem_w_kv,
                     mem_k_norm[:, None, :])
    hd = mem_k_norm.shape[-1]
    p_mem_k = mk.reshape(depth, b, mem_tokens, MEM_HEADS, hd)
    p_mem_v = mv.reshape(depth, b, mem_tokens, MEM_HEADS, hd)

    zeros_h = jnp.zeros((n_lru, b, d), F32)
    zeros_lconv = jnp.zeros((n_lru, b) + state_lru_conv.shape[2:], F32)
    zeros_fconv = jnp.zeros((depth, b) + state_ffn_conv.shape[2:], F32)
    yp, ph, plc, pk, pv, pfc = _run_trunk(x_prompt, p_mem_k, p_mem_v, zeros_h, zeros_lconv, None,
                                          zeros_fconv, p, cfg_p)
    ys, sh, slc, sk, sv, sfc = _run_trunk(x_sample, cache_mem_k, cache_mem_v, state_lru_h, state_lru_conv,
                                          (cache_attn_k, cache_attn_v), state_ffn_conv, p, cfg_s)

    dk = p["attn_q_norm"].shape[-1]
    db, dt = x_sample.shape[0], x_sample.shape[1]

    def kshape(a, n, tt):
        return a.reshape(n_attn, n, tt, DIFF_HEADS, 2, dk)

    def vshape(a, n, tt):
        a = a.reshape(n_attn, n, tt, 2, DIFF_HEADS, dk)
        return a.transpose(0, 1, 2, 4, 3, 5).reshape(n_attn, n, tt, DIFF_HEADS, 2 * dk)

    return (yp, ys, jnp.stack(ph), jnp.stack(plc), kshape(pk, b, t), vshape(pv, b, t), p_mem_k, p_mem_v,
            jnp.stack(pfc), jnp.stack(sh), jnp.stack(slc), kshape(sk, db, dt), vshape(sv, db, dt),
            jnp.stack(sfc))


CFG_PROMPT = dict(bm=1024, bc=512, lru_rows=128, att_rows=256, bm_ffn=1024, bf=512, ffn_rows=512,
                  bm_out=1024, bn_out=1024, tq=1024, tk=1024)
CFG_SAMPLE = dict(bm=1024, bc=512, lru_rows=128, att_rows=256, bm_ffn=1024, bf=512, ffn_rows=512,
                  bm_out=1024, bn_out=1024)


def kernel(x_prompt, x_sample, mem_prompt, cache_attn_k, cache_attn_v, cache_mem_k, cache_mem_v, state_lru_h, state_lru_conv, state_ffn_conv, norm_mix, norm_ffn, lru_w_in, lru_conv_w, lru_conv_b, lru_gate_a_w, lru_gate_a_b, lru_gate_x_w, lru_gate_x_b, lru_lambda, lru_w_out, attn_w_in, attn_q_norm, attn_k_norm, attn_lambda, attn_subln, attn_w_out, mem_norm, mem_w_kv, mem_q_norm, mem_k_norm, ffn_w_up, ffn_conv_w, ffn_conv_b, ffn_w_down):
    p = {
        "norm_mix": norm_mix, "norm_ffn": norm_ffn,
        "lru_w_in": lru_w_in, "lru_conv_w": lru_conv_w, "lru_conv_b": lru_conv_b,
        "lru_gate_a_w": lru_gate_a_w, "lru_gate_a_b": lru_gate_a_b,
        "lru_gate_x_w": lru_gate_x_w, "lru_gate_x_b": lru_gate_x_b,
        "lru_lambda": lru_lambda, "lru_w_out": lru_w_out,
        "attn_w_in": attn_w_in, "attn_q_norm": attn_q_norm, "attn_k_norm": attn_k_norm,
        "attn_lambda": attn_lambda, "attn_subln": attn_subln, "attn_w_out": attn_w_out,
        "mem_q_norm": mem_q_norm,
        "ffn_w_up": ffn_w_up, "ffn_conv_w": ffn_conv_w, "ffn_conv_b": ffn_conv_b, "ffn_w_down": ffn_w_down,
    }
    return _forward(x_prompt, x_sample, mem_prompt, cache_attn_k, cache_attn_v, cache_mem_k, cache_mem_v,
                    state_lru_h, state_lru_conv, state_ffn_conv, p, mem_norm, mem_w_kv, mem_k_norm,
                    CFG_PROMPT, CFG_SAMPLE)
```

```python
import functools
import math

import jax
import jax.numpy as jnp
from jax import lax
from jax.experimental import pallas as pl
from jax.experimental.pallas import tpu as pltpu

F32 = jnp.float32
BF16 = jnp.bfloat16

CHUNK = 64
LRU_HEADS = 8
LRU_C = 8.0
DIFF_HEADS = 8
MEM_HEADS = 4
RMS_EPS = 1e-6
NEG_BIG = -1e30
LOG2E = 1.4426950408889634
SUBLANES = 8
MEM_ATTN_ROWS = 512

V7X_VMEM_LIMIT_BYTES = 58 * 1024 * 1024


def _cparams(n_grid_axes):
    return pltpu.CompilerParams(
        dimension_semantics=("arbitrary",) * n_grid_axes,
        vmem_limit_bytes=V7X_VMEM_LIMIT_BYTES,
    )


def _dot(a, b):
    return jnp.dot(a, b, preferred_element_type=F32)


def _dot_nt(a, b):
    return lax.dot_general(a, b, (((1,), (1,)), ((), ())), preferred_element_type=F32)


def _rms(x, g):
    return x * lax.rsqrt(jnp.mean(x * x, axis=-1, keepdims=True) + RMS_EPS) * g


def _gelu(x):
    return x * (0.5 * (1.0 + jnp.tanh(0.7978845608028654 * (x + 0.044715 * (x * x * x)))))


def _cat_rows(parts):
    return parts[0] if len(parts) == 1 else jnp.concatenate(parts, axis=0)


def _delayed(seg, prev, d):
    n_prev = prev.shape[0]
    v = pltpu.roll(seg, d, 0)
    row = lax.broadcasted_iota(jnp.int32, (SUBLANES, 1), 0)
    top = v[0:SUBLANES]
    for r in range(d):
        top = jnp.where(row == r, prev[n_prev - d + r:n_prev - d + r + 1], top)
    return jnp.concatenate([top, v[SUBLANES:]], axis=0)


def _causal_conv(seg, prev, cw, cb):
    n_prev = cw.shape[0] - 1
    y = cb + _delayed(seg, prev, n_prev) * cw[0:1]
    for t in range(1, n_prev):
        y = y + _delayed(seg, prev, n_prev - t) * cw[t:t + 1]
    return y + seg * cw[n_prev:n_prev + 1]


def _scan_rows(a, b):
    n = a.shape[0]
    row = lax.broadcasted_iota(jnp.int32, (n, 1), 0)
    sh = 1
    while sh < min(SUBLANES, n):
        valid = row >= sh
        a_sh = pltpu.roll(a, sh, 0)
        b_sh = pltpu.roll(b, sh, 0)
        b = jnp.where(valid, b + a * b_sh, b)
        a = jnp.where(valid, a * a_sh, a)
        sh *= 2
    while sh < n:
        b = jnp.concatenate([b[:sh], b[sh:] + a[sh:] * b[:n - sh]], axis=0)
        a = jnp.concatenate([a[:sh], a[sh:] * a[:n - sh]], axis=0)
        sh *= 2
    return a, b


def _two_stage(n_chunks, first, second):
    for t in range(n_chunks + 1):
        if t < n_chunks:
            first(t)
        if t >= 1:
            second(t - 1)


def _mem_attn(q, gq, mk_ref, mv_ref, seg0, n_seg, seg_len):
    hd = gq.shape[-1]
    outs = []
    for h in range(q.shape[-1] // hd):
        cols = slice(h * hd, (h + 1) * hd)
        qn = _rms(q[:, cols], gq).astype(BF16)
        segs = []
        for s in range(n_seg):
            k = mk_ref[seg0 + s, :, cols]
            v = mv_ref[seg0 + s, :, cols]
            sc = _dot_nt(qn[s * seg_len:(s + 1) * seg_len], k) * (hd ** -0.5)
            p = jnp.exp(sc - jnp.max(sc, axis=-1, keepdims=True))
            pr = p / jnp.sum(p, axis=-1, keepdims=True)
            segs.append(_dot(pr.astype(BF16), v))
        outs.append(_cat_rows(segs))
    return outs[0] if len(outs) == 1 else jnp.concatenate(outs, axis=-1)


class _Tiling:
    def __init__(self, n_seq, t, bm):
        self.n_seq, self.t = n_seq, t
        if t >= bm:
            assert t % bm == 0
            self.n_seg, self.seg_len, self.tiles_per_seq = 1, bm, t // bm
        else:
            assert bm % t == 0 and n_seq % (bm // t) == 0
            self.n_seg, self.seg_len, self.tiles_per_seq = bm // t, t, 1
        self.bm = self.n_seg * self.seg_len
        self.n_tiles = n_seq * t // self.bm
        assert self.seg_len & (self.seg_len - 1) == 0 and self.seg_len >= SUBLANES


def _last_tile_state(per_tile, tl):
    rows, c = per_tile.shape[2:]
    return per_tile.reshape(tl.n_seq, tl.tiles_per_seq, rows, c)[:, -1]


def _resident(block_shape, index_map):
    return pl.BlockSpec(block_shape, index_map, pipeline_mode=pl.Buffered(1))


def _mem_kv_kernel(mem_ref, gn_ref, wk_ref, wv_ref, gk_ref, k_ref, v_ref, hn_ref):
    @pl.when(pl.program_id(1) == 0)
    def _():
        hn_ref[...] = _rms(mem_ref[...], gn_ref[...]).astype(BF16)

    hn = hn_ref[...]
    k_ref[...] = _rms(_dot(hn, wk_ref[...].astype(BF16)), gk_ref[...])
    v_ref[...] = _dot(hn, wv_ref[...].astype(BF16))


def _mem_kv(mem2d, mem_norm, w_kv, mem_k_norm):
    depth, d, two_w = w_kv.shape
    mem_w = two_w // 2
    hd = mem_w // MEM_HEADS
    m = mem2d.shape[0]
    return pl.pallas_call(
        _mem_kv_kernel,
        grid=(depth, MEM_HEADS),
        in_specs=[
            pl.BlockSpec((m, d), lambda l, j: (0, 0)),
            pl.BlockSpec((None, 1, d), lambda l, j: (l, 0, 0)),
            pl.BlockSpec((None, d, hd), lambda l, j: (l, 0, j)),
            pl.BlockSpec((None, d, hd), lambda l, j: (l, 0, MEM_HEADS + j)),
            pl.BlockSpec((None, 1, hd), lambda l, j: (l, 0, 0)),
        ],
        out_specs=[
            pl.BlockSpec((None, m, hd), lambda l, j: (l, 0, j)),
            pl.BlockSpec((None, m, hd), lambda l, j: (l, 0, j)),
        ],
        out_shape=[jax.ShapeDtypeStruct((depth, m, mem_w), F32)] * 2,
        scratch_shapes=[pltpu.VMEM((m, d), BF16)],
        compiler_params=_cparams(2),
        name="mem_kv",
    )(mem2d, mem_norm, w_kv, w_kv, mem_k_norm)


def _lru_in_kernel(x_ref, gn_ref, wg_ref, wx_ref, cw_ref, cb_ref, wa_ref, wi_ref, ba_ref, bi_ref,
                   lam_ref, h0_ref, c0_ref, gq_ref, mk_ref, mv_ref,
                   mix_ref, mo_ref, hn_out_ref, cn_out_ref,
                   hn_ref, hc_ref, cc_ref, raw_ref, *, n_seg, seg_len, tiles_per_seq, row_chunk):
    i = pl.program_id(0)
    j = pl.program_id(1)
    carried = tiles_per_seq > 1

    @pl.when(j == 0)
    def _():
        hn_ref[...] = _rms(x_ref[...], gn_ref[...]).astype(BF16)

    @pl.when(j < LRU_HEADS)
    def _():
        if carried:
            @pl.when(i % tiles_per_seq == 0)
            def _():
                cc_ref[j] = c0_ref[0]
                hc_ref[j] = h0_ref[0]

        cw = cw_ref[...]
        cb = cb_ref[...]
        wg, wx, wa, wi = (r[...].astype(BF16) for r in (wg_ref, wx_ref, wa_ref, wi_ref))
        n_prev = cw.shape[0] - 1
        n_chunks = hn_ref.shape[0] // row_chunk
        segs_per_chunk = row_chunk // seg_len if n_seg > 1 else 0
        whole = n_seg == 1
        state = {"conv": (cc_ref[j] if carried else c0_ref[0]) if whole else None,
                 "h": (hc_ref[j] if carried else h0_ref[0]) if whole else None}
        log_lam = jax.nn.log_sigmoid(lam_ref[...])

        def rows(c):
            return slice(c * row_chunk, (c + 1) * row_chunk)

        def in_proj(c):
            hn = hn_ref[rows(c), :]
            raw_ref[0, rows(c), :] = _dot(hn, wg)
            raw_ref[1, rows(c), :] = _dot(hn, wx)

        def conv_gates(c):
            xr = raw_ref[1, rows(c), :]
            if whole:
                xc = _causal_conv(xr, state["conv"], cw, cb)
                state["conv"] = xr[row_chunk - n_prev:row_chunk]
            else:
                parts = []
                for s in range(segs_per_chunk):
                    sidx = c * segs_per_chunk + s
                    seg = xr[s * seg_len:(s + 1) * seg_len]
                    parts.append(_causal_conv(seg, c0_ref[sidx], cw, cb))
                    cn_out_ref[sidx] = seg[seg_len - n_prev:seg_len]
                xc = _cat_rows(parts)
            xcb = xc.astype(BF16)
            raw_ref[1, rows(c), :] = xc
            raw_ref[2, rows(c), :] = _dot(xcb, wa)
            raw_ref[3, rows(c), :] = _dot(xcb, wi)

        def recur(c):
            xc = raw_ref[1, rows(c), :]
            r = jax.nn.sigmoid(raw_ref[2, rows(c), :] + ba_ref[...])
            ig = jax.nn.sigmoid(raw_ref[3, rows(c), :] + bi_ref[...])
            log_a = (LRU_C * r) * log_lam
            a = jnp.exp(log_a)
            one_minus_a2 = -jnp.tanh(log_a) * (a * a + 1.0)
            gated = jnp.sqrt(one_minus_a2) * (ig * xc)
            if whole:
                cum_a, cum_b = _scan_rows(a, gated)
                hs = cum_b + cum_a * state["h"]
                state["h"] = hs[row_chunk - 1:row_chunk]
            else:
                parts = []
                for s in range(segs_per_chunk):
                    sidx = c * segs_per_chunk + s
                    sl = slice(s * seg_len, (s + 1) * seg_len)
                    cum_a, cum_b = _scan_rows(a[sl], gated[sl])
                    seg_hs = cum_b + cum_a * h0_ref[sidx]
                    hn_out_ref[sidx] = seg_hs[seg_len - 1:seg_len]
                    parts.append(seg_hs)
                hs = _cat_rows(parts)
            mix_ref[rows(c), :] = (_gelu(raw_ref[0, rows(c), :]) * hs).astype(BF16)

        for t in range(n_chunks + 2):
            if t < n_chunks:
                in_proj(t)
            if 0 <= t - 1 < n_chunks:
                conv_gates(t - 1)
            if 0 <= t - 2 < n_chunks:
                recur(t - 2)
        if whole:
            cn_out_ref[0] = state["conv"]
            hn_out_ref[0] = state["h"]
            if carried:
                cc_ref[j] = state["conv"]
                hc_ref[j] = state["h"]

    @pl.when(j >= LRU_HEADS)
    def _():
        chunk = max(row_chunk, min(hn_ref.shape[0], MEM_ATTN_ROWS))
        n_chunks = hn_ref.shape[0] // chunk
        segs = (chunk // seg_len, seg_len) if n_seg > 1 else (1, chunk)

        wq = wx_ref[...].astype(BF16)

        def rows(c):
            return slice(c * chunk, (c + 1) * chunk)

        def project(c):
            raw_ref[0, rows(c), :] = _dot(hn_ref[rows(c), :], wq)

        def attend(c):
            seg0 = c * segs[0] if n_seg > 1 else 0
            mo_ref[rows(c), :] = _mem_attn(raw_ref[0, rows(c), :], gq_ref[...], mk_ref, mv_ref,
                                           seg0, *segs).astype(BF16)

        _two_stage(n_chunks, project, attend)


def _lru_in(x2d, tl, li, mi, p, h0, c0, mk, mv, row_chunk):
    m, d = x2d.shape
    hw = d // LRU_HEADS
    n_steps = LRU_HEADS + MEM_HEADS
    ns, tps = tl.n_seg, tl.tiles_per_seq
    width = p["lru_conv_w"].shape[1]
    last = LRU_HEADS - 1

    def hcol(j):
        return jnp.minimum(j, last)

    def mcol(j):
        return jnp.maximum(j - LRU_HEADS, 0)

    vec = pl.BlockSpec((None, 1, hw), lambda i, j: (mi, 0, hcol(j)))
    gate_w = pl.BlockSpec((None, None, hw, hw), lambda i, j: (mi, hcol(j), 0, 0))
    mem = pl.BlockSpec((None, ns, mk.shape[2], hw), lambda i, j: (li, i // tps, 0, mcol(j)))
    assert tl.bm % row_chunk == 0 and (row_chunk % tl.seg_len == 0 if ns > 1 else True)
    kern = functools.partial(_lru_in_kernel, n_seg=ns, seg_len=tl.seg_len, tiles_per_seq=tps,
                             row_chunk=row_chunk)
    return pl.pallas_call(
        kern,
        grid=(tl.n_tiles, n_steps),
        in_specs=[
            _resident((tl.bm, d), lambda i, j: (i, 0)),
            pl.BlockSpec((None, 1, d), lambda i, j: (li, 0, 0)),
            pl.BlockSpec((None, d, hw), lambda i, j: (mi, 0, hcol(j))),
            pl.BlockSpec((None, d, hw), lambda i, j: (mi, 0, LRU_HEADS + j)),
            pl.BlockSpec((None, width, hw), lambda i, j: (mi, 0, hcol(j))),
            vec, gate_w, gate_w, vec, vec, vec,
            pl.BlockSpec((None, ns, 1, hw), lambda i, j: (mi, i // tps, 0, hcol(j))),
            pl.BlockSpec((None, ns, width - 1, hw), lambda i, j: (mi, i // tps, 0, hcol(j))),
            pl.BlockSpec((None, 1, hw), lambda i, j: (li, 0, 0)),
            mem, mem,
        ],
        out_specs=[
            pl.BlockSpec((tl.bm, hw), lambda i, j: (i, hcol(j))),
            pl.BlockSpec((tl.bm, hw), lambda i, j: (i, mcol(j))),
            pl.BlockSpec((None, ns, 1, hw), lambda i, j: (i, 0, 0, hcol(j))),
            pl.BlockSpec((None, ns, width - 1, hw), lambda i, j: (i, 0, 0, hcol(j))),
        ],
        out_shape=[
            jax.ShapeDtypeStruct((m, d), BF16),
            jax.ShapeDtypeStruct((m, MEM_HEADS * hw), BF16),
            jax.ShapeDtypeStruct((tl.n_tiles, ns, 1, d), F32),
            jax.ShapeDtypeStruct((tl.n_tiles, ns, width - 1, d), F32),
        ],
        scratch_shapes=[
            pltpu.VMEM((tl.bm, d), BF16),
            pltpu.VMEM((LRU_HEADS, 1, hw), F32),
            pltpu.VMEM((LRU_HEADS, width - 1, hw), F32),
            pltpu.VMEM((4, tl.bm, hw), F32),
        ],
        compiler_params=_cparams(2),
        name="lru_in",
    )(x2d, p["norm_mix"], p["lru_w_in"], p["lru_w_in"], p["lru_conv_w"], p["lru_conv_b"],
      p["lru_gate_a_w"], p["lru_gate_x_w"], p["lru_gate_a_b"], p["lru_gate_x_b"], p["lru_lambda"],
      h0, c0, p["mem_q_norm"], mk, mv)


def _att_in_kernel(*refs, n_seg, seg_len, nq, row_chunk, aliased):
    if aliased:
        refs = refs[:8] + refs[10:]
    (x_ref, gn_ref, w_ref, qg_ref, kg_ref, gq_ref, mk_ref, mv_ref,
     qb_ref, kf_ref, kb_ref, vf_ref, vb_ref, mo_ref, hn_ref, z_ref) = refs
    j = pl.program_id(1)

    @pl.when(j == 0)
    def _():
        hn_ref[...] = _rms(x_ref[...], gn_ref[...]).astype(BF16)

    dk = qg_ref.shape[-1]
    n_groups = z_ref.shape[-1] // dk
    rows_per_t = hn_ref.shape[-1] // dk
    n_heads = rows_per_t // 2
    n_chunks = hn_ref.shape[0] // row_chunk
    w = w_ref[...].astype(BF16)

    def rows(c):
        return slice(c * row_chunk, (c + 1) * row_chunk)

    def project(c):
        z_ref[rows(c), :] = _dot(hn_ref[rows(c), :], w)

    def queries(c):
        qscale = dk ** -0.5 * LOG2E
        for g in range(n_groups):
            sl = slice(g * dk, (g + 1) * dk)
            qb_ref[rows(c), sl] = (_rms(z_ref[rows(c), sl], qg_ref[...]) * qscale).astype(BF16)

    def keys(c):
        for g in range(n_groups):
            sl = slice(g * dk, (g + 1) * dk)
            kn = _rms(z_ref[rows(c), sl], kg_ref[...])
            group = (j - nq) * n_groups + g
            kf_ref[pl.ds(c * row_chunk * rows_per_t + group, row_chunk, stride=rows_per_t), :] = kn
            kb_ref[rows(c), sl] = kn.astype(BF16)

    def values(c):
        z = z_ref[rows(c), :]
        vb_ref[rows(c), :] = z.astype(BF16)
        for g in range(n_groups):
            group = (j - 2 * nq) * n_groups + g
            row = (group & 1) * n_heads + (group >> 1)
            vf_ref[pl.ds(c * row_chunk * rows_per_t + row, row_chunk, stride=rows_per_t), :] = (
                z[:, g * dk:(g + 1) * dk])

    mem_chunk = max(row_chunk, min(hn_ref.shape[0], MEM_ATTN_ROWS))
    mem_segs = (mem_chunk // seg_len, seg_len) if n_seg > 1 else (1, mem_chunk)

    def mem_rows(c):
        return slice(c * mem_chunk, (c + 1) * mem_chunk)

    def mem_project(c):
        z_ref[mem_rows(c), :] = _dot(hn_ref[mem_rows(c), :], w)

    def memory(c):
        seg0 = c * mem_segs[0] if n_seg > 1 else 0
        mo_ref[mem_rows(c), :] = _mem_attn(z_ref[mem_rows(c), :], gq_ref[...], mk_ref, mv_ref,
                                           seg0, *mem_segs).astype(BF16)

    @pl.when(j < nq)
    def _():
        _two_stage(n_chunks, project, queries)

    @pl.when((j >= nq) & (j < 2 * nq))
    def _():
        _two_stage(n_chunks, project, keys)

    @pl.when((j >= 2 * nq) & (j < 3 * nq))
    def _():
        _two_stage(n_chunks, project, values)

    @pl.when(j >= 3 * nq)
    def _():
        _two_stage(hn_ref.shape[0] // mem_chunk, mem_project, memory)


def _att_in(x2d, tl, li, ai, p, mk, mv, bc, row_chunk, kv_all):
    m, d = x2d.shape
    mem_w = mk.shape[-1]
    nq, nm = d // bc, mem_w // bc
    n_steps = 3 * nq + nm
    ns, tps = tl.n_seg, tl.tiles_per_seq
    dk = p["attn_q_norm"].shape[-1]
    hd = p["mem_q_norm"].shape[-1]

    def col(lo, n):
        return lambda i, j: (i, jnp.clip(j - lo, 0, n - 1))

    mem = _resident((None, ns, mk.shape[2], bc),
                    lambda i, j: (li, i // tps, 0, jnp.clip(j - 3 * nq, 0, nm - 1)))
    assert tl.bm % row_chunk == 0 and (row_chunk % tl.seg_len == 0 if ns > 1 else True)
    n_attn = p["attn_w_in"].shape[0]
    rows_per_t = d // dk
    aliased = kv_all is not None
    kern = functools.partial(_att_in_kernel, n_seg=ns, seg_len=tl.seg_len, nq=nq, row_chunk=row_chunk,
                             aliased=aliased)
    kv_spec = _resident((None, tl.bm * rows_per_t, dk), lambda i, j: (ai, i, 0))
    kv_shape = jax.ShapeDtypeStruct((n_attn, m * rows_per_t, dk), F32)
    extra_in = [pl.BlockSpec(memory_space=pl.ANY)] * 2 if aliased else []
    return pl.pallas_call(
        kern,
        grid=(tl.n_tiles, n_steps),
        in_specs=[
            _resident((tl.bm, d), lambda i, j: (i, 0)),
            pl.BlockSpec((None, 1, d), lambda i, j: (li, 0, 0)),
            pl.BlockSpec((None, d, bc), lambda i, j: (ai, 0, j)),
            pl.BlockSpec((None, 1, dk), lambda i, j: (ai, 0, 0)),
            pl.BlockSpec((None, 1, dk), lambda i, j: (ai, 0, 0)),
            pl.BlockSpec((None, 1, hd), lambda i, j: (li, 0, 0)),
            mem, mem, *extra_in,
        ],
        out_specs=[
            pl.BlockSpec((tl.bm, bc), col(0, nq)),
            kv_spec,
            pl.BlockSpec((tl.bm, bc), col(nq, nq)),
            kv_spec,
            pl.BlockSpec((tl.bm, bc), col(2 * nq, nq)),
            pl.BlockSpec((tl.bm, bc), col(3 * nq, nm)),
        ],
        out_shape=[
            jax.ShapeDtypeStruct((m, d), BF16),
            kv_shape,
            jax.ShapeDtypeStruct((m, d), BF16),
            kv_shape,
            jax.ShapeDtypeStruct((m, d), BF16),
            jax.ShapeDtypeStruct((m, mem_w), BF16),
        ],
        input_output_aliases={8: 1, 9: 3} if aliased else {},
        scratch_shapes=[pltpu.VMEM((tl.bm, d), BF16), pltpu.VMEM((tl.bm, bc), F32)],
        compiler_params=_cparams(2),
        name="att_in",
    )(x2d, p["norm_mix"], p["attn_w_in"], p["attn_q_norm"], p["attn_k_norm"], p["mem_q_norm"], mk, mv,
      *(kv_all if aliased else ()))


def _attn_block(q, k, v, m_ref, l_ref, acc_ref, mask, lanes=slice(None)):
    dk = q.shape[-1] // 2
    tk = k.shape[0]
    scores = [_dot_nt(k[:, c * dk:(c + 1) * dk], q[:, c * dk:(c + 1) * dk]) for c in range(2)]
    for c in range(2):
        s = scores[c]
        if mask is not None:
            s = jnp.where(mask, s, NEG_BIG)
        m_old = m_ref[c, :, lanes]
        m_new = jnp.maximum(m_old, jnp.max(s, axis=0, keepdims=True))
        alpha = jnp.exp2(m_old - m_new)
        p = jnp.exp2(s - m_new)
        l_ref[c, :, lanes] = (alpha * l_ref[c, :, lanes]
                              + jnp.sum(p.reshape(tk // SUBLANES, SUBLANES, p.shape[-1]), axis=0))
        pv = lax.dot_general(v, p.astype(BF16), (((0,), (0,)), ((), ())), preferred_element_type=F32)
        acc_ref[c, :, lanes] = alpha * acc_ref[c, :, lanes] + pv
        m_ref[c, :, lanes] = m_new


def _attn_init(m_ref, l_ref, acc_ref):
    m_ref[...] = jnp.full(m_ref.shape, NEG_BIG, F32)
    l_ref[...] = jnp.zeros(l_ref.shape, F32)
    acc_ref[...] = jnp.zeros(acc_ref.shape, F32)


def _attn_finish(lp_ref, sg_ref, l_ref, acc_ref, lam_init):
    lp = lp_ref[...]
    lam = (jnp.exp(jnp.sum(lp[0:1] * lp[1:2], axis=-1, keepdims=True))
           - jnp.exp(jnp.sum(lp[2:3] * lp[3:4], axis=-1, keepdims=True)) + lam_init)
    l0 = jnp.sum(l_ref[0], axis=0, keepdims=True)
    l1 = jnp.sum(l_ref[1], axis=0, keepdims=True)
    o = (acc_ref[0] / l0 - lam * (acc_ref[1] / l1)).T
    return (_rms(o, sg_ref[...]) * (1.0 - lam_init)).astype(BF16)


def _attn_prompt_kernel(lp_ref, sg_ref, q_ref, k_ref, v_ref, o_ref, m_ref, l_ref, acc_ref, *, tq, tk, lam_init):
    qi = pl.program_id(2)
    _attn_init(m_ref, l_ref, acc_ref)
    q = q_ref[...]
    per = tk // tq

    def visible(start, size):
        start = pl.multiple_of(start, size)
        _attn_block(q, k_ref[pl.ds(start, size), :], v_ref[pl.ds(start, size), :], m_ref, l_ref, acc_ref, None)

    def body(kv, carry):
        visible(kv * tk, tk)
        return carry

    n_big = qi // per
    lax.fori_loop(0, n_big, body, 0)
    for r in range(1, per):
        @pl.when(qi % per >= r)
        def _():
            visible(n_big * tk + (r - 1) * tq, tq)

    half = tq // 2
    shift = CHUNK.bit_length() - 1

    def chunk_mask(n_keys, n_queries):
        keyc = lax.shift_right_logical(lax.broadcasted_iota(jnp.int32, (n_keys, n_queries), 0), shift)
        qryc = lax.shift_right_logical(lax.broadcasted_iota(jnp.int32, (n_keys, n_queries), 1), shift)
        return keyc <= qryc

    start = pl.multiple_of(qi * tq, tq)
    _attn_block(q, k_ref[pl.ds(start, half), :], v_ref[pl.ds(start, half), :], m_ref, l_ref, acc_ref,
                chunk_mask(half, tq))
    start = pl.multiple_of(qi * tq + half, half)
    _attn_block(q[half:], k_ref[pl.ds(start, half), :], v_ref[pl.ds(start, half), :], m_ref, l_ref, acc_ref,
                chunk_mask(half, half), lanes=slice(half, tq))
    o_ref[...] = _attn_finish(lp_ref, sg_ref, l_ref, acc_ref, lam_init)


def _attn_prompt(qb, kb, vb, ai, p, n_seq, t, tq, tk, lam_init):
    d = qb.shape[-1]
    hw = d // DIFF_HEADS
    assert t % tq == 0 and tq % (2 * CHUNK) == 0 and tk % tq == 0
    q3, k3, v3 = (a.reshape(n_seq, t, d) for a in (qb, kb, vb))
    lp, sg = p["attn_lambda"], p["attn_subln"]
    kern = functools.partial(_attn_prompt_kernel, tq=tq, tk=tk, lam_init=lam_init)
    out = pl.pallas_call(
        kern,
        grid=(n_seq, DIFF_HEADS, t // tq),
        in_specs=[
            pl.BlockSpec((None,) + lp.shape[1:], lambda b, h, qi: (ai, 0, 0)),
            pl.BlockSpec((None, 1, hw), lambda b, h, qi: (ai, 0, 0)),
            pl.BlockSpec((None, tq, hw), lambda b, h, qi: (b, qi, h)),
            pl.BlockSpec((None, t, hw), lambda b, h, qi: (b, 0, h)),
            pl.BlockSpec((None, t, hw), lambda b, h, qi: (b, 0, h)),
        ],
        out_specs=pl.BlockSpec((None, tq, hw), lambda b, h, qi: (b, qi, h)),
        out_shape=jax.ShapeDtypeStruct((n_seq, t, d), BF16),
        scratch_shapes=[
            pltpu.VMEM((2, 1, tq), F32),
            pltpu.VMEM((2, SUBLANES, tq), F32),
            pltpu.VMEM((2, hw, tq), F32),
        ],
        compiler_params=_cparams(3),
        name="attn_prompt",
    )(lp, sg, q3, k3, v3)
    return out.reshape(n_seq * t, d)


def _attn_cached_kernel(lp_ref, sg_ref, q_ref, kp_ref, vp_ref, kn_ref, vn_ref, o_ref,
                        m_ref, l_ref, acc_ref, *, past, lam_init):
    n_heads = DIFF_HEADS
    rows_per_t = 2 * n_heads
    hw = q_ref.shape[-1] // n_heads
    for h in range(n_heads):
        cols = slice(h * hw, (h + 1) * hw)
        k = jnp.concatenate([kp_ref[pl.ds(2 * h + c, past, stride=rows_per_t), :] for c in range(2)], axis=-1)
        v = jnp.concatenate([vp_ref[pl.ds(c * n_heads + h, past, stride=rows_per_t), :] for c in range(2)],
                            axis=-1)
        q = q_ref[:, cols]
        _attn_init(m_ref, l_ref, acc_ref)
        _attn_block(q, k.astype(BF16), v.astype(BF16), m_ref, l_ref, acc_ref, None)
        _attn_block(q, kn_ref[:, cols], vn_ref[:, cols], m_ref, l_ref, acc_ref, None)
        o_ref[:, cols] = _attn_finish(lp_ref, sg_ref, l_ref, acc_ref, lam_init)


def _attn_cached(qb, kb, vb, k_past, v_past, ai, p, n_seq, t, lam_init):
    d = qb.shape[-1]
    hw = d // DIFF_HEADS
    n_attn, _, past, n_heads, _, dk = k_past.shape
    assert past % CHUNK == 0 and t <= CHUNK and n_heads == DIFF_HEADS and 2 * dk == hw
    q3, k3, v3 = (a.reshape(n_seq, t, d) for a in (qb, kb, vb))
    lp, sg = p["attn_lambda"], p["attn_subln"]
    k_rows = k_past.reshape(n_attn * n_seq, past * 2 * n_heads, dk)
    v_rows = v_past.reshape(n_attn, n_seq, past, n_heads, 2, dk).transpose(0, 1, 2, 4, 3, 5)
    v_rows = v_rows.reshape(n_attn * n_seq, past * 2 * n_heads, dk)
    kern = functools.partial(_attn_cached_kernel, past=past, lam_init=lam_init)
    new_spec = pl.BlockSpec((None, t, d), lambda b: (b, 0, 0))
    past_spec = pl.BlockSpec((None, past * 2 * n_heads, dk), lambda b: (ai * n_seq + b, 0, 0))
    out = pl.pallas_call(
        kern,
        grid=(n_seq,),
        in_specs=[
            pl.BlockSpec((None,) + lp.shape[1:], lambda b: (ai, 0, 0)),
            pl.BlockSpec((None, 1, hw), lambda b: (ai, 0, 0)),
            new_spec, past_spec, past_spec, new_spec, new_spec,
        ],
        out_specs=new_spec,
        out_shape=jax.ShapeDtypeStruct((n_seq, t, d), BF16),
        scratch_shapes=[
            pltpu.VMEM((2, 1, t), F32),
            pltpu.VMEM((2, SUBLANES, t), F32),
            pltpu.VMEM((2, hw, t), F32),
        ],
        compiler_params=_cparams(1),
        name="attn_cached",
    )(lp, sg, q3, k_rows, v_rows, k3, v3)
    return out.reshape(n_seq * t, d)


def _out_kernel(x_ref, mix_ref, mo_ref, w1_ref, w2_ref, o_ref):
    o_ref[...] = x_ref[...] + (_dot(mix_ref[...], w1_ref[...]) + _dot(mo_ref[...], w2_ref[...]))


def _out_proj(x2d, mix, mo, w_out, wi, bm, bn):
    m, d = x2d.shape
    k1, k2 = mix.shape[1], mo.shape[1]
    assert k1 % k2 == 0
    return pl.pallas_call(
        _out_kernel,
        grid=(m // bm, d // bn),
        in_specs=[
            pl.BlockSpec((bm, bn), lambda i, j: (i, j)),
            pl.BlockSpec((bm, k1), lambda i, j: (i, 0)),
            pl.BlockSpec((bm, k2), lambda i, j: (i, 0)),
            pl.BlockSpec((None, k1, bn), lambda i, j: (wi, 0, j)),
            pl.BlockSpec((None, k2, bn), lambda i, j: (wi, k1 // k2, j)),
        ],
        out_specs=pl.BlockSpec((bm, bn), lambda i, j: (i, j)),
        out_shape=jax.ShapeDtypeStruct((m, d), F32),
        compiler_params=_cparams(2),
        name="out_proj",
    )(x2d, mix, mo, w_out, w_out)


def _ffn_kernel(x_ref, gn_ref, wg_ref, wv_ref, cwg_ref, cwv_ref, cbg_ref, cbv_ref, stg_ref, stv_ref, wdn_ref,
                o_ref, nstg_ref, nstv_ref, hn_ref, cg_ref, cv_ref, raw_ref, act_ref,
                *, n_seg, seg_len, tiles_per_seq, row_chunk):
    i = pl.program_id(0)
    j = pl.program_id(1)
    carried = tiles_per_seq > 1

    @pl.when(j == 0)
    def _():
        x = x_ref[...]
        hn_ref[...] = _rms(x, gn_ref[...]).astype(BF16)
        o_ref[...] = x

    if carried:
        @pl.when(i % tiles_per_seq == 0)
        def _():
            cg_ref[j] = stg_ref[0]
            cv_ref[j] = stv_ref[0]

    halves = ((wg_ref, cwg_ref, cbg_ref, stg_ref, cg_ref, nstg_ref),
              (wv_ref, cwv_ref, cbv_ref, stv_ref, cv_ref, nstv_ref))
    n_prev = cwg_ref.shape[0] - 1
    n_chunks = hn_ref.shape[0] // row_chunk
    segs_per_chunk = row_chunk // seg_len if n_seg > 1 else 0
    prevs = [(h[4][j] if carried else h[3][0]) if n_seg == 1 else None for h in halves]

    def rows(c):
        return slice(c * row_chunk, (c + 1) * row_chunk)

    def up_proj(c):
        hn = hn_ref[rows(c), :]
        for hi, h in enumerate(halves):
            raw_ref[hi, rows(c), :] = _dot(hn, h[0][...])

    def conv_gate(c):
        ys = []
        for hi, (_, cw_ref, cb_ref, st_ref, _, nst_ref) in enumerate(halves):
            up = raw_ref[hi, rows(c), :]
            cw = cw_ref[...]
            cb = cb_ref[...]
            if n_seg == 1:
                ys.append(_causal_conv(up, prevs[hi], cw, cb))
                prevs[hi] = up[row_chunk - n_prev:row_chunk]
            else:
                parts = []
                for s in range(segs_per_chunk):
                    seg = up[s * seg_len:(s + 1) * seg_len]
                    parts.append(_causal_conv(seg, st_ref[c * segs_per_chunk + s], cw, cb))
                    nst_ref[c * segs_per_chunk + s] = seg[seg_len - n_prev:seg_len]
                ys.append(_cat_rows(parts))
        act_ref[rows(c), :] = (_gelu(ys[0]) * ys[1]).astype(BF16)

    def down_proj(c):
        o_ref[rows(c), :] += _dot(act_ref[rows(c), :], wdn_ref[...])

    for t in range(n_chunks + 2):
        if t < n_chunks:
            up_proj(t)
        if 0 <= t - 1 < n_chunks:
            conv_gate(t - 1)
        if 0 <= t - 2 < n_chunks:
            down_proj(t - 2)
    if n_seg == 1:
        for hi, (_, _, _, _, carry_ref, nst_ref) in enumerate(halves):
            nst_ref[0] = prevs[hi]
            if carried:
                carry_ref[j] = prevs[hi]


def _ffn(x2d, tl, li, p, state, bf, row_chunk):
    m, d = x2d.shape
    dff = p["ffn_w_down"].shape[1]
    nj = dff // bf
    ns, tps = tl.n_seg, tl.tiles_per_seq
    width = p["ffn_conv_w"].shape[1]
    assert tl.bm % row_chunk == 0 and (row_chunk % tl.seg_len == 0 if ns > 1 else True)
    kern = functools.partial(_ffn_kernel, n_seg=ns, seg_len=tl.seg_len, tiles_per_seq=tps, row_chunk=row_chunk)

    def halves(block, index_map):
        gate = pl.BlockSpec(block, lambda i, j: index_map(i, j, j))
        value = pl.BlockSpec(block, lambda i, j: index_map(i, j, nj + j))
        return [gate, value]

    nst_spec = pl.BlockSpec((None, ns, width - 1, bf), lambda i, j: (i, 0, 0, j))
    nst_shape = jax.ShapeDtypeStruct((tl.n_tiles, ns, width - 1, dff), F32)
    return pl.pallas_call(
        kern,
        grid=(tl.n_tiles, nj),
        in_specs=[
            _resident((tl.bm, d), lambda i, j: (i, 0)),
            pl.BlockSpec((None, 1, d), lambda i, j: (li, 0, 0)),
            *halves((None, d, bf), lambda i, j, c: (li, 0, c)),
            *halves((None, width, bf), lambda i, j, c: (li, 0, c)),
            *halves((None, 1, bf), lambda i, j, c: (li, 0, c)),
            *halves((None, ns, width - 1, bf), lambda i, j, c: (li, i // tps, 0, c)),
            pl.BlockSpec((None, bf, d), lambda i, j: (li, j, 0)),
        ],
        out_specs=[pl.BlockSpec((tl.bm, d), lambda i, j: (i, 0)), nst_spec, nst_spec],
        out_shape=[jax.ShapeDtypeStruct((m, d), F32), nst_shape, nst_shape],
        scratch_shapes=[
            pltpu.VMEM((tl.bm, d), BF16),
            pltpu.VMEM((nj, width - 1, bf), F32),
            pltpu.VMEM((nj, width - 1, bf), F32),
            pltpu.VMEM((2, tl.bm, bf), F32),
            pltpu.VMEM((tl.bm, bf), BF16),
        ],
        compiler_params=_cparams(2),
        name="ffn",
    )(x2d, p["norm_ffn"], p["ffn_w_up"], p["ffn_w_up"], p["ffn_conv_w"], p["ffn_conv_w"],
      p["ffn_conv_b"], p["ffn_conv_b"], state, state, p["ffn_w_down"])


def _run_trunk(x, mem_k, mem_v, lru_h, lru_conv, attn_past, ffn_conv, p, cfg):
    n_seq, t, d = x.shape
    depth = p["norm_mix"].shape[0]
    x2d = x.reshape(n_seq * t, d)
    tl = _Tiling(n_seq, t, cfg["bm"])
    tl_ffn = _Tiling(n_seq, t, cfg["bm_ffn"])
    mem_tokens = mem_k.shape[2]
    mk = mem_k.reshape(depth, n_seq, mem_tokens, -1).astype(BF16)
    mv = mem_v.reshape(depth, n_seq, mem_tokens, -1).astype(BF16)
    h0 = lru_h[:, :, None, :]
    new_h, new_lconv, new_fconv = [], [], []
    kv_all = None
    for i in range(depth):
        j = i // 2
        if i % 2 == 0:
            mix, mo, h_last, buf = _lru_in(x2d, tl, i, j, p, h0, lru_conv, mk, mv, cfg["lru_rows"])
            new_h.append(_last_tile_state(h_last, tl)[:, 0, :])
            new_lconv.append(_last_tile_state(buf, tl))
            w_out = p["lru_w_out"]
        else:
            qb, k_all, kb, v_all, vb, mo = _att_in(x2d, tl, i, j, p, mk, mv, cfg["bc"], cfg["att_rows"], kv_all)
            kv_all = (k_all, v_all)
            lam_init = 0.8 - 0.6 * math.exp(-0.3 * i)
            if attn_past is None:
                mix = _attn_prompt(qb, kb, vb, j, p, n_seq, t, cfg["tq"], cfg["tk"], lam_init)
            else:
                mix = _attn_cached(qb, kb, vb, attn_past[0], attn_past[1], j, p, n_seq, t, lam_init)
            w_out = p["attn_w_out"]
        x2d = _out_proj(x2d, mix, mo, w_out, j, cfg["bm_out"], cfg["bn_out"])
        x2d, fg, fv = _ffn(x2d, tl_ffn, i, p, ffn_conv, cfg["bf"], cfg["ffn_rows"])
        new_fconv.append(jnp.concatenate([_last_tile_state(fg, tl_ffn), _last_tile_state(fv, tl_ffn)], axis=-1))
    return x2d.reshape(n_seq, t, d), new_h, new_lconv, kv_all[0], kv_all[1], new_fconv


def _forward(x_prompt, x_sample, mem_prompt, cache_attn_k, cache_attn_v, cache_mem_k, cache_mem_v,
             state_lru_h, state_lru_conv, state_ffn_conv, p, mem_norm, mem_w_kv, mem_k_norm, cfg_p, cfg_s):
    b, t, d = x_prompt.shape
    depth = p["norm_mix"].shape[0]
    n_lru, n_attn = (depth + 1) // 2, depth // 2

    p = dict(p)
    for name in ("ffn_w_up", "ffn_w_down", "lru_w_out", "attn_w_out"):
        p[name] = p[name].astype(BF16)
    for name in ("norm_mix", "norm_ffn", "lru_conv_b", "lru_gate_a_b", "lru_gate_x_b", "lru_lambda",
                 "attn_q_norm", "attn_k_norm", "attn_subln", "mem_q_norm", "ffn_conv_b"):
        p[name] = p[name][:, None, :]

    mem_tokens = mem_prompt.shape[1]
    mk, mv = _mem_kv(mem_prompt.reshape(b * mem_tokens, d), mem_norm[:, None, :], mem_w_kv,
                     mem_k_norm[:, None, :])
    hd = mem_k_norm.shape[-1]
    p_mem_k = mk.reshape(depth, b, mem_tokens, MEM_HEADS, hd)
    p_mem_v = mv.reshape(depth, b, mem_tokens, MEM_HEADS, hd)

    zeros_h = jnp.zeros((n_lru, b, d), F32)
    zeros_lconv = jnp.zeros((n_lru, b) + state_lru_conv.shape[2:], F32)
    zeros_fconv = jnp.zeros((depth, b) + state_ffn_conv.shape[2:], F32)
    yp, ph, plc, pk, pv, pfc = _run_trunk(x_prompt, p_mem_k, p_mem_v, zeros_h, zeros_lconv, None,
                                          zeros_fconv, p, cfg_p)
    ys, sh, slc, sk, sv, sfc = _run_trunk(x_sample, cache_mem_k, cache_mem_v, state_lru_h, state_lru_conv,
                                          (cache_attn_k, cache_attn_v), state_ffn_conv, p, cfg_s)

    dk = p["attn_q_norm"].shape[-1]
    db, dt = x_sample.shape[0], x_sample.shape[1]

    def kshape(a, n, tt):
        return a.reshape(n_attn, n, tt, DIFF_HEADS, 2, dk)

    def vshape(a, n, tt):
        a = a.reshape(n_attn, n, tt, 2, DIFF_HEADS, dk)
        return a.transpose(0, 1, 2, 4, 3, 5).reshape(n_attn, n, tt, DIFF_HEADS, 2 * dk)

    return (yp, ys, jnp.stack(ph), jnp.stack(plc), kshape(pk, b, t), vshape(pv, b, t), p_mem_k, p_mem_v,
            jnp.stack(pfc), jnp.stack(sh), jnp.stack(slc), kshape(sk, db, dt), vshape(sv, db, dt),
            jnp.stack(sfc))


CFG_PROMPT = dict(bm=1024, bc=512, lru_rows=128, att_rows=256, bm_ffn=1024, bf=512, ffn_rows=512,
                  bm_out=1024, bn_out=1024, tq=1024, tk=1024)
CFG_SAMPLE = dict(bm=1024, bc=512, lru_rows=128, att_rows=256, bm_ffn=1024, bf=512, ffn_rows=512,
                  bm_out=1024, bn_out=1024)


def kernel(x_prompt, x_sample, mem_prompt, cache_attn_k, cache_attn_v, cache_mem_k, cache_mem_v, state_lru_h, state_lru_conv, state_ffn_conv, norm_mix, norm_ffn, lru_w_in, lru_conv_w, lru_conv_b, lru_gate_a_w, lru_gate_a_b, lru_gate_x_w, lru_gate_x_b, lru_lambda, lru_w_out, attn_w_in, attn_q_norm, attn_k_norm, attn_lambda, attn_subln, attn_w_out, mem_norm, mem_w_kv, mem_q_norm, mem_k_norm, ffn_w_up, ffn_conv_w, ffn_conv_b, ffn_w_down):
    p = {
        "norm_mix": norm_mix, "norm_ffn": norm_ffn,
        "lru_w_in": lru_w_in, "lru_conv_w": lru_conv_w, "lru_conv_b": lru_conv_b,
        "lru_gate_a_w": lru_gate_a_w, "lru_gate_a_b": lru_gate_a_b,
        "lru_gate_x_w": lru_gate_x_w, "lru_gate_x_b": lru_gate_x_b,
        "lru_lambda": lru_lambda, "lru_w_out": lru_w_out,
        "attn_w_in": attn_w_in, "attn_q_norm": attn_q_norm, "attn_k_norm": attn_k_norm,
        "attn_lambda": attn_lambda, "attn_subln": attn_subln, "attn_w_out": attn_w_out,
        "mem_q_norm": mem_q_norm,
        "ffn_w_up": ffn_w_up, "ffn_conv_w": ffn_conv_w, "ffn_conv_b": ffn_conv_b, "ffn_w_down": ffn_w_down,
    }
    return _forward(x_prompt, x_sample, mem_prompt, cache_attn_k, cache_attn_v, cache_mem_k, cache_mem_v,
                    state_lru_h, state_lru_conv, state_ffn_conv, p, mem_norm, mem_w_kv, mem_k_norm,
                    CFG_PROMPT, CFG_SAMPLE)
```

```python
import functools
import math

import jax
import jax.numpy as jnp
from jax import lax
from jax.experimental import pallas as pl
from jax.experimental.pallas import tpu as pltpu

F32 = jnp.float32
BF16 = jnp.bfloat16

CHUNK = 64
LRU_HEADS = 8
LRU_C = 8.0
DIFF_HEADS = 8
MEM_HEADS = 4
RMS_EPS = 1e-6
NEG_BIG = -1e30
LOG2E = 1.4426950408889634
SUBLANES = 8
MEM_ATTN_ROWS = 512

V7X_VMEM_LIMIT_BYTES = 58 * 1024 * 1024


def _cparams(n_grid_axes):
    return pltpu.CompilerParams(
        dimension_semantics=("arbitrary",) * n_grid_axes,
        vmem_limit_bytes=V7X_VMEM_LIMIT_BYTES,
    )


def _dot(a, b):
    return jnp.dot(a, b, preferred_element_type=F32)


def _dot_nt(a, b):
    return lax.dot_general(a, b, (((1,), (1,)), ((), ())), preferred_element_type=F32)


def _rms(x, g):
    return x * lax.rsqrt(jnp.mean(x * x, axis=-1, keepdims=True) + RMS_EPS) * g


def _gelu(x):
    return x * (0.5 * (1.0 + jnp.tanh(0.7978845608028654 * (x + 0.044715 * (x * x * x)))))


def _cat_rows(parts):
    return parts[0] if len(parts) == 1 else jnp.concatenate(parts, axis=0)


def _delayed(seg, prev, d):
    n_prev = prev.shape[0]
    v = pltpu.roll(seg, d, 0)
    row = lax.broadcasted_iota(jnp.int32, (SUBLANES, 1), 0)
    top = v[0:SUBLANES]
    for r in range(d):
        top = jnp.where(row == r, prev[n_prev - d + r:n_prev - d + r + 1], top)
    return jnp.concatenate([top, v[SUBLANES:]], axis=0)


def _causal_conv(seg, prev, cw, cb):
    n_prev = cw.shape[0] - 1
    y = cb + _delayed(seg, prev, n_prev) * cw[0:1]
    for t in range(1, n_prev):
        y = y + _delayed(seg, prev, n_prev - t) * cw[t:t + 1]
    return y + seg * cw[n_prev:n_prev + 1]


def _scan_rows(a, b):
    n = a.shape[0]
    row = lax.broadcasted_iota(jnp.int32, (n, 1), 0)
    sh = 1
    while sh < min(SUBLANES, n):
        valid = row >= sh
        a_sh = pltpu.roll(a, sh, 0)
        b_sh = pltpu.roll(b, sh, 0)
        b = jnp.where(valid, b + a * b_sh, b)
        a = jnp.where(valid, a * a_sh, a)
        sh *= 2
    while sh < n:
        b = jnp.concatenate([b[:sh], b[sh:] + a[sh:] * b[:n - sh]], axis=0)
        a = jnp.concatenate([a[:sh], a[sh:] * a[:n - sh]], axis=0)
        sh *= 2
    return a, b


def _two_stage(n_chunks, first, second):
    for t in range(n_chunks + 1):
        if t < n_chunks:
            first(t)
        if t >= 1:
            second(t - 1)


def _mem_attn(q, gq, mk_ref, mv_ref, seg0, n_seg, seg_len):
    hd = gq.shape[-1]
    outs = []
    for h in range(q.shape[-1] // hd):
        cols = slice(h * hd, (h + 1) * hd)
        qn = _rms(q[:, cols], gq).astype(BF16)
        segs = []
        for s in range(n_seg):
            k = mk_ref[seg0 + s, :, cols]
            v = mv_ref[seg0 + s, :, cols]
            sc = _dot_nt(qn[s * seg_len:(s + 1) * seg_len], k) * (hd ** -0.5)
            p = jnp.exp(sc - jnp.max(sc, axis=-1, keepdims=True))
            pr = p / jnp.sum(p, axis=-1, keepdims=True)
            segs.append(_dot(pr.astype(BF16), v))
        outs.append(_cat_rows(segs))
    return outs[0] if len(outs) == 1 else jnp.concatenate(outs, axis=-1)


class _Tiling:
    def __init__(self, n_seq, t, bm):
        self.n_seq, self.t = n_seq, t
        if t >= bm:
            assert t % bm == 0
            self.n_seg, self.seg_len, self.tiles_per_seq = 1, bm, t // bm
        else:
            assert bm % t == 0 and n_seq % (bm // t) == 0
            self.n_seg, self.seg_len, self.tiles_per_seq = bm // t, t, 1
        self.bm = self.n_seg * self.seg_len
        self.n_tiles = n_seq * t // self.bm
        assert self.seg_len & (self.seg_len - 1) == 0 and self.seg_len >= SUBLANES


def _last_tile_state(per_tile, tl):
    rows, c = per_tile.shape[2:]
    return per_tile.reshape(tl.n_seq, tl.tiles_per_seq, rows, c)[:, -1]


def _resident(block_shape, index_map):
    return pl.BlockSpec(block_shape, index_map, pipeline_mode=pl.Buffered(1))


def _mem_kv_kernel(mem_ref, gn_ref, wk_ref, wv_ref, gk_ref, k_ref, v_ref, hn_ref):
    @pl.when(pl.program_id(1) == 0)
    def _():
        hn_ref[...] = _rms(mem_ref[...], gn_ref[...]).astype(BF16)

    hn = hn_ref[...]
    k_ref[...] = _rms(_dot(hn, wk_ref[...].astype(BF16)), gk_ref[...])
    v_ref[...] = _dot(hn, wv_ref[...].astype(BF16))


def _mem_kv(mem2d, mem_norm, w_kv, mem_k_norm):
    depth, d, two_w = w_kv.shape
    mem_w = two_w // 2
    hd = mem_w // MEM_HEADS
    m = mem2d.shape[0]
    return pl.pallas_call(
        _mem_kv_kernel,
        grid=(depth, MEM_HEADS),
        in_specs=[
            pl.BlockSpec((m, d), lambda l, j: (0, 0)),
            pl.BlockSpec((None, 1, d), lambda l, j: (l, 0, 0)),
            pl.BlockSpec((None, d, hd), lambda l, j: (l, 0, j)),
            pl.BlockSpec((None, d, hd), lambda l, j: (l, 0, MEM_HEADS + j)),
            pl.BlockSpec((None, 1, hd), lambda l, j: (l, 0, 0)),
        ],
        out_specs=[
            pl.BlockSpec((None, m, hd), lambda l, j: (l, 0, j)),
            pl.BlockSpec((None, m, hd), lambda l, j: (l, 0, j)),
        ],
        out_shape=[jax.ShapeDtypeStruct((depth, m, mem_w), F32)] * 2,
        scratch_shapes=[pltpu.VMEM((m, d), BF16)],
        compiler_params=_cparams(2),
        name="mem_kv",
    )(mem2d, mem_norm, w_kv, w_kv, mem_k_norm)


def _lru_in_kernel(x_ref, gn_ref, wg_ref, wx_ref, cw_ref, cb_ref, wa_ref, wi_ref, ba_ref, bi_ref,
                   lam_ref, h0_ref, c0_ref, gq_ref, mk_ref, mv_ref,
                   mix_ref, mo_ref, hn_out_ref, cn_out_ref,
                   hn_ref, hc_ref, cc_ref, raw_ref, *, n_seg, seg_len, tiles_per_seq, row_chunk):
    i = pl.program_id(0)
    j = pl.program_id(1)
    carried = tiles_per_seq > 1

    @pl.when(j == 0)
    def _():
        hn_ref[...] = _rms(x_ref[...], gn_ref[...]).astype(BF16)

    @pl.when(j < LRU_HEADS)
    def _():
        if carried:
            @pl.when(i % tiles_per_seq == 0)
            def _():
                cc_ref[j] = c0_ref[0]
                hc_ref[j] = h0_ref[0]

        cw = cw_ref[...]
        cb = cb_ref[...]
        wg, wx, wa, wi = (r[...].astype(BF16) for r in (wg_ref, wx_ref, wa_ref, wi_ref))
        n_prev = cw.shape[0] - 1
        n_chunks = hn_ref.shape[0] // row_chunk
        segs_per_chunk = row_chunk // seg_len if n_seg > 1 else 0
        whole = n_seg == 1
        state = {"conv": (cc_ref[j] if carried else c0_ref[0]) if whole else None,
                 "h": (hc_ref[j] if carried else h0_ref[0]) if whole else None}
        log_lam = jax.nn.log_sigmoid(lam_ref[...])

        def rows(c):
            return slice(c * row_chunk, (c + 1) * row_chunk)

        def in_proj(c):
            hn = hn_ref[rows(c), :]
            raw_ref[0, rows(c), :] = _dot(hn, wg)
            raw_ref[1, rows(c), :] = _dot(hn, wx)

        def conv_gates(c):
            xr = raw_ref[1, rows(c), :]
            if whole:
                xc = _causal_conv(xr, state["conv"], cw, cb)
                state["conv"] = xr[row_chunk - n_prev:row_chunk]
            else:
                parts = []
                for s in range(segs_per_chunk):
                    sidx = c * segs_per_chunk + s
                    seg = xr[s * seg_len:(s + 1) * seg_len]
                    parts.append(_causal_conv(seg, c0_ref[sidx], cw, cb))
                    cn_out_ref[sidx] = seg[seg_len - n_prev:seg_len]
                xc = _cat_rows(parts)
            xcb = xc.astype(BF16)
            raw_ref[1, rows(c), :] = xc
            raw_ref[2, rows(c), :] = _dot(xcb, wa)
            raw_ref[3, rows(c), :] = _dot(xcb, wi)

        def recur(c):
            xc = raw_ref[1, rows(c), :]
            r = jax.nn.sigmoid(raw_ref[2, rows(c), :] + ba_ref[...])
            ig = jax.nn.sigmoid(raw_ref[3, rows(c), :] + bi_ref[...])
            log_a = (LRU_C * r) * log_lam
            a = jnp.exp(log_a)
            one_minus_a2 = -jnp.tanh(log_a) * (a * a + 1.0)
            gated = jnp.sqrt(one_minus_a2) * (ig * xc)
            if whole:
                cum_a, cum_b = _scan_rows(a, gated)
                hs = cum_b + cum_a * state["h"]
                state["h"] = hs[row_chunk - 1:row_chunk]
            else:
                parts = []
                for s in range(segs_per_chunk):
                    sidx = c * segs_per_chunk + s
                    sl = slice(s * seg_len, (s + 1) * seg_len)
                    cum_a, cum_b = _scan_rows(a[sl], gated[sl])
                    seg_hs = cum_b + cum_a * h0_ref[sidx]
                    hn_out_ref[sidx] = seg_hs[seg_len - 1:seg_len]
                    parts.append(seg_hs)
                hs = _cat_rows(parts)
            mix_ref[rows(c), :] = (_gelu(raw_ref[0, rows(c), :]) * hs).astype(BF16)

        for t in range(n_chunks + 2):
            if t < n_chunks:
                in_proj(t)
            if 0 <= t - 1 < n_chunks:
                conv_gates(t - 1)
            if 0 <= t - 2 < n_chunks:
                recur(t - 2)
        if whole:
            cn_out_ref[0] = state["conv"]
            hn_out_ref[0] = state["h"]
            if carried:
                cc_ref[j] = state["conv"]
                hc_ref[j] = state["h"]

    @pl.when(j >= LRU_HEADS)
    def _():
        chunk = max(row_chunk, min(hn_ref.shape[0], MEM_ATTN_ROWS))
        n_chunks = hn_ref.shape[0] // chunk
        segs = (chunk // seg_len, seg_len) if n_seg > 1 else (1, chunk)

        wq = wx_ref[...].astype(BF16)

        def rows(c):
            return slice(c * chunk, (c + 1) * chunk)

        def project(c):
            raw_ref[0, rows(c), :] = _dot(hn_ref[rows(c), :], wq)

        def attend(c):
            seg0 = c * segs[0] if n_seg > 1 else 0
            mo_ref[rows(c), :] = _mem_attn(raw_ref[0, rows(c), :], gq_ref[...], mk_ref, mv_ref,
                                           seg0, *segs).astype(BF16)

        _two_stage(n_chunks, project, attend)


def _lru_in(x2d, tl, li, mi, p, h0, c0, mk, mv, row_chunk):
    m, d = x2d.shape
    hw = d // LRU_HEADS
    n_steps = LRU_HEADS + MEM_HEADS
    ns, tps = tl.n_seg, tl.tiles_per_seq
    width = p["lru_conv_w"].shape[1]
    last = LRU_HEADS - 1

    def hcol(j):
        return jnp.minimum(j, last)

    def mcol(j):
        return jnp.maximum(j - LRU_HEADS, 0)

    vec = pl.BlockSpec((None, 1, hw), lambda i, j: (mi, 0, hcol(j)))
    gate_w = pl.BlockSpec((None, None, hw, hw), lambda i, j: (mi, hcol(j), 0, 0))
    mem = pl.BlockSpec((None, ns, mk.shape[2], hw), lambda i, j: (li, i // tps, 0, mcol(j)))
    assert tl.bm % row_chunk == 0 and (row_chunk % tl.seg_len == 0 if ns > 1 else True)
    kern = functools.partial(_lru_in_kernel, n_seg=ns, seg_len=tl.seg_len, tiles_per_seq=tps,
                             row_chunk=row_chunk)
    return pl.pallas_call(
        kern,
        grid=(tl.n_tiles, n_steps),
        in_specs=[
            _resident((tl.bm, d), lambda i, j: (i, 0)),
            pl.BlockSpec((None, 1, d), lambda i, j: (li, 0, 0)),
            pl.BlockSpec((None, d, hw), lambda i, j: (mi, 0, hcol(j))),
            pl.BlockSpec((None, d, hw), lambda i, j: (mi, 0, LRU_HEADS + j)),
            pl.BlockSpec((None, width, hw), lambda i, j: (mi, 0, hcol(j))),
            vec, gate_w, gate_w, vec, vec, vec,
            pl.BlockSpec((None, ns, 1, hw), lambda i, j: (mi, i // tps, 0, hcol(j))),
            pl.BlockSpec((None, ns, width - 1, hw), lambda i, j: (mi, i // tps, 0, hcol(j))),
            pl.BlockSpec((None, 1, hw), lambda i, j: (li, 0, 0)),
            mem, mem,
        ],
        out_specs=[
            pl.BlockSpec((tl.bm, hw), lambda i, j: (i, hcol(j))),
            pl.BlockSpec((tl.bm, hw), lambda i, j: (i, mcol(j))),
            pl.BlockSpec((None, ns, 1, hw), lambda i, j: (i, 0, 0, hcol(j))),
            pl.BlockSpec((None, ns, width - 1, hw), lambda i, j: (i, 0, 0, hcol(j))),
        ],
        out_shape=[
            jax.ShapeDtypeStruct((m, d), BF16),
            jax.ShapeDtypeStruct((m, MEM_HEADS * hw), BF16),
            jax.ShapeDtypeStruct((tl.n_tiles, ns, 1, d), F32),
            jax.ShapeDtypeStruct((tl.n_tiles, ns, width - 1, d), F32),
        ],
        scratch_shapes=[
            pltpu.VMEM((tl.bm, d), BF16),
            pltpu.VMEM((LRU_HEADS, 1, hw), F32),
            pltpu.VMEM((LRU_HEADS, width - 1, hw), F32),
            pltpu.VMEM((4, tl.bm, hw), F32),
        ],
        compiler_params=_cparams(2),
        name="lru_in",
    )(x2d, p["norm_mix"], p["lru_w_in"], p["lru_w_in"], p["lru_conv_w"], p["lru_conv_b"],
      p["lru_gate_a_w"], p["lru_gate_x_w"], p["lru_gate_a_b"], p["lru_gate_x_b"], p["lru_lambda"],
      h0, c0, p["mem_q_norm"], mk, mv)


def _att_in_kernel(*refs, n_seg, seg_len, nq, row_chunk, aliased):
    if aliased:
        refs = refs[:8] + refs[10:]
    (x_ref, gn_ref, w_ref, qg_ref, kg_ref, gq_ref, mk_ref, mv_ref,
     qb_ref, kf_ref, kb_ref, vf_ref, vb_ref, mo_ref, hn_ref, z_ref) = refs
    j = pl.program_id(1)

    @pl.when(j == 0)
    def _():
        hn_ref[...] = _rms(x_ref[...], gn_ref[...]).astype(BF16)

    dk = qg_ref.shape[-1]
    n_groups = z_ref.shape[-1] // dk
    rows_per_t = hn_ref.shape[-1] // dk
    n_heads = rows_per_t // 2
    n_chunks = hn_ref.shape[0] // row_chunk
    w = w_ref[...].astype(BF16)

    def rows(c):
        return slice(c * row_chunk, (c + 1) * row_chunk)

    def project(c):
        z_ref[rows(c), :] = _dot(hn_ref[rows(c), :], w)

    def queries(c):
        qscale = dk ** -0.5 * LOG2E
        for g in range(n_groups):
            sl = slice(g * dk, (g + 1) * dk)
            qb_ref[rows(c), sl] = (_rms(z_ref[rows(c), sl], qg_ref[...]) * qscale).astype(BF16)

    def keys(c):
        kns = []
        for g in range(n_groups):
            sl = slice(g * dk, (g + 1) * dk)
            kn = _rms(z_ref[rows(c), sl], kg_ref[...])
            kb_ref[rows(c), sl] = kn.astype(BF16)
            kns.append(kn)
        first = pl.multiple_of((j - nq) * n_groups, n_groups)
        kf_ref[rows(c), pl.ds(first, n_groups), :] = (
            jnp.concatenate(kns, axis=-1).reshape(row_chunk, n_groups, dk))

    def values(c):
        z = z_ref[rows(c), :]
        vb_ref[rows(c), :] = z.astype(BF16)
        heads = n_groups // 2
        head0 = pl.multiple_of((j - 2 * nq) * heads, heads)
        for half in range(2):
            part = jnp.concatenate([z[:, (2 * h + half) * dk:(2 * h + half + 1) * dk] for h in range(heads)],
                                   axis=-1)
            vf_ref[rows(c), pl.ds(half * n_heads + head0, heads), :] = part.reshape(row_chunk, heads, dk)

    mem_chunk = max(row_chunk, min(hn_ref.shape[0], MEM_ATTN_ROWS))
    mem_segs = (mem_chunk // seg_len, seg_len) if n_seg > 1 else (1, mem_chunk)

    def mem_rows(c):
        return slice(c * mem_chunk, (c + 1) * mem_chunk)

    def mem_project(c):
        z_ref[mem_rows(c), :] = _dot(hn_ref[mem_rows(c), :], w)

    def memory(c):
        seg0 = c * mem_segs[0] if n_seg > 1 else 0
        mo_ref[mem_rows(c), :] = _mem_attn(z_ref[mem_rows(c), :], gq_ref[...], mk_ref, mv_ref,
                                           seg0, *mem_segs).astype(BF16)

    @pl.when(j < nq)
    def _():
        _two_stage(n_chunks, project, queries)

    @pl.when((j >= nq) & (j < 2 * nq))
    def _():
        _two_stage(n_chunks, project, keys)

    @pl.when((j >= 2 * nq) & (j < 3 * nq))
    def _():
        _two_stage(n_chunks, project, values)

    @pl.when(j >= 3 * nq)
    def _():
        _two_stage(hn_ref.shape[0] // mem_chunk, mem_project, memory)


def _att_in(x2d, tl, li, ai, p, mk, mv, bc, row_chunk, kv_all):
    m, d = x2d.shape
    mem_w = mk.shape[-1]
    nq, nm = d // bc, mem_w // bc
    n_steps = 3 * nq + nm
    ns, tps = tl.n_seg, tl.tiles_per_seq
    dk = p["attn_q_norm"].shape[-1]
    hd = p["mem_q_norm"].shape[-1]

    def col(lo, n):
        return lambda i, j: (i, jnp.clip(j - lo, 0, n - 1))

    mem = _resident((None, ns, mk.shape[2], bc),
                    lambda i, j: (li, i // tps, 0, jnp.clip(j - 3 * nq, 0, nm - 1)))
    assert tl.bm % row_chunk == 0 and (row_chunk % tl.seg_len == 0 if ns > 1 else True)
    n_attn = p["attn_w_in"].shape[0]
    rows_per_t = d // dk
    aliased = kv_all is not None
    kern = functools.partial(_att_in_kernel, n_seg=ns, seg_len=tl.seg_len, nq=nq, row_chunk=row_chunk,
                             aliased=aliased)
    kv_spec = _resident((None, tl.bm, rows_per_t, dk), lambda i, j: (ai, i, 0, 0))
    kv_shape = jax.ShapeDtypeStruct((n_attn, m, rows_per_t, dk), F32)
    extra_in = [pl.BlockSpec(memory_space=pl.ANY)] * 2 if aliased else []
    return pl.pallas_call(
        kern,
        grid=(tl.n_tiles, n_steps),
        in_specs=[
            _resident((tl.bm, d), lambda i, j: (i, 0)),
            pl.BlockSpec((None, 1, d), lambda i, j: (li, 0, 0)),
            pl.BlockSpec((None, d, bc), lambda i, j: (ai, 0, j)),
            pl.BlockSpec((None, 1, dk), lambda i, j: (ai, 0, 0)),
            pl.BlockSpec((None, 1, dk), lambda i, j: (ai, 0, 0)),
            pl.BlockSpec((None, 1, hd), lambda i, j: (li, 0, 0)),
            mem, mem, *extra_in,
        ],
        out_specs=[
            pl.BlockSpec((tl.bm, bc), col(0, nq)),
            kv_spec,
            pl.BlockSpec((tl.bm, bc), col(nq, nq)),
            kv_spec,
            pl.BlockSpec((tl.bm, bc), col(2 * nq, nq)),
            pl.BlockSpec((tl.bm, bc), col(3 * nq, nm)),
        ],
        out_shape=[
            jax.ShapeDtypeStruct((m, d), BF16),
            kv_shape,
            jax.ShapeDtypeStruct((m, d), BF16),
            kv_shape,
            jax.ShapeDtypeStruct((m, d), BF16),
            jax.ShapeDtypeStruct((m, mem_w), BF16),
        ],
        input_output_aliases={8: 1, 9: 3} if aliased else {},
        scratch_shapes=[pltpu.VMEM((tl.bm, d), BF16), pltpu.VMEM((tl.bm, bc), F32)],
        compiler_params=_cparams(2),
        name="att_in",
    )(x2d, p["norm_mix"], p["attn_w_in"], p["attn_q_norm"], p["attn_k_norm"], p["mem_q_norm"], mk, mv,
      *(kv_all if aliased else ()))


def _attn_block(q, k, v, m_ref, l_ref, acc_ref, mask, lanes=slice(None)):
    dk = q.shape[-1] // 2
    tk = k.shape[0]
    scores = [_dot_nt(k[:, c * dk:(c + 1) * dk], q[:, c * dk:(c + 1) * dk]) for c in range(2)]
    for c in range(2):
        s = scores[c]
        if mask is not None:
            s = jnp.where(mask, s, NEG_BIG)
        m_old = m_ref[c, :, lanes]
        m_new = jnp.maximum(m_old, jnp.max(s, axis=0, keepdims=True))
        alpha = jnp.exp2(m_old - m_new)
        p = jnp.exp2(s - m_new)
        l_ref[c, :, lanes] = (alpha * l_ref[c, :, lanes]
                              + jnp.sum(p.reshape(tk // SUBLANES, SUBLANES, p.shape[-1]), axis=0))
        pv = lax.dot_general(v, p.astype(BF16), (((0,), (0,)), ((), ())), preferred_element_type=F32)
        acc_ref[c, :, lanes] = alpha * acc_ref[c, :, lanes] + pv
        m_ref[c, :, lanes] = m_new


def _attn_init(m_ref, l_ref, acc_ref):
    m_ref[...] = jnp.full(m_ref.shape, NEG_BIG, F32)
    l_ref[...] = jnp.zeros(l_ref.shape, F32)
    acc_ref[...] = jnp.zeros(acc_ref.shape, F32)


def _attn_finish(lp_ref, sg_ref, l_ref, acc_ref, lam_init):
    lp = lp_ref[...]
    lam = (jnp.exp(jnp.sum(lp[0:1] * lp[1:2], axis=-1, keepdims=True))
           - jnp.exp(jnp.sum(lp[2:3] * lp[3:4], axis=-1, keepdims=True)) + lam_init)
    l0 = jnp.sum(l_ref[0], axis=0, keepdims=True)
    l1 = jnp.sum(l_ref[1], axis=0, keepdims=True)
    o = (acc_ref[0] / l0 - lam * (acc_ref[1] / l1)).T
    return (_rms(o, sg_ref[...]) * (1.0 - lam_init)).astype(BF16)


def _attn_prompt_kernel(lp_ref, sg_ref, q_ref, k_ref, v_ref, o_ref, m_ref, l_ref, acc_ref, *, tq, tk, lam_init):
    qi = pl.program_id(2)
    _attn_init(m_ref, l_ref, acc_ref)
    q = q_ref[...]
    per = tk // tq

    def visible(start, size):
        start = pl.multiple_of(start, size)
        _attn_block(q, k_ref[pl.ds(start, size), :], v_ref[pl.ds(start, size), :], m_ref, l_ref, acc_ref, None)

    def body(kv, carry):
        visible(kv * tk, tk)
        return carry

    n_big = qi // per
    lax.fori_loop(0, n_big, body, 0)
    for r in range(1, per):
        @pl.when(qi % per >= r)
        def _():
            visible(n_big * tk + (r - 1) * tq, tq)

    half = tq // 2
    shift = CHUNK.bit_length() - 1

    def chunk_mask(n_keys, n_queries):
        keyc = lax.shift_right_logical(lax.broadcasted_iota(jnp.int32, (n_keys, n_queries), 0), shift)
        qryc = lax.shift_right_logical(lax.broadcasted_iota(jnp.int32, (n_keys, n_queries), 1), shift)
        return keyc <= qryc

    start = pl.multiple_of(qi * tq, tq)
    _attn_block(q, k_ref[pl.ds(start, half), :], v_ref[pl.ds(start, half), :], m_ref, l_ref, acc_ref,
                chunk_mask(half, tq))
    start = pl.multiple_of(qi * tq + half, half)
    _attn_block(q[half:], k_ref[pl.ds(start, half), :], v_ref[pl.ds(start, half), :], m_ref, l_ref, acc_ref,
                chunk_mask(half, half), lanes=slice(half, tq))
    o_ref[...] = _attn_finish(lp_ref, sg_ref, l_ref, acc_ref, lam_init)


def _attn_prompt(qb, kb, vb, ai, p, n_seq, t, tq, tk, lam_init):
    d = qb.shape[-1]
    hw = d // DIFF_HEADS
    assert t % tq == 0 and tq % (2 * CHUNK) == 0 and tk % tq == 0
    q3, k3, v3 = (a.reshape(n_seq, t, d) for a in (qb, kb, vb))
    lp, sg = p["attn_lambda"], p["attn_subln"]
    kern = functools.partial(_attn_prompt_kernel, tq=tq, tk=tk, lam_init=lam_init)
    out = pl.pallas_call(
        kern,
        grid=(n_seq, DIFF_HEADS, t // tq),
        in_specs=[
            pl.BlockSpec((None,) + lp.shape[1:], lambda b, h, qi: (ai, 0, 0)),
            pl.BlockSpec((None, 1, hw), lambda b, h, qi: (ai, 0, 0)),
            pl.BlockSpec((None, tq, hw), lambda b, h, qi: (b, qi, h)),
            pl.BlockSpec((None, t, hw), lambda b, h, qi: (b, 0, h)),
            pl.BlockSpec((None, t, hw), lambda b, h, qi: (b, 0, h)),
        ],
        out_specs=pl.BlockSpec((None, tq, hw), lambda b, h, qi: (b, qi, h)),
        out_shape=jax.ShapeDtypeStruct((n_seq, t, d), BF16),
        scratch_shapes=[
            pltpu.VMEM((2, 1, tq), F32),
            pltpu.VMEM((2, SUBLANES, tq), F32),
            pltpu.VMEM((2, hw, tq), F32),
        ],
        compiler_params=_cparams(3),
        name="attn_prompt",
    )(lp, sg, q3, k3, v3)
    return out.reshape(n_seq * t, d)


def _attn_cached_kernel(lp_ref, sg_ref, q_ref, kp_ref, vp_ref, kn_ref, vn_ref, o_ref,
                        m_ref, l_ref, acc_ref, kbuf_ref, vbuf_ref, *, lam_init):
    n_heads = DIFF_HEADS
    past, rows_per_t, dk = kp_ref.shape
    hw = 2 * dk
    group = SUBLANES
    for g in range(rows_per_t // group):
        fold = slice(g * group * dk, (g + 1) * group * dk)
        kbuf_ref[:, fold] = kp_ref[:, g * group:(g + 1) * group, :].reshape(past, group * dk).astype(BF16)
        vbuf_ref[:, fold] = vp_ref[:, g * group:(g + 1) * group, :].reshape(past, group * dk).astype(BF16)
    for h in range(n_heads):
        cols = slice(h * hw, (h + 1) * hw)
        v = jnp.concatenate([vbuf_ref[:, (c * n_heads + h) * dk:(c * n_heads + h + 1) * dk] for c in range(2)],
                            axis=-1)
        q = q_ref[:, cols]
        _attn_init(m_ref, l_ref, acc_ref)
        _attn_block(q, kbuf_ref[:, cols], v, m_ref, l_ref, acc_ref, None)
        _attn_block(q, kn_ref[:, cols], vn_ref[:, cols], m_ref, l_ref, acc_ref, None)
        o_ref[:, cols] = _attn_finish(lp_ref, sg_ref, l_ref, acc_ref, lam_init)


def _attn_cached(qb, kb, vb, k_past, v_past, ai, p, n_seq, t, lam_init):
    d = qb.shape[-1]
    hw = d // DIFF_HEADS
    n_attn, _, past, n_heads, _, dk = k_past.shape
    assert past % CHUNK == 0 and t <= CHUNK and n_heads == DIFF_HEADS and 2 * dk == hw
    q3, k3, v3 = (a.reshape(n_seq, t, d) for a in (qb, kb, vb))
    lp, sg = p["attn_lambda"], p["attn_subln"]
    k_rows = k_past.reshape(n_attn * n_seq, past, 2 * n_heads, dk)
    v_rows = v_past.reshape(n_attn, n_seq, past, n_heads, 2, dk).transpose(0, 1, 2, 4, 3, 5)
    v_rows = v_rows.reshape(n_attn * n_seq, past, 2 * n_heads, dk)
    kern = functools.partial(_attn_cached_kernel, lam_init=lam_init)
    new_spec = pl.BlockSpec((None, t, d), lambda b: (b, 0, 0))
    past_spec = pl.BlockSpec((None, past, 2 * n_heads, dk), lambda b: (ai * n_seq + b, 0, 0, 0))
    out = pl.pallas_call(
        kern,
        grid=(n_seq,),
        in_specs=[
            pl.BlockSpec((None,) + lp.shape[1:], lambda b: (ai, 0, 0)),
            pl.BlockSpec((None, 1, hw), lambda b: (ai, 0, 0)),
            new_spec, past_spec, past_spec, new_spec, new_spec,
        ],
        out_specs=new_spec,
        out_shape=jax.ShapeDtypeStruct((n_seq, t, d), BF16),
        scratch_shapes=[
            pltpu.VMEM((2, 1, t), F32),
            pltpu.VMEM((2, SUBLANES, t), F32),
            pltpu.VMEM((2, hw, t), F32),
            pltpu.VMEM((past, d), BF16),
            pltpu.VMEM((past, d), BF16),
        ],
        compiler_params=_cparams(1),
        name="attn_cached",
    )(lp, sg, q3, k_rows, v_rows, k3, v3)
    return out.reshape(n_seq * t, d)


def _out_kernel(x_ref, mix_ref, mo_ref, w1_ref, w2_ref, o_ref):
    o_ref[...] = x_ref[...] + (_dot(mix_ref[...], w1_ref[...]) + _dot(mo_ref[...], w2_ref[...]))


def _out_proj(x2d, mix, mo, w_out, wi, bm, bn):
    m, d = x2d.shape
    k1, k2 = mix.shape[1], mo.shape[1]
    assert k1 % k2 == 0
    return pl.pallas_call(
        _out_kernel,
        grid=(m // bm, d // bn),
        in_specs=[
            pl.BlockSpec((bm, bn), lambda i, j: (i, j)),
            pl.BlockSpec((bm, k1), lambda i, j: (i, 0)),
            pl.BlockSpec((bm, k2), lambda i, j: (i, 0)),
            pl.BlockSpec((None, k1, bn), lambda i, j: (wi, 0, j)),
            pl.BlockSpec((None, k2, bn), lambda i, j: (wi, k1 // k2, j)),
        ],
        out_specs=pl.BlockSpec((bm, bn), lambda i, j: (i, j)),
        out_shape=jax.ShapeDtypeStruct((m, d), F32),
        compiler_params=_cparams(2),
        name="out_proj",
    )(x2d, mix, mo, w_out, w_out)


def _ffn_kernel(x_ref, gn_ref, wg_ref, wv_ref, cwg_ref, cwv_ref, cbg_ref, cbv_ref, stg_ref, stv_ref, wdn_ref,
                o_ref, nstg_ref, nstv_ref, hn_ref, cg_ref, cv_ref, raw_ref, act_ref,
                *, n_seg, seg_len, tiles_per_seq, row_chunk):
    i = pl.program_id(0)
    j = pl.program_id(1)
    carried = tiles_per_seq > 1

    @pl.when(j == 0)
    def _():
        x = x_ref[...]
        hn_ref[...] = _rms(x, gn_ref[...]).astype(BF16)
        o_ref[...] = x

    if carried:
        @pl.when(i % tiles_per_seq == 0)
        def _():
            cg_ref[j] = stg_ref[0]
            cv_ref[j] = stv_ref[0]

    halves = ((wg_ref, cwg_ref, cbg_ref, stg_ref, cg_ref, nstg_ref),
              (wv_ref, cwv_ref, cbv_ref, stv_ref, cv_ref, nstv_ref))
    n_prev = cwg_ref.shape[0] - 1
    n_chunks = hn_ref.shape[0] // row_chunk
    segs_per_chunk = row_chunk // seg_len if n_seg > 1 else 0
    prevs = [(h[4][j] if carried else h[3][0]) if n_seg == 1 else None for h in halves]

    def rows(c):
        return slice(c * row_chunk, (c + 1) * row_chunk)

    def up_proj(c):
        hn = hn_ref[rows(c), :]
        for hi, h in enumerate(halves):
            raw_ref[hi, rows(c), :] = _dot(hn, h[0][...])

    def conv_gate(c):
        ys = []
        for hi, (_, cw_ref, cb_ref, st_ref, _, nst_ref) in enumerate(halves):
            up = raw_ref[hi, rows(c), :]
            cw = cw_ref[...]
            cb = cb_ref[...]
            if n_seg == 1:
                ys.append(_causal_conv(up, prevs[hi], cw, cb))
                prevs[hi] = up[row_chunk - n_prev:row_chunk]
            else:
                parts = []
                for s in range(segs_per_chunk):
                    seg = up[s * seg_len:(s + 1) * seg_len]
                    parts.append(_causal_conv(seg, st_ref[c * segs_per_chunk + s], cw, cb))
                    nst_ref[c * segs_per_chunk + s] = seg[seg_len - n_prev:seg_len]
                ys.append(_cat_rows(parts))
        act_ref[rows(c), :] = (_gelu(ys[0]) * ys[1]).astype(BF16)

    def down_proj(c):
        o_ref[rows(c), :] += _dot(act_ref[rows(c), :], wdn_ref[...])

    for t in range(n_chunks + 2):
        if t < n_chunks:
            up_proj(t)
        if 0 <= t - 1 < n_chunks:
            conv_gate(t - 1)
        if 0 <= t - 2 < n_chunks:
            down_proj(t - 2)
    if n_seg == 1:
        for hi, (_, _, _, _, carry_ref, nst_ref) in enumerate(halves):
            nst_ref[0] = prevs[hi]
            if carried:
                carry_ref[j] = prevs[hi]


def _ffn(x2d, tl, li, p, state, bf, row_chunk):
    m, d = x2d.shape
    dff = p["ffn_w_down"].shape[1]
    nj = dff // bf
    ns, tps = tl.n_seg, tl.tiles_per_seq
    width = p["ffn_conv_w"].shape[1]
    assert tl.bm % row_chunk == 0 and (row_chunk % tl.seg_len == 0 if ns > 1 else True)
    kern = functools.partial(_ffn_kernel, n_seg=ns, seg_len=tl.seg_len, tiles_per_seq=tps, row_chunk=row_chunk)

    def halves(block, index_map):
        gate = pl.BlockSpec(block, lambda i, j: index_map(i, j, j))
        value = pl.BlockSpec(block, lambda i, j: index_map(i, j, nj + j))
        return [gate, value]

    nst_spec = pl.BlockSpec((None, ns, width - 1, bf), lambda i, j: (i, 0, 0, j))
    nst_shape = jax.ShapeDtypeStruct((tl.n_tiles, ns, width - 1, dff), F32)
    return pl.pallas_call(
        kern,
        grid=(tl.n_tiles, nj),
        in_specs=[
            _resident((tl.bm, d), lambda i, j: (i, 0)),
            pl.BlockSpec((None, 1, d), lambda i, j: (li, 0, 0)),
            *halves((None, d, bf), lambda i, j, c: (li, 0, c)),
            *halves((None, width, bf), lambda i, j, c: (li, 0, c)),
            *halves((None, 1, bf), lambda i, j, c: (li, 0, c)),
            *halves((None, ns, width - 1, bf), lambda i, j, c: (li, i // tps, 0, c)),
            pl.BlockSpec((None, bf, d), lambda i, j: (li, j, 0)),
        ],
        out_specs=[pl.BlockSpec((tl.bm, d), lambda i, j: (i, 0)), nst_spec, nst_spec],
        out_shape=[jax.ShapeDtypeStruct((m, d), F32), nst_shape, nst_shape],
        scratch_shapes=[
            pltpu.VMEM((tl.bm, d), BF16),
            pltpu.VMEM((nj, width - 1, bf), F32),
            pltpu.VMEM((nj, width - 1, bf), F32),
            pltpu.VMEM((2, tl.bm, bf), F32),
            pltpu.VMEM((tl.bm, bf), BF16),
        ],
        compiler_params=_cparams(2),
        name="ffn",
    )(x2d, p["norm_ffn"], p["ffn_w_up"], p["ffn_w_up"], p["ffn_conv_w"], p["ffn_conv_w"],
      p["ffn_conv_b"], p["ffn_conv_b"], state, state, p["ffn_w_down"])


def _run_trunk(x, mem_k, mem_v, lru_h, lru_conv, attn_past, ffn_conv, p, cfg):
    n_seq, t, d = x.shape
    depth = p["norm_mix"].shape[0]
    x2d = x.reshape(n_seq * t, d)
    tl = _Tiling(n_seq, t, cfg["bm"])
    tl_ffn = _Tiling(n_seq, t, cfg["bm_ffn"])
    mem_tokens = mem_k.shape[2]
    mk = mem_k.reshape(depth, n_seq, mem_tokens, -1).astype(BF16)
    mv = mem_v.reshape(depth, n_seq, mem_tokens, -1).astype(BF16)
    h0 = lru_h[:, :, None, :]
    new_h, new_lconv, new_fconv = [], [], []
    kv_all = None
    for i in range(depth):
        j = i // 2
        if i % 2 == 0:
            mix, mo, h_last, buf = _lru_in(x2d, tl, i, j, p, h0, lru_conv, mk, mv, cfg["lru_rows"])
            new_h.append(_last_tile_state(h_last, tl)[:, 0, :])
            new_lconv.append(_last_tile_state(buf, tl))
            w_out = p["lru_w_out"]
        else:
            qb, k_all, kb, v_all, vb, mo = _att_in(x2d, tl, i, j, p, mk, mv, cfg["bc"], cfg["att_rows"], kv_all)
            kv_all = (k_all, v_all)
            lam_init = 0.8 - 0.6 * math.exp(-0.3 * i)
            if attn_past is None:
                mix = _attn_prompt(qb, kb, vb, j, p, n_seq, t, cfg["tq"], cfg["tk"], lam_init)
            else:
                mix = _attn_cached(qb, kb, vb, attn_past[0], attn_past[1], j, p, n_seq, t, lam_init)
            w_out = p["attn_w_out"]
        x2d = _out_proj(x2d, mix, mo, w_out, j, cfg["bm_out"], cfg["bn_out"])
        x2d, fg, fv = _ffn(x2d, tl_ffn, i, p, ffn_conv, cfg["bf"], cfg["ffn_rows"])
        new_fconv.append(jnp.concatenate([_last_tile_state(fg, tl_ffn), _last_tile_state(fv, tl_ffn)], axis=-1))
    return x2d.reshape(n_seq, t, d), new_h, new_lconv, kv_all[0], kv_all[1], new_fconv


def _forward(x_prompt, x_sample, mem_prompt, cache_attn_k, cache_attn_v, cache_mem_k, cache_mem_v,
             state_lru_h, state_lru_conv, state_ffn_conv, p, mem_norm, mem_w_kv, mem_k_norm, cfg_p, cfg_s):
    b, t, d = x_prompt.shape
    depth = p["norm_mix"].shape[0]
    n_lru, n_attn = (depth + 1) // 2, depth // 2

    p = dict(p)
    for name in ("ffn_w_up", "ffn_w_down", "lru_w_out", "attn_w_out"):
        p[name] = p[name].astype(BF16)
    for name in ("norm_mix", "norm_ffn", "lru_conv_b", "lru_gate_a_b", "lru_gate_x_b", "lru_lambda",
                 "attn_q_norm", "attn_k_norm", "attn_subln", "mem_q_norm", "ffn_conv_b"):
        p[name] = p[name][:, None, :]

    mem_tokens = mem_prompt.shape[1]
    mk, mv = _mem_kv(mem_prompt.reshape(b * mem_tokens, d), mem_norm[:, None, :], mem_w_kv,
                     mem_k_norm[:, None, :])
    hd = mem_k_norm.shape[-1]
    p_mem_k = mk.reshape(depth, b, mem_tokens, MEM_HEADS, hd)
    p_mem_v = mv.reshape(depth, b, mem_tokens, MEM_HEADS, hd)

    zeros_h = jnp.zeros((n_lru, b, d), F32)
    zeros_lconv = jnp.zeros((n_lru, b) + state_lru_conv.shape[2:], F32)
    zeros_fconv = jnp.zeros((depth, b) + state_ffn_conv.shape[2:], F32)
    yp, ph, plc, pk, pv, pfc = _run_trunk(x_prompt, p_mem_k, p_mem_v, zeros_h, zeros_lconv, None,
                                          zeros_fconv, p, cfg_p)
    ys, sh, slc, sk, sv, sfc = _run_trunk(x_sample, cache_mem_k, cache_mem_v, state_lru_h, state_lru_conv,
                                          (cache_attn_k, cache_attn_v), state_ffn_conv, p, cfg_s)

    dk = p["attn_q_norm"].shape[-1]
    db, dt = x_sample.shape[0], x_sample.shape[1]

    def kshape(a, n, tt):
        return a.reshape(n_attn, n, tt, DIFF_HEADS, 2, dk)

    def vshape(a, n, tt):
        a = a.reshape(n_attn, n, tt, 2, DIFF_HEADS, dk)
        return a.transpose(0, 1, 2, 4, 3, 5).reshape(n_attn, n, tt, DIFF_HEADS, 2 * dk)

    return (yp, ys, jnp.stack(ph), jnp.stack(plc), kshape(pk, b, t), vshape(pv, b, t), p_mem_k, p_mem_v,
            jnp.stack(pfc), jnp.stack(sh), jnp.stack(slc), kshape(sk, db, dt), vshape(sv, db, dt),
            jnp.stack(sfc))


CFG_PROMPT = dict(bm=1024, bc=512, lru_rows=128, att_rows=256, bm_ffn=1024, bf=512, ffn_rows=512,
                  bm_out=1024, bn_out=1024, tq=1024, tk=1024)
CFG_SAMPLE = dict(bm=1024, bc=512, lru_rows=128, att_rows=256, bm_ffn=1024, bf=512, ffn_rows=512,
                  bm_out=1024, bn_out=1024)


def kernel(x_prompt, x_sample, mem_prompt, cache_attn_k, cache_attn_v, cache_mem_k, cache_mem_v, state_lru_h, state_lru_conv, state_ffn_conv, norm_mix, norm_ffn, lru_w_in, lru_conv_w, lru_conv_b, lru_gate_a_w, lru_gate_a_b, lru_gate_x_w, lru_gate_x_b, lru_lambda, lru_w_out, attn_w_in, attn_q_norm, attn_k_norm, attn_lambda, attn_subln, attn_w_out, mem_norm, mem_w_kv, mem_q_norm, mem_k_norm, ffn_w_up, ffn_conv_w, ffn_conv_b, ffn_w_down):
    p = {
        "norm_mix": norm_mix, "norm_ffn": norm_ffn,
        "lru_w_in": lru_w_in, "lru_conv_w": lru_conv_w, "lru_conv_b": lru_conv_b,
        "lru_gate_a_w": lru_gate_a_w, "lru_gate_a_b": lru_gate_a_b,
        "lru_gate_x_w": lru_gate_x_w, "lru_gate_x_b": lru_gate_x_b,
        "lru_lambda": lru_lambda, "lru_w_out": lru_w_out,
        "attn_w_in": attn_w_in, "attn_q_norm": attn_q_norm, "attn_k_norm": attn_k_norm,
        "attn_lambda": attn_lambda, "attn_subln": attn_subln, "attn_w_out": attn_w_out,
        "mem_q_norm": mem_q_norm,
        "ffn_w_up": ffn_w_up, "ffn_conv_w": ffn_conv_w, "ffn_conv_b": ffn_conv_b, "ffn_w_down": ffn_w_down,
    }
    return _forward(x_prompt, x_sample, mem_prompt, cache_attn_k, cache_attn_v, cache_mem_k, cache_mem_v,
                    state_lru_h, state_lru_conv, state_ffn_conv, p, mem_norm, mem_w_kv, mem_k_norm,
                    CFG_PROMPT, CFG_SAMPLE)
```

```python
import functools
import math

import jax
import jax.numpy as jnp
from jax import lax
from jax.experimental import pallas as pl
from jax.experimental.pallas import tpu as pltpu

F32 = jnp.float32
BF16 = jnp.bfloat16

CHUNK = 64
LRU_HEADS = 8
LRU_C = 8.0
DIFF_HEADS = 8
MEM_HEADS = 4
RMS_EPS = 1e-6
NEG_BIG = -1e30
LOG2E = 1.4426950408889634
SUBLANES = 8
MEM_ATTN_ROWS = 512

V7X_VMEM_LIMIT_BYTES = 58 * 1024 * 1024


def _cparams(n_grid_axes):
    return pltpu.CompilerParams(
        dimension_semantics=("arbitrary",) * n_grid_axes,
        vmem_limit_bytes=V7X_VMEM_LIMIT_BYTES,
    )


def _dot(a, b):
    return jnp.dot(a, b, preferred_element_type=F32)


def _dot_nt(a, b):
    return lax.dot_general(a, b, (((1,), (1,)), ((), ())), preferred_element_type=F32)


def _rms(x, g):
    return x * lax.rsqrt(jnp.mean(x * x, axis=-1, keepdims=True) + RMS_EPS) * g


def _gelu(x):
    return x * (0.5 * (1.0 + jnp.tanh(0.7978845608028654 * (x + 0.044715 * (x * x * x)))))


def _cat_rows(parts):
    return parts[0] if len(parts) == 1 else jnp.concatenate(parts, axis=0)


def _delayed(seg, prev, d):
    n_prev = prev.shape[0]
    v = pltpu.roll(seg, d, 0)
    row = lax.broadcasted_iota(jnp.int32, (SUBLANES, 1), 0)
    top = v[0:SUBLANES]
    for r in range(d):
        top = jnp.where(row == r, prev[n_prev - d + r:n_prev - d + r + 1], top)
    return jnp.concatenate([top, v[SUBLANES:]], axis=0)


def _causal_conv(seg, prev, cw, cb):
    n_prev = cw.shape[0] - 1
    y = cb + _delayed(seg, prev, n_prev) * cw[0:1]
    for t in range(1, n_prev):
        y = y + _delayed(seg, prev, n_prev - t) * cw[t:t + 1]
    return y + seg * cw[n_prev:n_prev + 1]


def _scan_rows(a, b):
    n = a.shape[0]
    row = lax.broadcasted_iota(jnp.int32, (n, 1), 0)
    sh = 1
    while sh < min(SUBLANES, n):
        valid = row >= sh
        a_sh = pltpu.roll(a, sh, 0)
        b_sh = pltpu.roll(b, sh, 0)
        b = jnp.where(valid, b + a * b_sh, b)
        a = jnp.where(valid, a * a_sh, a)
        sh *= 2
    while sh < n:
        b = jnp.concatenate([b[:sh], b[sh:] + a[sh:] * b[:n - sh]], axis=0)
        a = jnp.concatenate([a[:sh], a[sh:] * a[:n - sh]], axis=0)
        sh *= 2
    return a, b


def _two_stage(n_chunks, first, second):
    for t in range(n_chunks + 1):
        if t < n_chunks:
            first(t)
        if t >= 1:
            second(t - 1)


def _mem_attn(q, gq, mk_ref, mv_ref, seg0, n_seg, seg_len):
    hd = gq.shape[-1]
    outs = []
    for h in range(q.shape[-1] // hd):
        cols = slice(h * hd, (h + 1) * hd)
        qn = _rms(q[:, cols], gq).astype(BF16)
        segs = []
        for s in range(n_seg):
            k = mk_ref[seg0 + s, :, cols]
            v = mv_ref[seg0 + s, :, cols]
            sc = _dot_nt(qn[s * seg_len:(s + 1) * seg_len], k) * (hd ** -0.5)
            p = jnp.exp(sc - jnp.max(sc, axis=-1, keepdims=True))
            pr = p / jnp.sum(p, axis=-1, keepdims=True)
            segs.append(_dot(pr.astype(BF16), v))
        outs.append(_cat_rows(segs))
    return outs[0] if len(outs) == 1 else jnp.concatenate(outs, axis=-1)


class _Tiling:
    def __init__(self, n_seq, t, bm):
        self.n_seq, self.t = n_seq, t
        if t >= bm:
            assert t % bm == 0
            self.n_seg, self.seg_len, self.tiles_per_seq = 1, bm, t // bm
        else:
            assert bm % t == 0 and n_seq % (bm // t) == 0
            self.n_seg, self.seg_len, self.tiles_per_seq = bm // t, t, 1
        self.bm = self.n_seg * self.seg_len
        self.n_tiles = n_seq * t // self.bm
        assert self.seg_len & (self.seg_len - 1) == 0 and self.seg_len >= SUBLANES


def _last_tile_state(per_tile, tl):
    rows, c = per_tile.shape[2:]
    return per_tile.reshape(tl.n_seq, tl.tiles_per_seq, rows, c)[:, -1]


def _resident(block_shape, index_map):
    return pl.BlockSpec(block_shape, index_map, pipeline_mode=pl.Buffered(1))


def _mem_kv_kernel(mem_ref, gn_ref, wk_ref, wv_ref, gk_ref, k_ref, v_ref, hn_ref):
    @pl.when(pl.program_id(1) == 0)
    def _():
        hn_ref[...] = _rms(mem_ref[...], gn_ref[...]).astype(BF16)

    hn = hn_ref[...]
    k_ref[...] = _rms(_dot(hn, wk_ref[...].astype(BF16)), gk_ref[...])
    v_ref[...] = _dot(hn, wv_ref[...].astype(BF16))


def _mem_kv(mem2d, mem_norm, w_kv, mem_k_norm):
    depth, d, two_w = w_kv.shape
    mem_w = two_w // 2
    hd = mem_w // MEM_HEADS
    m = mem2d.shape[0]
    return pl.pallas_call(
        _mem_kv_kernel,
        grid=(depth, MEM_HEADS),
        in_specs=[
            pl.BlockSpec((m, d), lambda l, j: (0, 0)),
            pl.BlockSpec((None, 1, d), lambda l, j: (l, 0, 0)),
            pl.BlockSpec((None, d, hd), lambda l, j: (l, 0, j)),
            pl.BlockSpec((None, d, hd), lambda l, j: (l, 0, MEM_HEADS + j)),
            pl.BlockSpec((None, 1, hd), lambda l, j: (l, 0, 0)),
        ],
        out_specs=[
            pl.BlockSpec((None, m, hd), lambda l, j: (l, 0, j)),
            pl.BlockSpec((None, m, hd), lambda l, j: (l, 0, j)),
        ],
        out_shape=[jax.ShapeDtypeStruct((depth, m, mem_w), F32)] * 2,
        scratch_shapes=[pltpu.VMEM((m, d), BF16)],
        compiler_params=_cparams(2),
        name="mem_kv",
    )(mem2d, mem_norm, w_kv, w_kv, mem_k_norm)


def _lru_in_kernel(x_ref, gn_ref, wg_ref, wx_ref, cw_ref, cb_ref, wa_ref, wi_ref, ba_ref, bi_ref,
                   lam_ref, h0_ref, c0_ref, gq_ref, mk_ref, mv_ref,
                   mix_ref, mo_ref, hn_out_ref, cn_out_ref,
                   hn_ref, hc_ref, cc_ref, raw_ref, *, n_seg, seg_len, tiles_per_seq, row_chunk):
    i = pl.program_id(0)
    j = pl.program_id(1)
    carried = tiles_per_seq > 1
    heads_per_step, hw = wa_ref.shape[0], wa_ref.shape[1]
    n_lru_steps = LRU_HEADS // heads_per_step

    @pl.when(j == 0)
    def _():
        hn_ref[...] = _rms(x_ref[...], gn_ref[...]).astype(BF16)

    @pl.when(j < n_lru_steps)
    def _():
        if carried:
            @pl.when(i % tiles_per_seq == 0)
            def _():
                cc_ref[j] = c0_ref[0]
                hc_ref[j] = h0_ref[0]

        cw = cw_ref[...]
        cb = cb_ref[...]
        wg, wx, wa, wi = (r[...].astype(BF16) for r in (wg_ref, wx_ref, wa_ref, wi_ref))
        n_prev = cw.shape[0] - 1
        n_chunks = hn_ref.shape[0] // row_chunk
        segs_per_chunk = row_chunk // seg_len if n_seg > 1 else 0
        whole = n_seg == 1
        state = {"conv": (cc_ref[j] if carried else c0_ref[0]) if whole else None,
                 "h": (hc_ref[j] if carried else h0_ref[0]) if whole else None}
        log_lam = jax.nn.log_sigmoid(lam_ref[...])

        def rows(c):
            return slice(c * row_chunk, (c + 1) * row_chunk)

        def in_proj(c):
            hn = hn_ref[rows(c), :]
            raw_ref[0, rows(c), :] = _dot(hn, wg)
            raw_ref[1, rows(c), :] = _dot(hn, wx)

        def conv_gates(c):
            xr = raw_ref[1, rows(c), :]
            if whole:
                xc = _causal_conv(xr, state["conv"], cw, cb)
                state["conv"] = xr[row_chunk - n_prev:row_chunk]
            else:
                parts = []
                for s in range(segs_per_chunk):
                    sidx = c * segs_per_chunk + s
                    seg = xr[s * seg_len:(s + 1) * seg_len]
                    parts.append(_causal_conv(seg, c0_ref[sidx], cw, cb))
                    cn_out_ref[sidx] = seg[seg_len - n_prev:seg_len]
                xc = _cat_rows(parts)
            xcb = xc.astype(BF16)
            raw_ref[1, rows(c), :] = xc
            for hh in range(heads_per_step):
                cols = slice(hh * hw, (hh + 1) * hw)
                raw_ref[2, rows(c), cols] = _dot(xcb[:, cols], wa[hh])
                raw_ref[3, rows(c), cols] = _dot(xcb[:, cols], wi[hh])

        def recur(c):
            xc = raw_ref[1, rows(c), :]
            r = jax.nn.sigmoid(raw_ref[2, rows(c), :] + ba_ref[...])
            ig = jax.nn.sigmoid(raw_ref[3, rows(c), :] + bi_ref[...])
            log_a = (LRU_C * r) * log_lam
            a = jnp.exp(log_a)
            one_minus_a2 = -jnp.tanh(log_a) * (a * a + 1.0)
            gated = jnp.sqrt(one_minus_a2) * (ig * xc)
            if whole:
                cum_a, cum_b = _scan_rows(a, gated)
                hs = cum_b + cum_a * state["h"]
                state["h"] = hs[row_chunk - 1:row_chunk]
            else:
                parts = []
                for s in range(segs_per_chunk):
                    sidx = c * segs_per_chunk + s
                    sl = slice(s * seg_len, (s + 1) * seg_len)
                    cum_a, cum_b = _scan_rows(a[sl], gated[sl])
                    seg_hs = cum_b + cum_a * h0_ref[sidx]
                    hn_out_ref[sidx] = seg_hs[seg_len - 1:seg_len]
                    parts.append(seg_hs)
                hs = _cat_rows(parts)
            mix_ref[rows(c), :] = (_gelu(raw_ref[0, rows(c), :]) * hs).astype(BF16)

        for t in range(n_chunks + 2):
            if t < n_chunks:
                in_proj(t)
            if 0 <= t - 1 < n_chunks:
                conv_gates(t - 1)
            if 0 <= t - 2 < n_chunks:
                recur(t - 2)
        if whole:
            cn_out_ref[0] = state["conv"]
            hn_out_ref[0] = state["h"]
            if carried:
                cc_ref[j] = state["conv"]
                hc_ref[j] = state["h"]

    @pl.when(j >= n_lru_steps)
    def _():
        chunk = max(row_chunk, min(hn_ref.shape[0], MEM_ATTN_ROWS))
        n_chunks = hn_ref.shape[0] // chunk
        segs = (chunk // seg_len, seg_len) if n_seg > 1 else (1, chunk)

        wq = wx_ref[...].astype(BF16)

        def rows(c):
            return slice(c * chunk, (c + 1) * chunk)

        def project(c):
            raw_ref[0, rows(c), :] = _dot(hn_ref[rows(c), :], wq)

        def attend(c):
            seg0 = c * segs[0] if n_seg > 1 else 0
            mo_ref[rows(c), :] = _mem_attn(raw_ref[0, rows(c), :], gq_ref[...], mk_ref, mv_ref,
                                           seg0, *segs).astype(BF16)

        _two_stage(n_chunks, project, attend)


def _lru_in(x2d, tl, li, mi, p, h0, c0, mk, mv, bc, row_chunk):
    m, d = x2d.shape
    hw = d // LRU_HEADS
    mem_w = mk.shape[-1]
    hp = bc // hw
    n_lru, n_mem = d // bc, mem_w // bc
    ns, tps = tl.n_seg, tl.tiles_per_seq
    width = p["lru_conv_w"].shape[1]

    def hcol(j):
        return jnp.minimum(j, n_lru - 1)

    def mcol(j):
        return jnp.maximum(j - n_lru, 0)

    vec = pl.BlockSpec((None, 1, bc), lambda i, j: (mi, 0, hcol(j)))
    gate_w = pl.BlockSpec((None, hp, hw, hw), lambda i, j: (mi, hcol(j), 0, 0))
    mem = _resident((None, ns, mk.shape[2], bc), lambda i, j: (li, i // tps, 0, mcol(j)))
    assert bc % hw == 0 and d % bc == 0 and mem_w % bc == 0
    assert tl.bm % row_chunk == 0 and (row_chunk % tl.seg_len == 0 if ns > 1 else True)
    kern = functools.partial(_lru_in_kernel, n_seg=ns, seg_len=tl.seg_len, tiles_per_seq=tps,
                             row_chunk=row_chunk)
    return pl.pallas_call(
        kern,
        grid=(tl.n_tiles, n_lru + n_mem),
        in_specs=[
            _resident((tl.bm, d), lambda i, j: (i, 0)),
            pl.BlockSpec((None, 1, d), lambda i, j: (li, 0, 0)),
            pl.BlockSpec((None, d, bc), lambda i, j: (mi, 0, hcol(j))),
            pl.BlockSpec((None, d, bc), lambda i, j: (mi, 0, n_lru + j)),
            pl.BlockSpec((None, width, bc), lambda i, j: (mi, 0, hcol(j))),
            vec, gate_w, gate_w, vec, vec, vec,
            pl.BlockSpec((None, ns, 1, bc), lambda i, j: (mi, i // tps, 0, hcol(j))),
            pl.BlockSpec((None, ns, width - 1, bc), lambda i, j: (mi, i // tps, 0, hcol(j))),
            pl.BlockSpec((None, 1, hw), lambda i, j: (li, 0, 0)),
            mem, mem,
        ],
        out_specs=[
            pl.BlockSpec((tl.bm, bc), lambda i, j: (i, hcol(j))),
            pl.BlockSpec((tl.bm, bc), lambda i, j: (i, mcol(j))),
            pl.BlockSpec((None, ns, 1, bc), lambda i, j: (i, 0, 0, hcol(j))),
            pl.BlockSpec((None, ns, width - 1, bc), lambda i, j: (i, 0, 0, hcol(j))),
        ],
        out_shape=[
            jax.ShapeDtypeStruct((m, d), BF16),
            jax.ShapeDtypeStruct((m, mem_w), BF16),
            jax.ShapeDtypeStruct((tl.n_tiles, ns, 1, d), F32),
            jax.ShapeDtypeStruct((tl.n_tiles, ns, width - 1, d), F32),
        ],
        scratch_shapes=[
            pltpu.VMEM((tl.bm, d), BF16),
            pltpu.VMEM((n_lru, 1, bc), F32),
            pltpu.VMEM((n_lru, width - 1, bc), F32),
            pltpu.VMEM((4, tl.bm, bc), F32),
        ],
        compiler_params=_cparams(2),
        name="lru_in",
    )(x2d, p["norm_mix"], p["lru_w_in"], p["lru_w_in"], p["lru_conv_w"], p["lru_conv_b"],
      p["lru_gate_a_w"], p["lru_gate_x_w"], p["lru_gate_a_b"], p["lru_gate_x_b"], p["lru_lambda"],
      h0, c0, p["mem_q_norm"], mk, mv)


def _att_in_kernel(*refs, n_seg, seg_len, nq, row_chunk, aliased):
    if aliased:
        refs = refs[:8] + refs[10:]
    (x_ref, gn_ref, w_ref, qg_ref, kg_ref, gq_ref, mk_ref, mv_ref,
     qb_ref, kf_ref, kb_ref, vf_ref, vb_ref, mo_ref, hn_ref, z_ref) = refs
    j = pl.program_id(1)

    @pl.when(j == 0)
    def _():
        hn_ref[...] = _rms(x_ref[...], gn_ref[...]).astype(BF16)

    dk = qg_ref.shape[-1]
    n_groups = z_ref.shape[-1] // dk
    rows_per_t = hn_ref.shape[-1] // dk
    n_heads = rows_per_t // 2
    n_chunks = hn_ref.shape[0] // row_chunk
    w = w_ref[...].astype(BF16)

    def rows(c):
        return slice(c * row_chunk, (c + 1) * row_chunk)

    def project(c):
        z_ref[rows(c), :] = _dot(hn_ref[rows(c), :], w)

    def queries(c):
        qscale = dk ** -0.5 * LOG2E
        for g in range(n_groups):
            sl = slice(g * dk, (g + 1) * dk)
            qb_ref[rows(c), sl] = (_rms(z_ref[rows(c), sl], qg_ref[...]) * qscale).astype(BF16)

    def keys(c):
        kns = []
        for g in range(n_groups):
            sl = slice(g * dk, (g + 1) * dk)
            kn = _rms(z_ref[rows(c), sl], kg_ref[...])
            kb_ref[rows(c), sl] = kn.astype(BF16)
            kns.append(kn)
        first = pl.multiple_of((j - nq) * n_groups, n_groups)
        kf_ref[rows(c), pl.ds(first, n_groups), :] = (
            jnp.concatenate(kns, axis=-1).reshape(row_chunk, n_groups, dk))

    def values(c):
        z = z_ref[rows(c), :]
        vb_ref[rows(c), :] = z.astype(BF16)
        heads = n_groups // 2
        head0 = pl.multiple_of((j - 2 * nq) * heads, heads)
        for half in range(2):
            part = jnp.concatenate([z[:, (2 * h + half) * dk:(2 * h + half + 1) * dk] for h in range(heads)],
                                   axis=-1)
            vf_ref[rows(c), pl.ds(half * n_heads + head0, heads), :] = part.reshape(row_chunk, heads, dk)

    mem_chunk = max(row_chunk, min(hn_ref.shape[0], MEM_ATTN_ROWS))
    mem_segs = (mem_chunk // seg_len, seg_len) if n_seg > 1 else (1, mem_chunk)

    def mem_rows(c):
        return slice(c * mem_chunk, (c + 1) * mem_chunk)

    def mem_project(c):
        z_ref[mem_rows(c), :] = _dot(hn_ref[mem_rows(c), :], w)

    def memory(c):
        seg0 = c * mem_segs[0] if n_seg > 1 else 0
        mo_ref[mem_rows(c), :] = _mem_attn(z_ref[mem_rows(c), :], gq_ref[...], mk_ref, mv_ref,
                                           seg0, *mem_segs).astype(BF16)

    @pl.when(j < nq)
    def _():
        _two_stage(n_chunks, project, queries)

    @pl.when((j >= nq) & (j < 2 * nq))
    def _():
        _two_stage(n_chunks, project, keys)

    @pl.when((j >= 2 * nq) & (j < 3 * nq))
    def _():
        _two_stage(n_chunks, project, values)

    @pl.when(j >= 3 * nq)
    def _():
        _two_stage(hn_ref.shape[0] // mem_chunk, mem_project, memory)


def _att_in(x2d, tl, li, ai, p, mk, mv, bc, row_chunk, kv_all):
    m, d = x2d.shape
    mem_w = mk.shape[-1]
    nq, nm = d // bc, mem_w // bc
    n_steps = 3 * nq + nm
    ns, tps = tl.n_seg, tl.tiles_per_seq
    dk = p["attn_q_norm"].shape[-1]
    hd = p["mem_q_norm"].shape[-1]

    def col(lo, n):
        return lambda i, j: (i, jnp.clip(j - lo, 0, n - 1))

    mem = _resident((None, ns, mk.shape[2], bc),
                    lambda i, j: (li, i // tps, 0, jnp.clip(j - 3 * nq, 0, nm - 1)))
    assert tl.bm % row_chunk == 0 and (row_chunk % tl.seg_len == 0 if ns > 1 else True)
    n_attn = p["attn_w_in"].shape[0]
    rows_per_t = d // dk
    aliased = kv_all is not None
    kern = functools.partial(_att_in_kernel, n_seg=ns, seg_len=tl.seg_len, nq=nq, row_chunk=row_chunk,
                             aliased=aliased)
    kv_spec = _resident((None, tl.bm, rows_per_t, dk), lambda i, j: (ai, i, 0, 0))
    kv_shape = jax.ShapeDtypeStruct((n_attn, m, rows_per_t, dk), F32)
    extra_in = [pl.BlockSpec(memory_space=pl.ANY)] * 2 if aliased else []
    return pl.pallas_call(
        kern,
        grid=(tl.n_tiles, n_steps),
        in_specs=[
            _resident((tl.bm, d), lambda i, j: (i, 0)),
            pl.BlockSpec((None, 1, d), lambda i, j: (li, 0, 0)),
            pl.BlockSpec((None, d, bc), lambda i, j: (ai, 0, j)),
            pl.BlockSpec((None, 1, dk), lambda i, j: (ai, 0, 0)),
            pl.BlockSpec((None, 1, dk), lambda i, j: (ai, 0, 0)),
            pl.BlockSpec((None, 1, hd), lambda i, j: (li, 0, 0)),
            mem, mem, *extra_in,
        ],
        out_specs=[
            pl.BlockSpec((tl.bm, bc), col(0, nq)),
            kv_spec,
            pl.BlockSpec((tl.bm, bc), col(nq, nq)),
            kv_spec,
            pl.BlockSpec((tl.bm, bc), col(2 * nq, nq)),
            pl.BlockSpec((tl.bm, bc), col(3 * nq, nm)),
        ],
        out_shape=[
            jax.ShapeDtypeStruct((m, d), BF16),
            kv_shape,
            jax.ShapeDtypeStruct((m, d), BF16),
            kv_shape,
            jax.ShapeDtypeStruct((m, d), BF16),
            jax.ShapeDtypeStruct((m, mem_w), BF16),
        ],
        input_output_aliases={8: 1, 9: 3} if aliased else {},
        scratch_shapes=[pltpu.VMEM((tl.bm, d), BF16), pltpu.VMEM((tl.bm, bc), F32)],
        compiler_params=_cparams(2),
        name="att_in",
    )(x2d, p["norm_mix"], p["attn_w_in"], p["attn_q_norm"], p["attn_k_norm"], p["mem_q_norm"], mk, mv,
      *(kv_all if aliased else ()))


def _attn_block(q, k, v, m_ref, l_ref, acc_ref, mask, lanes=slice(None)):
    dk = q.shape[-1] // 2
    tk = k.shape[0]
    scores = [_dot_nt(k[:, c * dk:(c + 1) * dk], q[:, c * dk:(c + 1) * dk]) for c in range(2)]
    for c in range(2):
        s = scores[c]
        if mask is not None:
            s = jnp.where(mask, s, NEG_BIG)
        m_old = m_ref[c, :, lanes]
        m_new = jnp.maximum(m_old, jnp.max(s, axis=0, keepdims=True))
        alpha = jnp.exp2(m_old - m_new)
        p = jnp.exp2(s - m_new)
        l_ref[c, :, lanes] = (alpha * l_ref[c, :, lanes]
                              + jnp.sum(p.reshape(tk // SUBLANES, SUBLANES, p.shape[-1]), axis=0))
        pv = lax.dot_general(v, p.astype(BF16), (((0,), (0,)), ((), ())), preferred_element_type=F32)
        acc_ref[c, :, lanes] = alpha * acc_ref[c, :, lanes] + pv
        m_ref[c, :, lanes] = m_new


def _attn_init(m_ref, l_ref, acc_ref):
    m_ref[...] = jnp.full(m_ref.shape, NEG_BIG, F32)
    l_ref[...] = jnp.zeros(l_ref.shape, F32)
    acc_ref[...] = jnp.zeros(acc_ref.shape, F32)


def _attn_finish(lp_ref, sg_ref, l_ref, acc_ref, lam_init):
    lp = lp_ref[...]
    lam = (jnp.exp(jnp.sum(lp[0:1] * lp[1:2], axis=-1, keepdims=True))
           - jnp.exp(jnp.sum(lp[2:3] * lp[3:4], axis=-1, keepdims=True)) + lam_init)
    l0 = jnp.sum(l_ref[0], axis=0, keepdims=True)
    l1 = jnp.sum(l_ref[1], axis=0, keepdims=True)
    o = (acc_ref[0] / l0 - lam * (acc_ref[1] / l1)).T
    return (_rms(o, sg_ref[...]) * (1.0 - lam_init)).astype(BF16)


def _attn_prompt_kernel(lp_ref, sg_ref, q_ref, k_ref, v_ref, o_ref, m_ref, l_ref, acc_ref, *, tq, tk, lam_init):
    qi = pl.program_id(2)
    _attn_init(m_ref, l_ref, acc_ref)
    q = q_ref[...]
    per = tk // tq

    def visible(start, size):
        start = pl.multiple_of(start, size)
        _attn_block(q, k_ref[pl.ds(start, size), :], v_ref[pl.ds(start, size), :], m_ref, l_ref, acc_ref, None)

    def body(kv, carry):
        visible(kv * tk, tk)
        return carry

    n_big = qi // per
    lax.fori_loop(0, n_big, body, 0)
    for r in range(1, per):
        @pl.when(qi % per >= r)
        def _():
            visible(n_big * tk + (r - 1) * tq, tq)

    half = tq // 2
    shift = CHUNK.bit_length() - 1

    def chunk_mask(n_keys, n_queries):
        keyc = lax.shift_right_logical(lax.broadcasted_iota(jnp.int32, (n_keys, n_queries), 0), shift)
        qryc = lax.shift_right_logical(lax.broadcasted_iota(jnp.int32, (n_keys, n_queries), 1), shift)
        return keyc <= qryc

    start = pl.multiple_of(qi * tq, tq)
    _attn_block(q, k_ref[pl.ds(start, half), :], v_ref[pl.ds(start, half), :], m_ref, l_ref, acc_ref,
                chunk_mask(half, tq))
    start = pl.multiple_of(qi * tq + half, half)
    _attn_block(q[half:], k_ref[pl.ds(start, half), :], v_ref[pl.ds(start, half), :], m_ref, l_ref, acc_ref,
                chunk_mask(half, half), lanes=slice(half, tq))
    o_ref[...] = _attn_finish(lp_ref, sg_ref, l_ref, acc_ref, lam_init)


def _attn_prompt(qb, kb, vb, ai, p, n_seq, t, tq, tk, lam_init):
    d = qb.shape[-1]
    hw = d // DIFF_HEADS
    assert t % tq == 0 and tq % (2 * CHUNK) == 0 and tk % tq == 0
    q3, k3, v3 = (a.reshape(n_seq, t, d) for a in (qb, kb, vb))
    lp, sg = p["attn_lambda"], p["attn_subln"]
    kern = functools.partial(_attn_prompt_kernel, tq=tq, tk=tk, lam_init=lam_init)
    out = pl.pallas_call(
        kern,
        grid=(n_seq, DIFF_HEADS, t // tq),
        in_specs=[
            pl.BlockSpec((None,) + lp.shape[1:], lambda b, h, qi: (ai, 0, 0)),
            pl.BlockSpec((None, 1, hw), lambda b, h, qi: (ai, 0, 0)),
            pl.BlockSpec((None, tq, hw), lambda b, h, qi: (b, qi, h)),
            pl.BlockSpec((None, t, hw), lambda b, h, qi: (b, 0, h)),
            pl.BlockSpec((None, t, hw), lambda b, h, qi: (b, 0, h)),
        ],
        out_specs=pl.BlockSpec((None, tq, hw), lambda b, h, qi: (b, qi, h)),
        out_shape=jax.ShapeDtypeStruct((n_seq, t, d), BF16),
        scratch_shapes=[
            pltpu.VMEM((2, 1, tq), F32),
            pltpu.VMEM((2, SUBLANES, tq), F32),
            pltpu.VMEM((2, hw, tq), F32),
        ],
        compiler_params=_cparams(3),
        name="attn_prompt",
    )(lp, sg, q3, k3, v3)
    return out.reshape(n_seq * t, d)


def _attn_cached_kernel(lp_ref, sg_ref, q_ref, kp_ref, vp_ref, kn_ref, vn_ref, o_ref,
                        m_ref, l_ref, acc_ref, kbuf_ref, vbuf_ref, *, lam_init):
    n_heads = DIFF_HEADS
    past, rows_per_t, dk = kp_ref.shape
    hw = 2 * dk
    group = SUBLANES
    for g in range(rows_per_t // group):
        fold = slice(g * group * dk, (g + 1) * group * dk)
        kbuf_ref[:, fold] = kp_ref[:, g * group:(g + 1) * group, :].reshape(past, group * dk).astype(BF16)
        vbuf_ref[:, fold] = vp_ref[:, g * group:(g + 1) * group, :].reshape(past, group * dk).astype(BF16)
    for h in range(n_heads):
        cols = slice(h * hw, (h + 1) * hw)
        v = jnp.concatenate([vbuf_ref[:, (c * n_heads + h) * dk:(c * n_heads + h + 1) * dk] for c in range(2)],
                            axis=-1)
        q = q_ref[:, cols]
        _attn_init(m_ref, l_ref, acc_ref)
        _attn_block(q, kbuf_ref[:, cols], v, m_ref, l_ref, acc_ref, None)
        _attn_block(q, kn_ref[:, cols], vn_ref[:, cols], m_ref, l_ref, acc_ref, None)
        o_ref[:, cols] = _attn_finish(lp_ref, sg_ref, l_ref, acc_ref, lam_init)


def _attn_cached(qb, kb, vb, k_past, v_past, ai, p, n_seq, t, lam_init):
    d = qb.shape[-1]
    hw = d // DIFF_HEADS
    n_attn, _, past, n_heads, _, dk = k_past.shape
    assert past % CHUNK == 0 and t <= CHUNK and n_heads == DIFF_HEADS and 2 * dk == hw
    q3, k3, v3 = (a.reshape(n_seq, t, d) for a in (qb, kb, vb))
    lp, sg = p["attn_lambda"], p["attn_subln"]
    k_rows = k_past.reshape(n_attn * n_seq, past, 2 * n_heads, dk)
    v_rows = v_past.reshape(n_attn, n_seq, past, n_heads, 2, dk).transpose(0, 1, 2, 4, 3, 5)
    v_rows = v_rows.reshape(n_attn * n_seq, past, 2 * n_heads, dk)
    kern = functools.partial(_attn_cached_kernel, lam_init=lam_init)
    new_spec = pl.BlockSpec((None, t, d), lambda b: (b, 0, 0))
    past_spec = pl.BlockSpec((None, past, 2 * n_heads, dk), lambda b: (ai * n_seq + b, 0, 0, 0))
    out = pl.pallas_call(
        kern,
        grid=(n_seq,),
        in_specs=[
            pl.BlockSpec((None,) + lp.shape[1:], lambda b: (ai, 0, 0)),
            pl.BlockSpec((None, 1, hw), lambda b: (ai, 0, 0)),
            new_spec, past_spec, past_spec, new_spec, new_spec,
        ],
        out_specs=new_spec,
        out_shape=jax.ShapeDtypeStruct((n_seq, t, d), BF16),
        scratch_shapes=[
            pltpu.VMEM((2, 1, t), F32),
            pltpu.VMEM((2, SUBLANES, t), F32),
            pltpu.VMEM((2, hw, t), F32),
            pltpu.VMEM((past, d), BF16),
            pltpu.VMEM((past, d), BF16),
        ],
        compiler_params=_cparams(1),
        name="attn_cached",
    )(lp, sg, q3, k_rows, v_rows, k3, v3)
    return out.reshape(n_seq * t, d)


def _out_kernel(x_ref, mix_ref, mo_ref, w1_ref, w2_ref, o_ref):
    o_ref[...] = x_ref[...] + (_dot(mix_ref[...], w1_ref[...]) + _dot(mo_ref[...], w2_ref[...]))


def _out_proj(x2d, mix, mo, w_out, wi, bm, bn):
    m, d = x2d.shape
    k1, k2 = mix.shape[1], mo.shape[1]
    assert k1 % k2 == 0
    return pl.pallas_call(
        _out_kernel,
        grid=(m // bm, d // bn),
        in_specs=[
            pl.BlockSpec((bm, bn), lambda i, j: (i, j)),
            pl.BlockSpec((bm, k1), lambda i, j: (i, 0)),
            pl.BlockSpec((bm, k2), lambda i, j: (i, 0)),
            pl.BlockSpec((None, k1, bn), lambda i, j: (wi, 0, j)),
            pl.BlockSpec((None, k2, bn), lambda i, j: (wi, k1 // k2, j)),
        ],
        out_specs=pl.BlockSpec((bm, bn), lambda i, j: (i, j)),
        out_shape=jax.ShapeDtypeStruct((m, d), F32),
        compiler_params=_cparams(2),
        name="out_proj",
    )(x2d, mix, mo, w_out, w_out)


def _ffn_kernel(x_ref, gn_ref, wg_ref, wv_ref, cwg_ref, cwv_ref, cbg_ref, cbv_ref, stg_ref, stv_ref, wdn_ref,
                o_ref, nstg_ref, nstv_ref, hn_ref, cg_ref, cv_ref, raw_ref, act_ref,
                *, n_seg, seg_len, tiles_per_seq, row_chunk):
    i = pl.program_id(0)
    j = pl.program_id(1)
    carried = tiles_per_seq > 1

    @pl.when(j == 0)
    def _():
        x = x_ref[...]
        hn_ref[...] = _rms(x, gn_ref[...]).astype(BF16)
        o_ref[...] = x

    if carried:
        @pl.when(i % tiles_per_seq == 0)
        def _():
            cg_ref[j] = stg_ref[0]
            cv_ref[j] = stv_ref[0]

    halves = ((wg_ref, cwg_ref, cbg_ref, stg_ref, cg_ref, nstg_ref),
              (wv_ref, cwv_ref, cbv_ref, stv_ref, cv_ref, nstv_ref))
    n_prev = cwg_ref.shape[0] - 1
    n_chunks = hn_ref.shape[0] // row_chunk
    segs_per_chunk = row_chunk // seg_len if n_seg > 1 else 0
    prevs = [(h[4][j] if carried else h[3][0]) if n_seg == 1 else None for h in halves]

    def rows(c):
        return slice(c * row_chunk, (c + 1) * row_chunk)

    def up_proj(c):
        hn = hn_ref[rows(c), :]
        for hi, h in enumerate(halves):
            raw_ref[hi, rows(c), :] = _dot(hn, h[0][...])

    def conv_gate(c):
        ys = []
        for hi, (_, cw_ref, cb_ref, st_ref, _, nst_ref) in enumerate(halves):
            up = raw_ref[hi, rows(c), :]
            cw = cw_ref[...]
            cb = cb_ref[...]
            if n_seg == 1:
                ys.append(_causal_conv(up, prevs[hi], cw, cb))
                prevs[hi] = up[row_chunk - n_prev:row_chunk]
            else:
                parts = []
                for s in range(segs_per_chunk):
                    seg = up[s * seg_len:(s + 1) * seg_len]
                    parts.append(_causal_conv(seg, st_ref[c * segs_per_chunk + s], cw, cb))
                    nst_ref[c * segs_per_chunk + s] = seg[seg_len - n_prev:seg_len]
                ys.append(_cat_rows(parts))
        act_ref[rows(c), :] = (_gelu(ys[0]) * ys[1]).astype(BF16)

    def down_proj(c):
        o_ref[rows(c), :] += _dot(act_ref[rows(c), :], wdn_ref[...])

    for t in range(n_chunks + 2):
        if t < n_chunks:
            up_proj(t)
        if 0 <= t - 1 < n_chunks:
            conv_gate(t - 1)
        if 0 <= t - 2 < n_chunks:
            down_proj(t - 2)
    if n_seg == 1:
        for hi, (_, _, _, _, carry_ref, nst_ref) in enumerate(halves):
            nst_ref[0] = prevs[hi]
            if carried:
                carry_ref[j] = prevs[hi]


def _ffn(x2d, tl, li, p, state, bf, row_chunk):
    m, d = x2d.shape
    dff = p["ffn_w_down"].shape[1]
    nj = dff // bf
    ns, tps = tl.n_seg, tl.tiles_per_seq
    width = p["ffn_conv_w"].shape[1]
    assert tl.bm % row_chunk == 0 and (row_chunk % tl.seg_len == 0 if ns > 1 else True)
    kern = functools.partial(_ffn_kernel, n_seg=ns, seg_len=tl.seg_len, tiles_per_seq=tps, row_chunk=row_chunk)

    def halves(block, index_map):
        gate = pl.BlockSpec(block, lambda i, j: index_map(i, j, j))
        value = pl.BlockSpec(block, lambda i, j: index_map(i, j, nj + j))
        return [gate, value]

    nst_spec = pl.BlockSpec((None, ns, width - 1, bf), lambda i, j: (i, 0, 0, j))
    nst_shape = jax.ShapeDtypeStruct((tl.n_tiles, ns, width - 1, dff), F32)
    return pl.pallas_call(
        kern,
        grid=(tl.n_tiles, nj),
        in_specs=[
            _resident((tl.bm, d), lambda i, j: (i, 0)),
            pl.BlockSpec((None, 1, d), lambda i, j: (li, 0, 0)),
            *halves((None, d, bf), lambda i, j, c: (li, 0, c)),
            *halves((None, width, bf), lambda i, j, c: (li, 0, c)),
            *halves((None, 1, bf), lambda i, j, c: (li, 0, c)),
            *halves((None, ns, width - 1, bf), lambda i, j, c: (li, i // tps, 0, c)),
            pl.BlockSpec((None, bf, d), lambda i, j: (li, j, 0)),
        ],
        out_specs=[pl.BlockSpec((tl.bm, d), lambda i, j: (i, 0)), nst_spec, nst_spec],
        out_shape=[jax.ShapeDtypeStruct((m, d), F32), nst_shape, nst_shape],
        scratch_shapes=[
            pltpu.VMEM((tl.bm, d), BF16),
            pltpu.VMEM((nj, width - 1, bf), F32),
            pltpu.VMEM((nj, width - 1, bf), F32),
            pltpu.VMEM((2, tl.bm, bf), F32),
            pltpu.VMEM((tl.bm, bf), BF16),
        ],
        compiler_params=_cparams(2),
        name="ffn",
    )(x2d, p["norm_ffn"], p["ffn_w_up"], p["ffn_w_up"], p["ffn_conv_w"], p["ffn_conv_w"],
      p["ffn_conv_b"], p["ffn_conv_b"], state, state, p["ffn_w_down"])


def _run_trunk(x, mem_k, mem_v, lru_h, lru_conv, attn_past, ffn_conv, p, cfg):
    n_seq, t, d = x.shape
    depth = p["norm_mix"].shape[0]
    x2d = x.reshape(n_seq * t, d)
    tl = _Tiling(n_seq, t, cfg["bm"])
    tl_ffn = _Tiling(n_seq, t, cfg["bm_ffn"])
    mem_tokens = mem_k.shape[2]
    mk = mem_k.reshape(depth, n_seq, mem_tokens, -1).astype(BF16)
    mv = mem_v.reshape(depth, n_seq, mem_tokens, -1).astype(BF16)
    h0 = lru_h[:, :, None, :]
    new_h, new_lconv, new_fconv = [], [], []
    kv_all = None
    for i in range(depth):
        j = i // 2
        if i % 2 == 0:
            mix, mo, h_last, buf = _lru_in(x2d, tl, i, j, p, h0, lru_conv, mk, mv, cfg["lru_bc"], cfg["lru_rows"])
            new_h.append(_last_tile_state(h_last, tl)[:, 0, :])
            new_lconv.append(_last_tile_state(buf, tl))
            w_out = p["lru_w_out"]
        else:
            qb, k_all, kb, v_all, vb, mo = _att_in(x2d, tl, i, j, p, mk, mv, cfg["bc"], cfg["att_rows"], kv_all)
            kv_all = (k_all, v_all)
            lam_init = 0.8 - 0.6 * math.exp(-0.3 * i)
            if attn_past is None:
                mix = _attn_prompt(qb, kb, vb, j, p, n_seq, t, cfg["tq"], cfg["tk"], lam_init)
            else:
                mix = _attn_cached(qb, kb, vb, attn_past[0], attn_past[1], j, p, n_seq, t, lam_init)
            w_out = p["attn_w_out"]
        x2d = _out_proj(x2d, mix, mo, w_out, j, cfg["bm_out"], cfg["bn_out"])
        x2d, fg, fv = _ffn(x2d, tl_ffn, i, p, ffn_conv, cfg["bf"], cfg["ffn_rows"])
        new_fconv.append(jnp.concatenate([_last_tile_state(fg, tl_ffn), _last_tile_state(fv, tl_ffn)], axis=-1))
    return x2d.reshape(n_seq, t, d), new_h, new_lconv, kv_all[0], kv_all[1], new_fconv


def _forward(x_prompt, x_sample, mem_prompt, cache_attn_k, cache_attn_v, cache_mem_k, cache_mem_v,
             state_lru_h, state_lru_conv, state_ffn_conv, p, mem_norm, mem_w_kv, mem_k_norm, cfg_p, cfg_s):
    b, t, d = x_prompt.shape
    depth = p["norm_mix"].shape[0]
    n_lru, n_attn = (depth + 1) // 2, depth // 2

    p = dict(p)
    for name in ("ffn_w_up", "ffn_w_down", "lru_w_out", "attn_w_out"):
        p[name] = p[name].astype(BF16)
    for name in ("norm_mix", "norm_ffn", "lru_conv_b", "lru_gate_a_b", "lru_gate_x_b", "lru_lambda",
                 "attn_q_norm", "attn_k_norm", "attn_subln", "mem_q_norm", "ffn_conv_b"):
        p[name] = p[name][:, None, :]

    mem_tokens = mem_prompt.shape[1]
    mk, mv = _mem_kv(mem_prompt.reshape(b * mem_tokens, d), mem_norm[:, None, :], mem_w_kv,
                     mem_k_norm[:, None, :])
    hd = mem_k_norm.shape[-1]
    p_mem_k = mk.reshape(depth, b, mem_tokens, MEM_HEADS, hd)
    p_mem_v = mv.reshape(depth, b, mem_tokens, MEM_HEADS, hd)

    zeros_h = jnp.zeros((n_lru, b, d), F32)
    zeros_lconv = jnp.zeros((n_lru, b) + state_lru_conv.shape[2:], F32)
    zeros_fconv = jnp.zeros((depth, b) + state_ffn_conv.shape[2:], F32)
    yp, ph, plc, pk, pv, pfc = _run_trunk(x_prompt, p_mem_k, p_mem_v, zeros_h, zeros_lconv, None,
                                          zeros_fconv, p, cfg_p)
    ys, sh, slc, sk, sv, sfc = _run_trunk(x_sample, cache_mem_k, cache_mem_v, state_lru_h, state_lru_conv,
                                          (cache_attn_k, cache_attn_v), state_ffn_conv, p, cfg_s)

    dk = p["attn_q_norm"].shape[-1]
    db, dt = x_sample.shape[0], x_sample.shape[1]

    def kshape(a, n, tt):
        return a.reshape(n_attn, n, tt, DIFF_HEADS, 2, dk)

    def vshape(a, n, tt):
        a = a.reshape(n_attn, n, tt, 2, DIFF_HEADS, dk)
        return a.transpose(0, 1, 2, 4, 3, 5).reshape(n_attn, n, tt, DIFF_HEADS, 2 * dk)

    return (yp, ys, jnp.stack(ph), jnp.stack(plc), kshape(pk, b, t), vshape(pv, b, t), p_mem_k, p_mem_v,
            jnp.stack(pfc), jnp.stack(sh), jnp.stack(slc), kshape(sk, db, dt), vshape(sv, db, dt),
            jnp.stack(sfc))


CFG_PROMPT = dict(bm=1024, bc=512, lru_bc=512, lru_rows=128, att_rows=256, bm_ffn=1024, bf=512, ffn_rows=512,
                  bm_out=1024, bn_out=1024, tq=1024, tk=1024)
CFG_SAMPLE = dict(bm=1024, bc=512, lru_bc=512, lru_rows=128, att_rows=256, bm_ffn=1024, bf=512, ffn_rows=512,
                  bm_out=1024, bn_out=1024)


def kernel(x_prompt, x_sample, mem_prompt, cache_attn_k, cache_attn_v, cache_mem_k, cache_mem_v, state_lru_h, state_lru_conv, state_ffn_conv, norm_mix, norm_ffn, lru_w_in, lru_conv_w, lru_conv_b, lru_gate_a_w, lru_gate_a_b, lru_gate_x_w, lru_gate_x_b, lru_lambda, lru_w_out, attn_w_in, attn_q_norm, attn_k_norm, attn_lambda, attn_subln, attn_w_out, mem_norm, mem_w_kv, mem_q_norm, mem_k_norm, ffn_w_up, ffn_conv_w, ffn_conv_b, ffn_w_down):
    p = {
        "norm_mix": norm_mix, "norm_ffn": norm_ffn,
        "lru_w_in": lru_w_in, "lru_conv_w": lru_conv_w, "lru_conv_b": lru_conv_b,
        "lru_gate_a_w": lru_gate_a_w, "lru_gate_a_b": lru_gate_a_b,
        "lru_gate_x_w": lru_gate_x_w, "lru_gate_x_b": lru_gate_x_b,
        "lru_lambda": lru_lambda, "lru_w_out": lru_w_out,
        "attn_w_in": attn_w_in, "attn_q_norm": attn_q_norm, "attn_k_norm": attn_k_norm,
        "attn_lambda": attn_lambda, "attn_subln": attn_subln, "attn_w_out": attn_w_out,
        "mem_q_norm": mem_q_norm,
        "ffn_w_up": ffn_w_up, "ffn_conv_w": ffn_conv_w, "ffn_conv_b": ffn_conv_b, "ffn_w_down": ffn_w_down,
    }
    return _forward(x_prompt, x_sample, mem_prompt, cache_attn_k, cache_attn_v, cache_mem_k, cache_mem_v,
                    state_lru_h, state_lru_conv, state_ffn_conv, p, mem_norm, mem_w_kv, mem_k_norm,
                    CFG_PROMPT, CFG_SAMPLE)
```

```python
import functools
import math

import jax
import jax.numpy as jnp
from jax import lax
from jax.experimental import pallas as pl
from jax.experimental.pallas import tpu as pltpu

F32 = jnp.float32
BF16 = jnp.bfloat16

CHUNK = 64
LRU_HEADS = 8
LRU_C = 8.0
DIFF_HEADS = 8
MEM_HEADS = 4
RMS_EPS = 1e-6
NEG_BIG = -1e30
LOG2E = 1.4426950408889634
SUBLANES = 8
MEM_ATTN_ROWS = 512

V7X_VMEM_LIMIT_BYTES = 58 * 1024 * 1024


def _cparams(n_grid_axes):
    return pltpu.CompilerParams(
        dimension_semantics=("arbitrary",) * n_grid_axes,
        vmem_limit_bytes=V7X_VMEM_LIMIT_BYTES,
    )


def _dot(a, b):
    return jnp.dot(a, b, preferred_element_type=F32)


def _dot_nt(a, b):
    return lax.dot_general(a, b, (((1,), (1,)), ((), ())), preferred_element_type=F32)


def _rms(x, g):
    return x * lax.rsqrt(jnp.mean(x * x, axis=-1, keepdims=True) + RMS_EPS) * g


def _gelu(x):
    return x * (0.5 * (1.0 + jnp.tanh(0.7978845608028654 * (x + 0.044715 * (x * x * x)))))


def _cat_rows(parts):
    return parts[0] if len(parts) == 1 else jnp.concatenate(parts, axis=0)


def _delayed(seg, prev, d):
    n_prev = prev.shape[0]
    v = pltpu.roll(seg, d, 0)
    row = lax.broadcasted_iota(jnp.int32, (SUBLANES, 1), 0)
    top = v[0:SUBLANES]
    for r in range(d):
        top = jnp.where(row == r, prev[n_prev - d + r:n_prev - d + r + 1], top)
    return jnp.concatenate([top, v[SUBLANES:]], axis=0)


def _causal_conv(seg, prev, cw, cb):
    n_prev = cw.shape[0] - 1
    y = cb + _delayed(seg, prev, n_prev) * cw[0:1]
    for t in range(1, n_prev):
        y = y + _delayed(seg, prev, n_prev - t) * cw[t:t + 1]
    return y + seg * cw[n_prev:n_prev + 1]


def _scan_rows(a, b):
    n = a.shape[0]
    row = lax.broadcasted_iota(jnp.int32, (n, 1), 0)
    sh = 1
    while sh < min(SUBLANES, n):
        valid = row >= sh
        a_sh = pltpu.roll(a, sh, 0)
        b_sh = pltpu.roll(b, sh, 0)
        b = jnp.where(valid, b + a * b_sh, b)
        a = jnp.where(valid, a * a_sh, a)
        sh *= 2
    while sh < n:
        b = jnp.concatenate([b[:sh], b[sh:] + a[sh:] * b[:n - sh]], axis=0)
        a = jnp.concatenate([a[:sh], a[sh:] * a[:n - sh]], axis=0)
        sh *= 2
    return a, b


def _two_stage(n_chunks, first, second):
    for t in range(n_chunks + 1):
        if t < n_chunks:
            first(t)
        if t >= 1:
            second(t - 1)


def _mem_attn(q, gq, mk_ref, mv_ref, seg0, n_seg, seg_len):
    hd = gq.shape[-1]
    outs = []
    for h in range(q.shape[-1] // hd):
        cols = slice(h * hd, (h + 1) * hd)
        qn = _rms(q[:, cols], gq).astype(BF16)
        segs = []
        for s in range(n_seg):
            k = mk_ref[seg0 + s, :, cols]
            v = mv_ref[seg0 + s, :, cols]
            sc = _dot_nt(qn[s * seg_len:(s + 1) * seg_len], k) * (hd ** -0.5)
            p = jnp.exp(sc - jnp.max(sc, axis=-1, keepdims=True))
            pr = p / jnp.sum(p, axis=-1, keepdims=True)
            segs.append(_dot(pr.astype(BF16), v))
        outs.append(_cat_rows(segs))
    return outs[0] if len(outs) == 1 else jnp.concatenate(outs, axis=-1)


class _Tiling:
    def __init__(self, n_seq, t, bm):
        self.n_seq, self.t = n_seq, t
        if t >= bm:
            assert t % bm == 0
            self.n_seg, self.seg_len, self.tiles_per_seq = 1, bm, t // bm
        else:
            assert bm % t == 0 and n_seq % (bm // t) == 0
            self.n_seg, self.seg_len, self.tiles_per_seq = bm // t, t, 1
        self.bm = self.n_seg * self.seg_len
        self.n_tiles = n_seq * t // self.bm
        assert self.seg_len & (self.seg_len - 1) == 0 and self.seg_len >= SUBLANES


def _last_tile_state(per_tile, tl):
    rows, c = per_tile.shape[2:]
    return per_tile.reshape(tl.n_seq, tl.tiles_per_seq, rows, c)[:, -1]


def _resident(block_shape, index_map):
    return pl.BlockSpec(block_shape, index_map, pipeline_mode=pl.Buffered(1))


def _mem_kv_kernel(mem_ref, gn_ref, wk_ref, wv_ref, gk_ref, k_ref, v_ref, hn_ref):
    @pl.when(pl.program_id(1) == 0)
    def _():
        hn_ref[...] = _rms(mem_ref[...], gn_ref[...]).astype(BF16)

    hn = hn_ref[...]
    k_ref[...] = _rms(_dot(hn, wk_ref[...].astype(BF16)), gk_ref[...])
    v_ref[...] = _dot(hn, wv_ref[...].astype(BF16))


def _mem_kv(mem2d, mem_norm, w_kv, mem_k_norm):
    depth, d, two_w = w_kv.shape
    mem_w = two_w // 2
    hd = mem_w // MEM_HEADS
    m = mem2d.shape[0]
    return pl.pallas_call(
        _mem_kv_kernel,
        grid=(depth, MEM_HEADS),
        in_specs=[
            pl.BlockSpec((m, d), lambda l, j: (0, 0)),
            pl.BlockSpec((None, 1, d), lambda l, j: (l, 0, 0)),
            pl.BlockSpec((None, d, hd), lambda l, j: (l, 0, j)),
            pl.BlockSpec((None, d, hd), lambda l, j: (l, 0, MEM_HEADS + j)),
            pl.BlockSpec((None, 1, hd), lambda l, j: (l, 0, 0)),
        ],
        out_specs=[
            pl.BlockSpec((None, m, hd), lambda l, j: (l, 0, j)),
            pl.BlockSpec((None, m, hd), lambda l, j: (l, 0, j)),
        ],
        out_shape=[jax.ShapeDtypeStruct((depth, m, mem_w), F32)] * 2,
        scratch_shapes=[pltpu.VMEM((m, d), BF16)],
        compiler_params=_cparams(2),
        name="mem_kv",
    )(mem2d, mem_norm, w_kv, w_kv, mem_k_norm)


def _lru_in_kernel(x_ref, gn_ref, wg_ref, wx_ref, cw_ref, cb_ref, wa_ref, wi_ref, ba_ref, bi_ref,
                   lam_ref, h0_ref, c0_ref, gq_ref, mk_ref, mv_ref,
                   mix_ref, mo_ref, hn_out_ref, cn_out_ref,
                   hn_ref, hc_ref, cc_ref, raw_ref, *, n_seg, seg_len, tiles_per_seq, row_chunk):
    i = pl.program_id(0)
    j = pl.program_id(1)
    carried = tiles_per_seq > 1
    heads_per_step, hw = wa_ref.shape[0], wa_ref.shape[1]
    n_lru_steps = LRU_HEADS // heads_per_step

    @pl.when(j == 0)
    def _():
        hn_ref[...] = _rms(x_ref[...], gn_ref[...]).astype(BF16)

    @pl.when(j < n_lru_steps)
    def _():
        if carried:
            @pl.when(i % tiles_per_seq == 0)
            def _():
                cc_ref[j] = c0_ref[0]
                hc_ref[j] = h0_ref[0]

        lanes = pl.ds(pl.multiple_of(j * (heads_per_step * hw), heads_per_step * hw), heads_per_step * hw)
        cw = cw_ref[:, lanes]
        cb = cb_ref[:, lanes]
        ba = ba_ref[:, lanes]
        bi = bi_ref[:, lanes]
        wg, wx, wa, wi = (r[...].astype(BF16) for r in (wg_ref, wx_ref, wa_ref, wi_ref))
        n_prev = cw.shape[0] - 1
        n_chunks = hn_ref.shape[0] // row_chunk
        segs_per_chunk = row_chunk // seg_len if n_seg > 1 else 0
        whole = n_seg == 1
        state = {"conv": (cc_ref[j] if carried else c0_ref[0]) if whole else None,
                 "h": (hc_ref[j] if carried else h0_ref[0]) if whole else None}
        log_lam = jax.nn.log_sigmoid(lam_ref[:, lanes])

        def rows(c):
            return slice(c * row_chunk, (c + 1) * row_chunk)

        def in_proj(c):
            hn = hn_ref[rows(c), :]
            raw_ref[0, rows(c), :] = _dot(hn, wg)
            raw_ref[1, rows(c), :] = _dot(hn, wx)

        def conv_gates(c):
            xr = raw_ref[1, rows(c), :]
            if whole:
                xc = _causal_conv(xr, state["conv"], cw, cb)
                state["conv"] = xr[row_chunk - n_prev:row_chunk]
            else:
                parts = []
                for s in range(segs_per_chunk):
                    sidx = c * segs_per_chunk + s
                    seg = xr[s * seg_len:(s + 1) * seg_len]
                    parts.append(_causal_conv(seg, c0_ref[sidx], cw, cb))
                    cn_out_ref[sidx] = seg[seg_len - n_prev:seg_len]
                xc = _cat_rows(parts)
            xcb = xc.astype(BF16)
            raw_ref[1, rows(c), :] = xc
            for hh in range(heads_per_step):
                cols = slice(hh * hw, (hh + 1) * hw)
                raw_ref[2, rows(c), cols] = _dot(xcb[:, cols], wa[hh])
                raw_ref[3, rows(c), cols] = _dot(xcb[:, cols], wi[hh])

        def recur(c):
            xc = raw_ref[1, rows(c), :]
            r = jax.nn.sigmoid(raw_ref[2, rows(c), :] + ba)
            ig = jax.nn.sigmoid(raw_ref[3, rows(c), :] + bi)
            log_a = (LRU_C * r) * log_lam
            a = jnp.exp(log_a)
            one_minus_a2 = -jnp.tanh(log_a) * (a * a + 1.0)
            gated = jnp.sqrt(one_minus_a2) * (ig * xc)
            if whole:
                cum_a, cum_b = _scan_rows(a, gated)
                hs = cum_b + cum_a * state["h"]
                state["h"] = hs[row_chunk - 1:row_chunk]
            else:
                parts = []
                for s in range(segs_per_chunk):
                    sidx = c * segs_per_chunk + s
                    sl = slice(s * seg_len, (s + 1) * seg_len)
                    cum_a, cum_b = _scan_rows(a[sl], gated[sl])
                    seg_hs = cum_b + cum_a * h0_ref[sidx]
                    hn_out_ref[sidx] = seg_hs[seg_len - 1:seg_len]
                    parts.append(seg_hs)
                hs = _cat_rows(parts)
            mix_ref[rows(c), :] = (_gelu(raw_ref[0, rows(c), :]) * hs).astype(BF16)

        for t in range(n_chunks + 2):
            if t < n_chunks:
                in_proj(t)
            if 0 <= t - 1 < n_chunks:
                conv_gates(t - 1)
            if 0 <= t - 2 < n_chunks:
                recur(t - 2)
        if whole:
            cn_out_ref[0] = state["conv"]
            hn_out_ref[0] = state["h"]
            if carried:
                cc_ref[j] = state["conv"]
                hc_ref[j] = state["h"]

    @pl.when(j >= n_lru_steps)
    def _():
        chunk = max(row_chunk, min(hn_ref.shape[0], MEM_ATTN_ROWS))
        n_chunks = hn_ref.shape[0] // chunk
        segs = (chunk // seg_len, seg_len) if n_seg > 1 else (1, chunk)

        wq = wx_ref[...].astype(BF16)

        def rows(c):
            return slice(c * chunk, (c + 1) * chunk)

        def project(c):
            raw_ref[0, rows(c), :] = _dot(hn_ref[rows(c), :], wq)

        def attend(c):
            seg0 = c * segs[0] if n_seg > 1 else 0
            mo_ref[rows(c), :] = _mem_attn(raw_ref[0, rows(c), :], gq_ref[...], mk_ref, mv_ref,
                                           seg0, *segs).astype(BF16)

        _two_stage(n_chunks, project, attend)


def _lru_in(x2d, tl, li, mi, p, h0, c0, mk, mv, bc, row_chunk):
    m, d = x2d.shape
    hw = d // LRU_HEADS
    mem_w = mk.shape[-1]
    hp = bc // hw
    n_lru, n_mem = d // bc, mem_w // bc
    ns, tps = tl.n_seg, tl.tiles_per_seq
    width = p["lru_conv_w"].shape[1]

    def hcol(j):
        return jnp.minimum(j, n_lru - 1)

    def mcol(j):
        return jnp.maximum(j - n_lru, 0)

    vec = _resident((None, 1, d), lambda i, j: (mi, 0, 0))
    gate_w = pl.BlockSpec((None, hp, hw, hw), lambda i, j: (mi, hcol(j), 0, 0))
    mem = _resident((None, ns, mk.shape[2], bc), lambda i, j: (li, i // tps, 0, mcol(j)))
    assert bc % hw == 0 and d % bc == 0 and mem_w % bc == 0
    assert tl.bm % row_chunk == 0 and (row_chunk % tl.seg_len == 0 if ns > 1 else True)
    kern = functools.partial(_lru_in_kernel, n_seg=ns, seg_len=tl.seg_len, tiles_per_seq=tps,
                             row_chunk=row_chunk)
    return pl.pallas_call(
        kern,
        grid=(tl.n_tiles, n_lru + n_mem),
        in_specs=[
            _resident((tl.bm, d), lambda i, j: (i, 0)),
            pl.BlockSpec((None, 1, d), lambda i, j: (li, 0, 0)),
            pl.BlockSpec((None, d, bc), lambda i, j: (mi, 0, hcol(j))),
            pl.BlockSpec((None, d, bc), lambda i, j: (mi, 0, n_lru + j)),
            _resident((None, width, d), lambda i, j: (mi, 0, 0)),
            vec, gate_w, gate_w, vec, vec, vec,
            pl.BlockSpec((None, ns, 1, bc), lambda i, j: (mi, i // tps, 0, hcol(j))),
            pl.BlockSpec((None, ns, width - 1, bc), lambda i, j: (mi, i // tps, 0, hcol(j))),
            pl.BlockSpec((None, 1, hw), lambda i, j: (li, 0, 0)),
            mem, mem,
        ],
        out_specs=[
            pl.BlockSpec((tl.bm, bc), lambda i, j: (i, hcol(j))),
            pl.BlockSpec((tl.bm, bc), lambda i, j: (i, mcol(j))),
            pl.BlockSpec((None, ns, 1, bc), lambda i, j: (i, 0, 0, hcol(j))),
            pl.BlockSpec((None, ns, width - 1, bc), lambda i, j: (i, 0, 0, hcol(j))),
        ],
        out_shape=[
            jax.ShapeDtypeStruct((m, d), BF16),
            jax.ShapeDtypeStruct((m, mem_w), BF16),
            jax.ShapeDtypeStruct((tl.n_tiles, ns, 1, d), F32),
            jax.ShapeDtypeStruct((tl.n_tiles, ns, width - 1, d), F32),
        ],
        scratch_shapes=[
            pltpu.VMEM((tl.bm, d), BF16),
            pltpu.VMEM((n_lru, 1, bc), F32),
            pltpu.VMEM((n_lru, width - 1, bc), F32),
            pltpu.VMEM((4, tl.bm, bc), F32),
        ],
        compiler_params=_cparams(2),
        name="lru_in",
    )(x2d, p["norm_mix"], p["lru_w_in"], p["lru_w_in"], p["lru_conv_w"], p["lru_conv_b"],
      p["lru_gate_a_w"], p["lru_gate_x_w"], p["lru_gate_a_b"], p["lru_gate_x_b"], p["lru_lambda"],
      h0, c0, p["mem_q_norm"], mk, mv)


def _att_in_kernel(*refs, n_seg, seg_len, nq, row_chunk, aliased):
    if aliased:
        refs = refs[:8] + refs[10:]
    (x_ref, gn_ref, w_ref, qg_ref, kg_ref, gq_ref, mk_ref, mv_ref,
     qb_ref, kf_ref, kb_ref, vf_ref, vb_ref, mo_ref, hn_ref, z_ref) = refs
    j = pl.program_id(1)

    @pl.when(j == 0)
    def _():
        hn_ref[...] = _rms(x_ref[...], gn_ref[...]).astype(BF16)

    dk = qg_ref.shape[-1]
    n_groups = z_ref.shape[-1] // dk
    rows_per_t = hn_ref.shape[-1] // dk
    n_heads = rows_per_t // 2
    n_chunks = hn_ref.shape[0] // row_chunk
    w = w_ref[...].astype(BF16)

    def rows(c):
        return slice(c * row_chunk, (c + 1) * row_chunk)

    def project(c):
        z_ref[rows(c), :] = _dot(hn_ref[rows(c), :], w)

    def queries(c):
        qscale = dk ** -0.5 * LOG2E
        for g in range(n_groups):
            sl = slice(g * dk, (g + 1) * dk)
            qb_ref[rows(c), sl] = (_rms(z_ref[rows(c), sl], qg_ref[...]) * qscale).astype(BF16)

    def keys(c):
        kns = []
        for g in range(n_groups):
            sl = slice(g * dk, (g + 1) * dk)
            kn = _rms(z_ref[rows(c), sl], kg_ref[...])
            kb_ref[rows(c), sl] = kn.astype(BF16)
            kns.append(kn)
        first = pl.multiple_of((j - nq) * n_groups, n_groups)
        kf_ref[rows(c), pl.ds(first, n_groups), :] = (
            jnp.concatenate(kns, axis=-1).reshape(row_chunk, n_groups, dk))

    def values(c):
        z = z_ref[rows(c), :]
        vb_ref[rows(c), :] = z.astype(BF16)
        heads = n_groups // 2
        head0 = pl.multiple_of((j - 2 * nq) * heads, heads)
        for half in range(2):
            part = jnp.concatenate([z[:, (2 * h + half) * dk:(2 * h + half + 1) * dk] for h in range(heads)],
                                   axis=-1)
            vf_ref[rows(c), pl.ds(half * n_heads + head0, heads), :] = part.reshape(row_chunk, heads, dk)

    mem_chunk = max(row_chunk, min(hn_ref.shape[0], MEM_ATTN_ROWS))
    mem_segs = (mem_chunk // seg_len, seg_len) if n_seg > 1 else (1, mem_chunk)

    def mem_rows(c):
        return slice(c * mem_chunk, (c + 1) * mem_chunk)

    def mem_project(c):
        z_ref[mem_rows(c), :] = _dot(hn_ref[mem_rows(c), :], w)

    def memory(c):
        seg0 = c * mem_segs[0] if n_seg > 1 else 0
        mo_ref[mem_rows(c), :] = _mem_attn(z_ref[mem_rows(c), :], gq_ref[...], mk_ref, mv_ref,
                                           seg0, *mem_segs).astype(BF16)

    @pl.when(j < nq)
    def _():
        _two_stage(n_chunks, project, queries)

    @pl.when((j >= nq) & (j < 2 * nq))
    def _():
        _two_stage(n_chunks, project, keys)

    @pl.when((j >= 2 * nq) & (j < 3 * nq))
    def _():
        _two_stage(n_chunks, project, values)

    @pl.when(j >= 3 * nq)
    def _():
        _two_stage(hn_ref.shape[0] // mem_chunk, mem_project, memory)


def _att_in(x2d, tl, li, ai, p, mk, mv, bc, row_chunk, kv_all):
    m, d = x2d.shape
    mem_w = mk.shape[-1]
    nq, nm = d // bc, mem_w // bc
    n_steps = 3 * nq + nm
    ns, tps = tl.n_seg, tl.tiles_per_seq
    dk = p["attn_q_norm"].shape[-1]
    hd = p["mem_q_norm"].shape[-1]

    def col(lo, n):
        return lambda i, j: (i, jnp.clip(j - lo, 0, n - 1))

    mem = _resident((None, ns, mk.shape[2], bc),
                    lambda i, j: (li, i // tps, 0, jnp.clip(j - 3 * nq, 0, nm - 1)))
    assert tl.bm % row_chunk == 0 and (row_chunk % tl.seg_len == 0 if ns > 1 else True)
    n_attn = p["attn_w_in"].shape[0]
    rows_per_t = d // dk
    aliased = kv_all is not None
    kern = functools.partial(_att_in_kernel, n_seg=ns, seg_len=tl.seg_len, nq=nq, row_chunk=row_chunk,
                             aliased=aliased)
    kv_spec = _resident((None, tl.bm, rows_per_t, dk), lambda i, j: (ai, i, 0, 0))
    kv_shape = jax.ShapeDtypeStruct((n_attn, m, rows_per_t, dk), F32)
    extra_in = [pl.BlockSpec(memory_space=pl.ANY)] * 2 if aliased else []
    return pl.pallas_call(
        kern,
        grid=(tl.n_tiles, n_steps),
        in_specs=[
            _resident((tl.bm, d), lambda i, j: (i, 0)),
            pl.BlockSpec((None, 1, d), lambda i, j: (li, 0, 0)),
            pl.BlockSpec((None, d, bc), lambda i, j: (ai, 0, j)),
            pl.BlockSpec((None, 1, dk), lambda i, j: (ai, 0, 0)),
            pl.BlockSpec((None, 1, dk), lambda i, j: (ai, 0, 0)),
            pl.BlockSpec((None, 1, hd), lambda i, j: (li, 0, 0)),
            mem, mem, *extra_in,
        ],
        out_specs=[
            pl.BlockSpec((tl.bm, bc), col(0, nq)),
            kv_spec,
            pl.BlockSpec((tl.bm, bc), col(nq, nq)),
            kv_spec,
            pl.BlockSpec((tl.bm, bc), col(2 * nq, nq)),
            pl.BlockSpec((tl.bm, bc), col(3 * nq, nm)),
        ],
        out_shape=[
            jax.ShapeDtypeStruct((m, d), BF16),
            kv_shape,
            jax.ShapeDtypeStruct((m, d), BF16),
            kv_shape,
            jax.ShapeDtypeStruct((m, d), BF16),
            jax.ShapeDtypeStruct((m, mem_w), BF16),
        ],
        input_output_aliases={8: 1, 9: 3} if aliased else {},
        scratch_shapes=[pltpu.VMEM((tl.bm, d), BF16), pltpu.VMEM((tl.bm, bc), F32)],
        compiler_params=_cparams(2),
        name="att_in",
    )(x2d, p["norm_mix"], p["attn_w_in"], p["attn_q_norm"], p["attn_k_norm"], p["mem_q_norm"], mk, mv,
      *(kv_all if aliased else ()))


def _attn_block(q, k, v, m_ref, l_ref, acc_ref, mask, lanes=slice(None)):
    dk = q.shape[-1] // 2
    tk = k.shape[0]
    scores = [_dot_nt(k[:, c * dk:(c + 1) * dk], q[:, c * dk:(c + 1) * dk]) for c in range(2)]
    for c in range(2):
        s = scores[c]
        if mask is not None:
            s = jnp.where(mask, s, NEG_BIG)
        m_old = m_ref[c, :, lanes]
        m_new = jnp.maximum(m_old, jnp.max(s, axis=0, keepdims=True))
        alpha = jnp.exp2(m_old - m_new)
        p = jnp.exp2(s - m_new)
        l_ref[c, :, lanes] = (alpha * l_ref[c, :, lanes]
                              + jnp.sum(p.reshape(tk // SUBLANES, SUBLANES, p.shape[-1]), axis=0))
        pv = lax.dot_general(v, p.astype(BF16), (((0,), (0,)), ((), ())), preferred_element_type=F32)
        acc_ref[c, :, lanes] = alpha * acc_ref[c, :, lanes] + pv
        m_ref[c, :, lanes] = m_new


def _attn_init(m_ref, l_ref, acc_ref):
    m_ref[...] = jnp.full(m_ref.shape, NEG_BIG, F32)
    l_ref[...] = jnp.zeros(l_ref.shape, F32)
    acc_ref[...] = jnp.zeros(acc_ref.shape, F32)


def _attn_finish(lp_ref, sg_ref, l_ref, acc_ref, lam_init):
    lp = lp_ref[...]
    lam = (jnp.exp(jnp.sum(lp[0:1] * lp[1:2], axis=-1, keepdims=True))
           - jnp.exp(jnp.sum(lp[2:3] * lp[3:4], axis=-1, keepdims=True)) + lam_init)
    l0 = jnp.sum(l_ref[0], axis=0, keepdims=True)
    l1 = jnp.sum(l_ref[1], axis=0, keepdims=True)
    o = (acc_ref[0] / l0 - lam * (acc_ref[1] / l1)).T
    return (_rms(o, sg_ref[...]) * (1.0 - lam_init)).astype(BF16)


def _attn_prompt_kernel(lp_ref, sg_ref, q_ref, k_ref, v_ref, o_ref, m_ref, l_ref, acc_ref, *, tq, tk, lam_init):
    qi = pl.program_id(2)
    _attn_init(m_ref, l_ref, acc_ref)
    q = q_ref[...]
    per = tk // tq

    def visible(start, size):
        start = pl.multiple_of(start, size)
        _attn_block(q, k_ref[pl.ds(start, size), :], v_ref[pl.ds(start, size), :], m_ref, l_ref, acc_ref, None)

    def body(kv, carry):
        visible(kv * tk, tk)
        return carry

    n_big = qi // per
    lax.fori_loop(0, n_big, body, 0)
    for r in range(1, per):
        @pl.when(qi % per >= r)
        def _():
            visible(n_big * tk + (r - 1) * tq, tq)

    half = tq // 2
    shift = CHUNK.bit_length() - 1

    def chunk_mask(n_keys, n_queries):
        keyc = lax.shift_right_logical(lax.broadcasted_iota(jnp.int32, (n_keys, n_queries), 0), shift)
        qryc = lax.shift_right_logical(lax.broadcasted_iota(jnp.int32, (n_keys, n_queries), 1), shift)
        return keyc <= qryc

    start = pl.multiple_of(qi * tq, tq)
    _attn_block(q, k_ref[pl.ds(start, half), :], v_ref[pl.ds(start, half), :], m_ref, l_ref, acc_ref,
                chunk_mask(half, tq))
    start = pl.multiple_of(qi * tq + half, half)
    _attn_block(q[half:], k_ref[pl.ds(start, half), :], v_ref[pl.ds(start, half), :], m_ref, l_ref, acc_ref,
                chunk_mask(half, half), lanes=slice(half, tq))
    o_ref[...] = _attn_finish(lp_ref, sg_ref, l_ref, acc_ref, lam_init)


def _attn_prompt(qb, kb, vb, ai, p, n_seq, t, tq, tk, lam_init):
    d = qb.shape[-1]
    hw = d // DIFF_HEADS
    assert t % tq == 0 and tq % (2 * CHUNK) == 0 and tk % tq == 0
    q3, k3, v3 = (a.reshape(n_seq, t, d) for a in (qb, kb, vb))
    lp, sg = p["attn_lambda"], p["attn_subln"]
    kern = functools.partial(_attn_prompt_kernel, tq=tq, tk=tk, lam_init=lam_init)
    out = pl.pallas_call(
        kern,
        grid=(n_seq, DIFF_HEADS, t // tq),
        in_specs=[
            pl.BlockSpec((None,) + lp.shape[1:], lambda b, h, qi: (ai, 0, 0)),
            pl.BlockSpec((None, 1, hw), lambda b, h, qi: (ai, 0, 0)),
            pl.BlockSpec((None, tq, hw), lambda b, h, qi: (b, qi, h)),
            pl.BlockSpec((None, t, hw), lambda b, h, qi: (b, 0, h)),
            pl.BlockSpec((None, t, hw), lambda b, h, qi: (b, 0, h)),
        ],
        out_specs=pl.BlockSpec((None, tq, hw), lambda b, h, qi: (b, qi, h)),
        out_shape=jax.ShapeDtypeStruct((n_seq, t, d), BF16),
        scratch_shapes=[
            pltpu.VMEM((2, 1, tq), F32),
            pltpu.VMEM((2, SUBLANES, tq), F32),
            pltpu.VMEM((2, hw, tq), F32),
        ],
        compiler_params=_cparams(3),
        name="attn_prompt",
    )(lp, sg, q3, k3, v3)
    return out.reshape(n_seq * t, d)


def _attn_cached_kernel(lp_ref, sg_ref, q_ref, kp_ref, vp_ref, kn_ref, vn_ref, o_ref,
                        m_ref, l_ref, acc_ref, kbuf_ref, vbuf_ref, *, lam_init):
    n_heads = DIFF_HEADS
    past, rows_per_t, dk = kp_ref.shape
    hw = 2 * dk
    group = SUBLANES
    for g in range(rows_per_t // group):
        fold = slice(g * group * dk, (g + 1) * group * dk)
        kbuf_ref[:, fold] = kp_ref[:, g * group:(g + 1) * group, :].reshape(past, group * dk).astype(BF16)
        vbuf_ref[:, fold] = vp_ref[:, g * group:(g + 1) * group, :].reshape(past, group * dk).astype(BF16)
    for h in range(n_heads):
        cols = slice(h * hw, (h + 1) * hw)
        v = jnp.concatenate([vbuf_ref[:, (c * n_heads + h) * dk:(c * n_heads + h + 1) * dk] for c in range(2)],
                            axis=-1)
        q = q_ref[:, cols]
        _attn_init(m_ref, l_ref, acc_ref)
        _attn_block(q, kbuf_ref[:, cols], v, m_ref, l_ref, acc_ref, None)
        _attn_block(q, kn_ref[:, cols], vn_ref[:, cols], m_ref, l_ref, acc_ref, None)
        o_ref[:, cols] = _attn_finish(lp_ref, sg_ref, l_ref, acc_ref, lam_init)


def _attn_cached(qb, kb, vb, k_past, v_past, ai, p, n_seq, t, lam_init):
    d = qb.shape[-1]
    hw = d // DIFF_HEADS
    n_attn, _, past, n_heads, _, dk = k_past.shape
    assert past % CHUNK == 0 and t <= CHUNK and n_heads == DIFF_HEADS and 2 * dk == hw
    q3, k3, v3 = (a.reshape(n_seq, t, d) for a in (qb, kb, vb))
    lp, sg = p["attn_lambda"], p["attn_subln"]
    k_rows = k_past.reshape(n_attn * n_seq, past, 2 * n_heads, dk)
    v_rows = v_past.reshape(n_attn, n_seq, past, n_heads, 2, dk).transpose(0, 1, 2, 4, 3, 5)
    v_rows = v_rows.reshape(n_attn * n_seq, past, 2 * n_heads, dk)
    kern = functools.partial(_attn_cached_kernel, lam_init=lam_init)
    new_spec = pl.BlockSpec((None, t, d), lambda b: (b, 0, 0))
    past_spec = pl.BlockSpec((None, past, 2 * n_heads, dk), lambda b: (ai * n_seq + b, 0, 0, 0))
    out = pl.pallas_call(
        kern,
        grid=(n_seq,),
        in_specs=[
            pl.BlockSpec((None,) + lp.shape[1:], lambda b: (ai, 0, 0)),
            pl.BlockSpec((None, 1, hw), lambda b: (ai, 0, 0)),
            new_spec, past_spec, past_spec, new_spec, new_spec,
        ],
        out_specs=new_spec,
        out_shape=jax.ShapeDtypeStruct((n_seq, t, d), BF16),
        scratch_shapes=[
            pltpu.VMEM((2, 1, t), F32),
            pltpu.VMEM((2, SUBLANES, t), F32),
            pltpu.VMEM((2, hw, t), F32),
            pltpu.VMEM((past, d), BF16),
            pltpu.VMEM((past, d), BF16),
        ],
        compiler_params=_cparams(1),
        name="attn_cached",
    )(lp, sg, q3, k_rows, v_rows, k3, v3)
    return out.reshape(n_seq * t, d)


def _out_kernel(x_ref, mix_ref, mo_ref, w1_ref, w2_ref, o_ref):
    o_ref[...] = x_ref[...] + (_dot(mix_ref[...], w1_ref[...]) + _dot(mo_ref[...], w2_ref[...]))


def _out_proj(x2d, mix, mo, w_out, wi, bm, bn):
    m, d = x2d.shape
    k1, k2 = mix.shape[1], mo.shape[1]
    assert k1 % k2 == 0
    return pl.pallas_call(
        _out_kernel,
        grid=(m // bm, d // bn),
        in_specs=[
            pl.BlockSpec((bm, bn), lambda i, j: (i, j)),
            pl.BlockSpec((bm, k1), lambda i, j: (i, 0)),
            pl.BlockSpec((bm, k2), lambda i, j: (i, 0)),
            pl.BlockSpec((None, k1, bn), lambda i, j: (wi, 0, j)),
            pl.BlockSpec((None, k2, bn), lambda i, j: (wi, k1 // k2, j)),
        ],
        out_specs=pl.BlockSpec((bm, bn), lambda i, j: (i, j)),
        out_shape=jax.ShapeDtypeStruct((m, d), F32),
        compiler_params=_cparams(2),
        name="out_proj",
    )(x2d, mix, mo, w_out, w_out)


def _ffn_kernel(x_ref, gn_ref, wg_ref, wv_ref, cw_ref, cb_ref, stg_ref, stv_ref, wdn_ref,
                o_ref, nstg_ref, nstv_ref, hn_ref, cg_ref, cv_ref, raw_ref, act_ref,
                *, n_seg, seg_len, tiles_per_seq, row_chunk):
    i = pl.program_id(0)
    j = pl.program_id(1)
    carried = tiles_per_seq > 1
    bf = wg_ref.shape[-1]
    dff = cw_ref.shape[-1] // 2
    col = pl.multiple_of(j * bf, bf)
    conv_params = [(cw_ref[:, pl.ds(off + col, bf)], cb_ref[:, pl.ds(off + col, bf)]) for off in (0, dff)]

    @pl.when(j == 0)
    def _():
        x = x_ref[...]
        hn_ref[...] = _rms(x, gn_ref[...]).astype(BF16)
        o_ref[...] = x

    if carried:
        @pl.when(i % tiles_per_seq == 0)
        def _():
            cg_ref[j] = stg_ref[0]
            cv_ref[j] = stv_ref[0]

    halves = ((wg_ref, *conv_params[0], stg_ref, cg_ref, nstg_ref),
              (wv_ref, *conv_params[1], stv_ref, cv_ref, nstv_ref))
    n_prev = cw_ref.shape[0] - 1
    n_chunks = hn_ref.shape[0] // row_chunk
    segs_per_chunk = row_chunk // seg_len if n_seg > 1 else 0
    prevs = [(h[4][j] if carried else h[3][0]) if n_seg == 1 else None for h in halves]

    def rows(c):
        return slice(c * row_chunk, (c + 1) * row_chunk)

    def up_proj(c):
        hn = hn_ref[rows(c), :]
        for hi, h in enumerate(halves):
            raw_ref[hi, rows(c), :] = _dot(hn, h[0][...])

    def conv_gate(c):
        ys = []
        for hi, (_, cw, cb, st_ref, _, nst_ref) in enumerate(halves):
            up = raw_ref[hi, rows(c), :]
            if n_seg == 1:
                ys.append(_causal_conv(up, prevs[hi], cw, cb))
                prevs[hi] = up[row_chunk - n_prev:row_chunk]
            else:
                parts = []
                for s in range(segs_per_chunk):
                    seg = up[s * seg_len:(s + 1) * seg_len]
                    parts.append(_causal_conv(seg, st_ref[c * segs_per_chunk + s], cw, cb))
                    nst_ref[c * segs_per_chunk + s] = seg[seg_len - n_prev:seg_len]
                ys.append(_cat_rows(parts))
        act_ref[rows(c), :] = (_gelu(ys[0]) * ys[1]).astype(BF16)

    def down_proj(c):
        o_ref[rows(c), :] += _dot(act_ref[rows(c), :], wdn_ref[...])

    for t in range(n_chunks + 2):
        if t < n_chunks:
            up_proj(t)
        if 0 <= t - 1 < n_chunks:
            conv_gate(t - 1)
        if 0 <= t - 2 < n_chunks:
            down_proj(t - 2)
    if n_seg == 1:
        for hi, (_, _, _, _, carry_ref, nst_ref) in enumerate(halves):
            nst_ref[0] = prevs[hi]
            if carried:
                carry_ref[j] = prevs[hi]


def _ffn(x2d, tl, li, p, state, bf, row_chunk):
    m, d = x2d.shape
    dff = p["ffn_w_down"].shape[1]
    nj = dff // bf
    ns, tps = tl.n_seg, tl.tiles_per_seq
    width = p["ffn_conv_w"].shape[1]
    assert tl.bm % row_chunk == 0 and (row_chunk % tl.seg_len == 0 if ns > 1 else True)
    kern = functools.partial(_ffn_kernel, n_seg=ns, seg_len=tl.seg_len, tiles_per_seq=tps, row_chunk=row_chunk)

    def halves(block, index_map):
        gate = pl.BlockSpec(block, lambda i, j: index_map(i, j, j))
        value = pl.BlockSpec(block, lambda i, j: index_map(i, j, nj + j))
        return [gate, value]

    nst_spec = pl.BlockSpec((None, ns, width - 1, bf), lambda i, j: (i, 0, 0, j))
    nst_shape = jax.ShapeDtypeStruct((tl.n_tiles, ns, width - 1, dff), F32)
    return pl.pallas_call(
        kern,
        grid=(tl.n_tiles, nj),
        in_specs=[
            _resident((tl.bm, d), lambda i, j: (i, 0)),
            pl.BlockSpec((None, 1, d), lambda i, j: (li, 0, 0)),
            *halves((None, d, bf), lambda i, j, c: (li, 0, c)),
            _resident((None, width, 2 * dff), lambda i, j: (li, 0, 0)),
            _resident((None, 1, 2 * dff), lambda i, j: (li, 0, 0)),
            *halves((None, ns, width - 1, bf), lambda i, j, c: (li, i // tps, 0, c)),
            pl.BlockSpec((None, bf, d), lambda i, j: (li, j, 0)),
        ],
        out_specs=[pl.BlockSpec((tl.bm, d), lambda i, j: (i, 0)), nst_spec, nst_spec],
        out_shape=[jax.ShapeDtypeStruct((m, d), F32), nst_shape, nst_shape],
        scratch_shapes=[
            pltpu.VMEM((tl.bm, d), BF16),
            pltpu.VMEM((nj, width - 1, bf), F32),
            pltpu.VMEM((nj, width - 1, bf), F32),
            pltpu.VMEM((2, tl.bm, bf), F32),
            pltpu.VMEM((tl.bm, bf), BF16),
        ],
        compiler_params=_cparams(2),
        name="ffn",
    )(x2d, p["norm_ffn"], p["ffn_w_up"], p["ffn_w_up"], p["ffn_conv_w"], p["ffn_conv_b"],
      state, state, p["ffn_w_down"])


def _run_trunk(x, mem_k, mem_v, lru_h, lru_conv, attn_past, ffn_conv, p, cfg):
    n_seq, t, d = x.shape
    depth = p["norm_mix"].shape[0]
    x2d = x.reshape(n_seq * t, d)
    tl = _Tiling(n_seq, t, cfg["bm"])
    tl_ffn = _Tiling(n_seq, t, cfg["bm_ffn"])
    mem_tokens = mem_k.shape[2]
    mk = mem_k.reshape(depth, n_seq, mem_tokens, -1).astype(BF16)
    mv = mem_v.reshape(depth, n_seq, mem_tokens, -1).astype(BF16)
    h0 = lru_h[:, :, None, :]
    new_h, new_lconv, new_fconv = [], [], []
    kv_all = None
    for i in range(depth):
        j = i // 2
        if i % 2 == 0:
            mix, mo, h_last, buf = _lru_in(x2d, tl, i, j, p, h0, lru_conv, mk, mv, cfg["lru_bc"], cfg["lru_rows"])
            new_h.append(_last_tile_state(h_last, tl)[:, 0, :])
            new_lconv.append(_last_tile_state(buf, tl))
            w_out = p["lru_w_out"]
        else:
            qb, k_all, kb, v_all, vb, mo = _att_in(x2d, tl, i, j, p, mk, mv, cfg["bc"], cfg["att_rows"], kv_all)
            kv_all = (k_all, v_all)
            lam_init = 0.8 - 0.6 * math.exp(-0.3 * i)
            if attn_past is None:
                mix = _attn_prompt(qb, kb, vb, j, p, n_seq, t, cfg["tq"], cfg["tk"], lam_init)
            else:
                mix = _attn_cached(qb, kb, vb, attn_past[0], attn_past[1], j, p, n_seq, t, lam_init)
            w_out = p["attn_w_out"]
        x2d = _out_proj(x2d, mix, mo, w_out, j, cfg["bm_out"], cfg["bn_out"])
        x2d, fg, fv = _ffn(x2d, tl_ffn, i, p, ffn_conv, cfg["bf"], cfg["ffn_rows"])
        new_fconv.append(jnp.concatenate([_last_tile_state(fg, tl_ffn), _last_tile_state(fv, tl_ffn)], axis=-1))
    return x2d.reshape(n_seq, t, d), new_h, new_lconv, kv_all[0], kv_all[1], new_fconv


def _forward(x_prompt, x_sample, mem_prompt, cache_attn_k, cache_attn_v, cache_mem_k, cache_mem_v,
             state_lru_h, state_lru_conv, state_ffn_conv, p, mem_norm, mem_w_kv, mem_k_norm, cfg_p, cfg_s):
    b, t, d = x_prompt.shape
    depth = p["norm_mix"].shape[0]
    n_lru, n_attn = (depth + 1) // 2, depth // 2

    p = dict(p)
    for name in ("ffn_w_up", "ffn_w_down", "lru_w_out", "attn_w_out"):
        p[name] = p[name].astype(BF16)
    for name in ("norm_mix", "norm_ffn", "lru_conv_b", "lru_gate_a_b", "lru_gate_x_b", "lru_lambda",
                 "attn_q_norm", "attn_k_norm", "attn_subln", "mem_q_norm", "ffn_conv_b"):
        p[name] = p[name][:, None, :]

    mem_tokens = mem_prompt.shape[1]
    mk, mv = _mem_kv(mem_prompt.reshape(b * mem_tokens, d), mem_norm[:, None, :], mem_w_kv,
                     mem_k_norm[:, None, :])
    hd = mem_k_norm.shape[-1]
    p_mem_k = mk.reshape(depth, b, mem_tokens, MEM_HEADS, hd)
    p_mem_v = mv.reshape(depth, b, mem_tokens, MEM_HEADS, hd)

    zeros_h = jnp.zeros((n_lru, b, d), F32)
    zeros_lconv = jnp.zeros((n_lru, b) + state_lru_conv.shape[2:], F32)
    zeros_fconv = jnp.zeros((depth, b) + state_ffn_conv.shape[2:], F32)
    yp, ph, plc, pk, pv, pfc = _run_trunk(x_prompt, p_mem_k, p_mem_v, zeros_h, zeros_lconv, None,
                                          zeros_fconv, p, cfg_p)
    ys, sh, slc, sk, sv, sfc = _run_trunk(x_sample, cache_mem_k, cache_mem_v, state_lru_h, state_lru_conv,
                                          (cache_attn_k, cache_attn_v), state_ffn_conv, p, cfg_s)

    dk = p["attn_q_norm"].shape[-1]
    db, dt = x_sample.shape[0], x_sample.shape[1]

    def kshape(a, n, tt):
        return a.reshape(n_attn, n, tt, DIFF_HEADS, 2, dk)

    def vshape(a, n, tt):
        a = a.reshape(n_attn, n, tt, 2, DIFF_HEADS, dk)
        return a.transpose(0, 1, 2, 4, 3, 5).reshape(n_attn, n, tt, DIFF_HEADS, 2 * dk)

    return (yp, ys, jnp.stack(ph), jnp.stack(plc), kshape(pk, b, t), vshape(pv, b, t), p_mem_k, p_mem_v,
            jnp.stack(pfc), jnp.stack(sh), jnp.stack(slc), kshape(sk, db, dt), vshape(sv, db, dt),
            jnp.stack(sfc))


CFG_PROMPT = dict(bm=1024, bc=512, lru_bc=512, lru_rows=128, att_rows=256, bm_ffn=1024, bf=512, ffn_rows=512,
                  bm_out=1024, bn_out=1024, tq=1024, tk=1024)
CFG_SAMPLE = dict(bm=1024, bc=512, lru_bc=512, lru_rows=128, att_rows=256, bm_ffn=1024, bf=512, ffn_rows=512,
                  bm_out=1024, bn_out=1024)


def kernel(x_prompt, x_sample, mem_prompt, cache_attn_k, cache_attn_v, cache_mem_k, cache_mem_v, state_lru_h, state_lru_conv, state_ffn_conv, norm_mix, norm_ffn, lru_w_in, lru_conv_w, lru_conv_b, lru_gate_a_w, lru_gate_a_b, lru_gate_x_w, lru_gate_x_b, lru_lambda, lru_w_out, attn_w_in, attn_q_norm, attn_k_norm, attn_lambda, attn_subln, attn_w_out, mem_norm, mem_w_kv, mem_q_norm, mem_k_norm, ffn_w_up, ffn_conv_w, ffn_conv_b, ffn_w_down):
    p = {
        "norm_mix": norm_mix, "norm_ffn": norm_ffn,
        "lru_w_in": lru_w_in, "lru_conv_w": lru_conv_w, "lru_conv_b": lru_conv_b,
        "lru_gate_a_w": lru_gate_a_w, "lru_gate_a_b": lru_gate_a_b,
        "lru_gate_x_w": lru_gate_x_w, "lru_gate_x_b": lru_gate_x_b,
        "lru_lambda": lru_lambda, "lru_w_out": lru_w_out,
        "attn_w_in": attn_w_in, "attn_q_norm": attn_q_norm, "attn_k_norm": attn_k_norm,
        "attn_lambda": attn_lambda, "attn_subln": attn_subln, "attn_w_out": attn_w_out,
        "mem_q_norm": mem_q_norm,
        "ffn_w_up": ffn_w_up, "ffn_conv_w": ffn_conv_w, "ffn_conv_b": ffn_conv_b, "ffn_w_down": ffn_w_down,
    }
    return _forward(x_prompt, x_sample, mem_prompt, cache_attn_k, cache_attn_v, cache_mem_k, cache_mem_v,
                    state_lru_h, state_lru_conv, state_ffn_conv, p, mem_norm, mem_w_kv, mem_k_norm,
                    CFG_PROMPT, CFG_SAMPLE)
```
